```python
import math
import jax, jax.numpy as jnp
from jax import lax
import numpy as np

D_MODEL = 1024
BATCH = 32
SEQ = 256
DEPTH = 2
DEC_BATCH = 8
DEC_SEQ = 1024
PAST_LEN = 512

GRID_W = 64
D_FF = 2816
CONV_W = 3
W_A = 512
H_D = 4
DK_D = 128
DV_D = 128
CHUNK_D = 64
H_G = 4
DK_G = 64
DV_G = 128
GLA_RANK = 16
GLA_TAU = 16.0
CHUNK_G = 16
N_BRANCH = 3
N_ADA = 9
ALPHA = float((2 * DEPTH) ** 0.25)
BETA_INIT = float((8 * DEPTH) ** -0.25)
LN_EPS = 1e-5
RMS_EPS = 1e-6
PROJ_WIDTHS = (W_A, W_A, W_A,
               H_D * DK_D, H_D * DK_D, H_D * DV_D, H_D * DV_D, 2 * H_D, 2 * H_D,
               H_G * DK_G, H_G * DK_G, H_G * DV_G, H_G * DV_G, 2 * GLA_RANK,
               N_BRANCH * D_MODEL)
D_PROJ = sum(PROJ_WIDTHS)
PROJ_SPLITS = tuple(int(s) for s in np.cumsum(PROJ_WIDTHS)[:-1])

kernel_name = 'hybrid_diffusion_conv_deltanet_gla_step'

F32 = jnp.float32


def layer_norm(x, g, b):
    xf = x.astype(F32)
    mu = jnp.mean(xf, -1, keepdims=True)
    var = jnp.mean(jnp.square(xf - mu), -1, keepdims=True)
    return ((xf - mu) * lax.rsqrt(var + LN_EPS) * g.astype(F32) + b.astype(F32)).astype(x.dtype)


def rms_norm(x, g):
    return x * lax.rsqrt(jnp.mean(jnp.square(x), -1, keepdims=True) + RMS_EPS) * g.astype(F32)


def l2_normalize(x):
    return x * lax.rsqrt(jnp.sum(jnp.square(x), -1, keepdims=True) + RMS_EPS)


def swiglu(x, w1, w2):
    gate, up = jnp.split(x @ w1, 2, axis=-1)
    return (jax.nn.silu(gate) * up) @ w2


def centred_conv(x, w, rows):
    b, l, ch = x.shape
    seq = x if rows is None else x.reshape(b * rows, GRID_W, ch)
    pad = CONV_W // 2
    n = seq.shape[1]
    xp = jnp.pad(seq, ((0, 0), (pad, pad), (0, 0)))
    y = sum(xp[:, j:j + n] * w[j] for j in range(CONV_W))
    return y.reshape(b, l, ch)


def to_heads(t, n_heads):
    b, l, _ = t.shape
    return t.reshape(b, l, n_heads, -1).transpose(0, 2, 1, 3).astype(F32)


def flip_seq(t):
    return jnp.flip(t, axis=2)


def gated_delta_chunked(q, k, v, g, beta, s0):
    b, h, l, dk = q.shape
    dv = v.shape[-1]
    c = CHUNK_D
    n = l // c
    q = q.reshape(b, h, n, c, dk)
    k = k.reshape(b, h, n, c, dk)
    v = v.reshape(b, h, n, c, dv)
    beta = beta.reshape(b, h, n, c)
    G = jnp.cumsum(g.reshape(b, h, n, c), axis=-1)
    idx = jnp.arange(c)
    causal = idx[:, None] >= idx[None, :]
    strict = idx[:, None] > idx[None, :]
    gamma = jnp.exp(jnp.where(causal, G[..., :, None] - G[..., None, :], -jnp.inf))
    kb = k * beta[..., None]
    m = jnp.where(strict, jnp.einsum('bhnid,bhnjd->bhnij', kb, k) * gamma, 0.0)
    eye = jnp.eye(c, dtype=F32)
    rhs = jnp.concatenate([v * beta[..., None], kb * jnp.exp(G)[..., None]], axis=-1)
    sol = lax.linalg.triangular_solve(m + eye, rhs, left_side=True, lower=True, unit_diagonal=True)
    u, w = sol[..., :dv], sol[..., dv:]
    attn = jnp.einsum('bhnid,bhnjd->bhnij', q, k) * gamma
    q_dec = q * jnp.exp(G)[..., None]
    k_dec = k * jnp.exp(G[..., -1:] - G)[..., None]
    last = jnp.exp(G[..., -1])

    def step(s, inp):
        u_i, w_i, a_i, qd_i, kd_i, last_i = inp
        v_new = u_i - jnp.einsum('bhck,bhkv->bhcv', w_i, s)
        o_i = jnp.einsum('bhck,bhkv->bhcv', qd_i, s) + jnp.einsum('bhij,bhjv->bhiv', a_i, v_new)
        s = s * last_i[..., None, None] + jnp.einsum('bhck,bhcv->bhkv', kd_i, v_new)
        return s, o_i

    xs = tuple(jnp.moveaxis(t, 2, 0) for t in (u, w, attn, q_dec, k_dec, last))
    s_fin, o = lax.scan(step, s0, xs)
    return jnp.moveaxis(o, 0, 2).reshape(b, h, l, dv), s_fin


def gla_chunked(q, k, v, log_a, s0):
    b, h, l, dk = q.shape
    dv = v.shape[-1]
    c = CHUNK_G
    n = l // c
    q = q.reshape(b, h, n, c, dk)
    k = k.reshape(b, h, n, c, dk)
    v = v.reshape(b, h, n, c, dv)
    Bc = jnp.cumsum(log_a.reshape(b, h, n, c, dk), axis=3)
    idx = jnp.arange(c)
    causal = (idx[:, None] >= idx[None, :])[..., None]
    decay = jnp.exp(jnp.where(causal, Bc[..., :, None, :] - Bc[..., None, :, :], -jnp.inf))
    attn = jnp.einsum('bhnid,bhnjd,bhnijd->bhnij', q, k, decay)
    q_dec = q * jnp.exp(Bc)
    k_dec = k * jnp.exp(Bc[..., -1:, :] - Bc)
    last = jnp.exp(Bc[..., -1, :])

    def step(s, inp):
        qd_i, kd_i, a_i, v_i, last_i = inp
        o_i = jnp.einsum('bhck,bhkv->bhcv', qd_i, s) + jnp.einsum('bhij,bhjv->bhiv', a_i, v_i)
        s = s * last_i[..., :, None] + jnp.einsum('bhck,bhcv->bhkv', kd_i, v_i)
        return s, o_i

    xs = tuple(jnp.moveaxis(t, 2, 0) for t in (q_dec, k_dec, attn, v, last))
    s_fin, o = lax.scan(step, s0, xs)
    return jnp.moveaxis(o, 0, 2).reshape(b, h, l, dv), s_fin


def bidirectional(scan_fn, q, k, v, gates_f, gates_b, s0):
    o_f, s_f = scan_fn(q, k, v, *gates_f, s0[:, 0])
    o_b, s_b = scan_fn(flip_seq(q), flip_seq(k), flip_seq(v), *[flip_seq(t) for t in gates_b], s0[:, 1])
    return o_f + flip_seq(o_b), jnp.stack([s_f, s_b], axis=1)


def token_mixer(h, lp, rows, s_delta0, s_gla0):
    b, l, _ = h.shape
    (a_x, a_b, a_c, d_q, d_k, d_v, d_z, d_beta, d_a,
     g_q, g_k, g_v, g_r, g_lr, m_g) = jnp.split(h @ lp['w_in'], PROJ_SPLITS, axis=-1)

    y_a = a_b * centred_conv(a_c * a_x, lp['conv_a'], rows)
    br_a = y_a @ lp['w_br_a']

    qkv = jax.nn.silu(centred_conv(jnp.concatenate([d_q, d_k, d_v], axis=-1), lp['conv_qkv'], rows))
    q, k, v = jnp.split(qkv, (H_D * DK_D, 2 * H_D * DK_D), axis=-1)
    q = l2_normalize(to_heads(q, H_D)) * (DK_D ** -0.5)
    k = l2_normalize(to_heads(k, H_D))
    v = to_heads(v, H_D)
    beta = jax.nn.sigmoid(d_beta.astype(F32)).reshape(b, l, 2, H_D).transpose(2, 0, 3, 1)
    a_in = d_a.astype(F32).reshape(b, l, 2, H_D).transpose(2, 0, 3, 1)
    g_dec = -jnp.exp(lp['delta_a_log'].astype(F32))[:, None, :, None] * jax.nn.softplus(
        a_in + lp['delta_dt_bias'].astype(F32)[:, None, :, None])
    o_d, s_delta = bidirectional(gated_delta_chunked, q, k, v, (g_dec[0], beta[0]), (g_dec[1], beta[1]), s_delta0)
    o_d = rms_norm(o_d.transpose(0, 2, 1, 3), lp['delta_norm_g']) * jax.nn.silu(
        d_z.astype(F32).reshape(b, l, H_D, DV_D))
    br_d = o_d.reshape(b, l, H_D * DV_D).astype(h.dtype) @ lp['w_br_d']

    gq = to_heads(g_q, H_G) * (DK_G ** -0.5)
    gk = to_heads(g_k, H_G)
    gv = to_heads(g_v, H_G)
    lr = g_lr.astype(F32).reshape(b, l, 2, GLA_RANK)
    logits = jnp.einsum('blsr,srk->sblk', lr, lp['gla_w2'].astype(F32)) + lp['gla_b'].astype(F32)[:, None, None, :]
    log_a = (jax.nn.log_sigmoid(logits) / GLA_TAU).reshape(2, b, l, H_G, DK_G).transpose(0, 1, 3, 2, 4)
    o_g, s_gla = bidirectional(gla_chunked, gq, gk, gv, (log_a[0],), (log_a[1],), s_gla0)
    o_g = rms_norm(o_g.transpose(0, 2, 1, 3), lp['gla_norm_g']) * jax.nn.silu(
        g_r.astype(F32).reshape(b, l, H_G, DV_G))
    br_g = o_g.reshape(b, l, H_G * DV_G).astype(h.dtype) @ lp['w_br_g']

    gates = jax.nn.sigmoid(m_g).reshape(b, l, N_BRANCH, D_MODEL)
    merged = gates[:, :, 0] * br_a + gates[:, :, 1] * br_d + gates[:, :, 2] * br_g
    return merged @ lp['w_o'], s_delta, s_gla


def trunk_layer(x, cond, lp, rows, s_delta0, s_gla0):
    ada = (jax.nn.silu(cond) @ lp['w_ada'] + lp['b_ada']).reshape(cond.shape[0], 1, N_ADA, D_MODEL)

    def mod(t, j):
        return t * (1.0 + ada[:, :, 3 * j + 1]) + ada[:, :, 3 * j]

    x = layer_norm(ALPHA * x + 0.5 * ada[:, :, 2] * swiglu(mod(x, 0), lp['ffn_w1'][0], lp['ffn_w2'][0]),
                   lp['ln_g'][0], lp['ln_b'][0])
    y, s_delta, s_gla = token_mixer(mod(x, 1), lp, rows, s_delta0, s_gla0)
    x = layer_norm(ALPHA * x + ada[:, :, 5] * y, lp['ln_g'][1], lp['ln_b'][1])
    x = layer_norm(ALPHA * x + 0.5 * ada[:, :, 8] * swiglu(mod(x, 2), lp['ffn_w1'][1], lp['ffn_w2'][1]),
                   lp['ln_g'][2], lp['ln_b'][2])
    return x, s_delta, s_gla


def setup_inputs(seed: int = 0) -> dict:
    key = jax.random.key(seed)
    ks = jax.random.split(key, 32)

    def nrm(k, shape, s):
        return jax.random.normal(k, shape, F32) * s

    L = DEPTH
    dt = jnp.exp(jax.random.uniform(ks[13], (L, 2, H_D), F32, math.log(1e-3), math.log(1e-1)))
    return {
        'x_prompt': nrm(ks[0], (BATCH, SEQ, D_MODEL), 1.0),
        'x_sample': nrm(ks[1], (DEC_BATCH, DEC_SEQ, D_MODEL), 1.0),
        'state_delta': nrm(ks[2], (DEC_BATCH, DEPTH, 2, H_D, DK_D, DV_D), 0.1),
        'state_gla': nrm(ks[3], (DEC_BATCH, DEPTH, 2, H_G, DK_G, DV_G), 0.1),
        'c': nrm(ks[4], (DEC_BATCH, D_MODEL), 1.0),
        'c_ctx': nrm(ks[5], (D_MODEL,), 1.0),
        'w_ada': nrm(ks[6], (L, D_MODEL, N_ADA * D_MODEL), D_MODEL ** -0.5),
        'b_ada': nrm(ks[7], (L, N_ADA * D_MODEL), 0.02),
        'ln_g': 1.0 + nrm(ks[8], (L, 3, D_MODEL), 0.02),
        'ln_b': nrm(ks[9], (L, 3, D_MODEL), 0.02),
        'ffn_w1': nrm(ks[10], (L, 2, D_MODEL, 2 * D_FF), D_MODEL ** -0.5),
        'ffn_w2': nrm(ks[11], (L, 2, D_FF, D_MODEL), BETA_INIT * D_FF ** -0.5),
        'w_in': nrm(ks[12], (L, D_MODEL, D_PROJ), D_MODEL ** -0.5),
        'conv_a': nrm(ks[14], (L, CONV_W, W_A), CONV_W ** -0.5),
        'conv_qkv': nrm(ks[15], (L, CONV_W, 2 * H_D * DK_D + H_D * DV_D), CONV_W ** -0.5),
        'delta_a_log': jnp.log(jax.random.uniform(ks[16], (L, 2, H_D), F32, 1.0, 16.0)),
        'delta_dt_bias': dt + jnp.log(-jnp.expm1(-dt)),
        'delta_norm_g': 1.0 + nrm(ks[17], (L, DV_D), 0.02),
        'gla_w2': nrm(ks[18], (L, 2, GLA_RANK, H_G * DK_G), GLA_RANK ** -0.5),
        'gla_b': nrm(ks[19], (L, 2, H_G * DK_G), 0.02),
        'gla_norm_g': 1.0 + nrm(ks[20], (L, DV_G), 0.02),
        'w_br_a': nrm(ks[21], (L, W_A, D_MODEL), W_A ** -0.5),
        'w_br_d': nrm(ks[22], (L, H_D * DV_D, D_MODEL), (H_D * DV_D) ** -0.5),
        'w_br_g': nrm(ks[23], (L, H_G * DV_G, D_MODEL), (H_G * DV_G) ** -0.5),
        'w_o': nrm(ks[24], (L, D_MODEL, D_MODEL), BETA_INIT * D_MODEL ** -0.5),
    }


def reference(x_prompt, x_sample, state_delta, state_gla, c, c_ctx, w_ada, b_ada, ln_g, ln_b,
              ffn_w1, ffn_w2, w_in, conv_a, conv_qkv, delta_a_log, delta_dt_bias, delta_norm_g,
              gla_w2, gla_b, gla_norm_g, w_br_a, w_br_d, w_br_g, w_o):
    layers = [dict(w_ada=w_ada[i], b_ada=b_ada[i], ln_g=ln_g[i], ln_b=ln_b[i], ffn_w1=ffn_w1[i],
                   ffn_w2=ffn_w2[i], w_in=w_in[i], conv_a=conv_a[i], conv_qkv=conv_qkv[i],
                   delta_a_log=delta_a_log[i], delta_dt_bias=delta_dt_bias[i], delta_norm_g=delta_norm_g[i],
                   gla_w2=gla_w2[i], gla_b=gla_b[i], gla_norm_g=gla_norm_g[i], w_br_a=w_br_a[i],
                   w_br_d=w_br_d[i], w_br_g=w_br_g[i], w_o=w_o[i]) for i in range(DEPTH)]

    n_ctx = x_prompt.shape[0]
    zero_d = jnp.zeros((n_ctx, 2, H_D, DK_D, DV_D), F32)
    zero_g = jnp.zeros((n_ctx, 2, H_G, DK_G, DV_G), F32)
    h = x_prompt
    sds, sgs = [], []
    for i in range(DEPTH):
        h, sd, sg = trunk_layer(h, c_ctx[None, :], layers[i], None, zero_d, zero_g)
        sds.append(sd)
        sgs.append(sg)
    y_prompt = h
    new_state_delta = jnp.stack(sds, axis=1).astype(x_prompt.dtype)
    new_state_gla = jnp.stack(sgs, axis=1).astype(x_prompt.dtype)

    rows = x_sample.shape[1] // GRID_W
    h = x_sample
    for i in range(DEPTH):
        h, _, _ = trunk_layer(h, c, layers[i], rows, state_delta[:, i].astype(F32), state_gla[:, i].astype(F32))
    y_sample = h
    return (y_prompt, y_sample, new_state_delta, new_state_gla)
```

```python
import functools

import jax
import jax.numpy as jnp
import numpy as np
from jax import lax
from jax.experimental import pallas as pl
from jax.experimental.pallas import tpu as pltpu

F32 = jnp.float32
BF16 = jnp.bfloat16

D_MODEL = 1024
DEPTH = 2
GRID_W = 64
D_FF = 2816
W_A = 512
H_D, DK_D, DV_D, CHUNK_D = 4, 128, 128, 64
H_G, DK_G, DV_G, CHUNK_G = 4, 64, 128, 16
GLA_RANK = 16
GLA_TAU = 16.0
N_ADA = 9
ALPHA = float((2 * DEPTH) ** 0.25)
LN_EPS = 1e-5
RMS_EPS = 1e-6

COL_A = 0
COL_DQ = 1536
COL_GQ = 3584
COL_MG = 5120
COL_SM = 8192
D_PROJ_PAD = 8320
SM_BETA, SM_A, SM_LR = 0, 8, 16

VMEM_LIMIT = 56 * 1024 * 1024


def _cparams(sem):
    return pltpu.CompilerParams(dimension_semantics=sem, vmem_limit_bytes=VMEM_LIMIT)


def _dot(a, b):
    return jnp.dot(a, b, preferred_element_type=F32)


def _dot_nt(a, b):
    return lax.dot_general(a, b, (((1,), (1,)), ((), ())), preferred_element_type=F32)


def _dot_tn(a, b):
    return lax.dot_general(a, b, (((0,), (0,)), ((), ())), preferred_element_type=F32)


def _sigmoid(x):
    return 1.0 / (1.0 + jnp.exp(-x))


def _silu(x):
    return x * _sigmoid(x)


def _softplus(x):
    return jnp.maximum(x, 0.0) + jnp.log1p(jnp.exp(-jnp.abs(x)))


def _split2(x):
    hi = x.astype(BF16)
    lo = (x - hi.astype(F32)).astype(BF16)
    return hi, lo


def _split3_rows(x):
    hi = x.astype(BF16)
    r = x - hi.astype(F32)
    mid = r.astype(BF16)
    lo = (r - mid.astype(F32)).astype(BF16)
    return jnp.concatenate([hi, mid, lo], axis=0)


def _mm3(a, b):
    ah, al = _split2(a)
    bh, bl = _split2(b)
    n = a.shape[0]
    p = _dot(jnp.concatenate([ah, al], axis=0), bh)
    return p[:n] + p[n:] + _dot(ah, bl)


def _mm1(a, b):
    return _dot(a.astype(BF16), b.astype(BF16))


def _layer_norm(y, g, b):
    mu = jnp.mean(y, axis=-1, keepdims=True)
    yc = y - mu
    var = jnp.mean(yc * yc, axis=-1, keepdims=True)
    return yc * lax.rsqrt(var + LN_EPS) * g + b


def _ada_kernel(cond_ref, w_ref, b_ref, o_ref):
    s = _silu(cond_ref[...]).astype(BF16)
    o_ref[0] = _dot(s, w_ref[0].astype(BF16)) + b_ref[0]


def _ada_table(cond16, w_ada, b_ada):
    n_l = w_ada.shape[0]
    tn = 1024
    out = pl.pallas_call(
        _ada_kernel,
        grid=(n_l, N_ADA * D_MODEL // tn),
        in_specs=[
            pl.BlockSpec((16, D_MODEL), lambda l, j: (0, 0)),
            pl.BlockSpec((1, D_MODEL, tn), lambda l, j: (l, 0, j)),
            pl.BlockSpec((1, 1, tn), lambda l, j: (l, 0, j)),
        ],
        out_specs=pl.BlockSpec((1, 16, tn), lambda l, j: (l, 0, j)),
        out_shape=jax.ShapeDtypeStruct((n_l, 16, N_ADA * D_MODEL), F32),
        compiler_params=_cparams(("arbitrary", "arbitrary")),
        name="ada",
    )(cond16, w_ada, b_ada.reshape(n_l, 1, N_ADA * D_MODEL))
    return out.reshape(n_l, 16, N_ADA, D_MODEL)


def _cond_row(tok0, n_ctx_tok, lat_len):
    return jnp.where(tok0 < n_ctx_tok, 8, (tok0 - n_ctx_tok) // lat_len)


def _ffn_kernel(x_ref, ada_ref, w1g_ref, w1u_ref, w2_ref, lng_ref, lnb_ref, o_ref, h_scr, acc_scr, *, j):
    f = pl.program_id(1)

    @pl.when(f == 0)
    def _():
        x = x_ref[...]
        h_scr[...] = (x * (1.0 + ada_ref[3 * j + 1:3 * j + 2, :]) + ada_ref[3 * j:3 * j + 1, :]).astype(BF16)
        acc_scr[...] = jnp.zeros_like(acc_scr)

    h = h_scr[...]
    g = _dot(h, w1g_ref[...])
    u = _dot(h, w1u_ref[...])
    a = (_silu(g) * u).astype(BF16)
    acc_scr[...] += _dot(a, w2_ref[...])

    @pl.when(f == pl.num_programs(1) - 1)
    def _():
        y = ALPHA * x_ref[...] + 0.5 * ada_ref[3 * j + 2:3 * j + 3, :] * acc_scr[...]
        o_ref[...] = _layer_norm(y, lng_ref[...], lnb_ref[...])


def _ffn(x, ada_l, w1, w2, lng, lnb, j, n_ctx_tok, lat_len, tm=512, tf=1408):
    t = x.shape[0]
    nf = D_FF // tf
    cond = lambda i: _cond_row(i * tm, n_ctx_tok, lat_len)
    return pl.pallas_call(
        functools.partial(_ffn_kernel, j=j),
        grid=(t // tm, nf),
        in_specs=[
            pl.BlockSpec((tm, D_MODEL), lambda i, f: (i, 0)),
            pl.BlockSpec((None, N_ADA, D_MODEL), lambda i, f: (cond(i), 0, 0)),
            pl.BlockSpec((D_MODEL, tf), lambda i, f: (0, f)),
            pl.BlockSpec((D_MODEL, tf), lambda i, f: (0, f + nf)),
            pl.BlockSpec((tf, D_MODEL), lambda i, f: (f, 0)),
            pl.BlockSpec((1, D_MODEL), lambda i, f: (0, 0)),
            pl.BlockSpec((1, D_MODEL), lambda i, f: (0, 0)),
        ],
        out_specs=pl.BlockSpec((tm, D_MODEL), lambda i, f: (i, 0)),
        out_shape=jax.ShapeDtypeStruct((t, D_MODEL), F32),
        scratch_shapes=[pltpu.VMEM((tm, D_MODEL), BF16), pltpu.VMEM((tm, D_MODEL), F32)],
        compiler_params=_cparams(("arbitrary", "arbitrary")),
        name="ffn",
    )(x, ada_l, w1, w1, w2, lng.reshape(1, D_MODEL), lnb.reshape(1, D_MODEL))


def _inproj_kernel(x_ref, ada_ref, w_ref, o_ref, h_scr):
    @pl.when(pl.program_id(1) == 0)
    def _():
        h_scr[...] = (x_ref[...] * (1.0 + ada_ref[4:5, :]) + ada_ref[3:4, :]).astype(BF16)

    o_ref[...] = _dot(h_scr[...], w_ref[...])


def _inproj(x, ada_l, w_in_p, n_ctx_tok, lat_len, tm=1024, tn=1664):
    t = x.shape[0]
    cond = lambda i: _cond_row(i * tm, n_ctx_tok, lat_len)
    return pl.pallas_call(
        _inproj_kernel,
        grid=(t // tm, D_PROJ_PAD // tn),
        in_specs=[
            pl.BlockSpec((tm, D_MODEL), lambda i, n: (i, 0)),
            pl.BlockSpec((None, N_ADA, D_MODEL), lambda i, n: (cond(i), 0, 0)),
            pl.BlockSpec((D_MODEL, tn), lambda i, n: (0, n)),
        ],
        out_specs=pl.BlockSpec((tm, tn), lambda i, n: (i, n)),
        out_shape=jax.ShapeDtypeStruct((t, D_PROJ_PAD), F32),
        scratch_shapes=[pltpu.VMEM((tm, D_MODEL), BF16)],
        compiler_params=_cparams(("arbitrary", "arbitrary")),
        name="inproj",
    )(x, ada_l, w_in_p)


MERGE_TM = 256


def _merge_kernel(a_ref, m0_ref, m1_ref, m2_ref, od_ref, og_ref, x_ref, ada_ref, cw_ref,
                  wa_ref, wd_ref, wg_ref, wo_ref, lng_ref, lnb_ref, o_ref, *, n_ctx_tiles):
    i = pl.program_id(0)
    seg = jnp.where(i < n_ctx_tiles, MERGE_TM, GRID_W)
    row = lax.broadcasted_iota(jnp.int32, (MERGE_TM, W_A), 0)
    pos = jnp.bitwise_and(row, seg - 1)
    a_x = a_ref[:, 0:W_A]
    a_b = a_ref[:, W_A:2 * W_A]
    a_c = a_ref[:, 2 * W_A:3 * W_A]
    z = a_c * a_x
    z_prev = jnp.where(pos == 0, 0.0, pltpu.roll(z, 1, 0))
    z_next = jnp.where(pos == seg - 1, 0.0, pltpu.roll(z, MERGE_TM - 1, 0))
    y_a = a_b * (cw_ref[0:1, :] * z_prev + cw_ref[1:2, :] * z + cw_ref[2:3, :] * z_next)
    br_a = _dot(y_a.astype(BF16), wa_ref[...])
    br_d = _dot(od_ref[...], wd_ref[...])
    br_g = _dot(og_ref[...], wg_ref[...])
    merged = _sigmoid(m0_ref[...]) * br_a + _sigmoid(m1_ref[...]) * br_d + _sigmoid(m2_ref[...]) * br_g
    y = _dot(merged.astype(BF16), wo_ref[...])
    y = ALPHA * x_ref[...] + ada_ref[5:6, :] * y
    o_ref[...] = _layer_norm(y, lng_ref[...], lnb_ref[...])


def _merge(p, od, og, x, ada_l, conv_a, wa, wd, wg, wo, lng, lnb, n_ctx_tok, lat_len):
    t = x.shape[0]
    tm = MERGE_TM
    cond = lambda i: _cond_row(i * tm, n_ctx_tok, lat_len)
    full = lambda shape: pl.BlockSpec(shape, lambda i: (0,) * len(shape))
    mg0 = COL_MG // D_MODEL
    return pl.pallas_call(
        functools.partial(_merge_kernel, n_ctx_tiles=n_ctx_tok // tm),
        grid=(t // tm,),
        in_specs=[
            pl.BlockSpec((tm, 3 * W_A), lambda i: (i, 0)),
            pl.BlockSpec((tm, D_MODEL), lambda i: (i, mg0)),
            pl.BlockSpec((tm, D_MODEL), lambda i: (i, mg0 + 1)),
            pl.BlockSpec((tm, D_MODEL), lambda i: (i, mg0 + 2)),
            pl.BlockSpec((tm, 512), lambda i: (i, 0)),
            pl.BlockSpec((tm, 512), lambda i: (i, 0)),
            pl.BlockSpec((tm, D_MODEL), lambda i: (i, 0)),
            pl.BlockSpec((None, N_ADA, D_MODEL), lambda i: (cond(i), 0, 0)),
            full((3, W_A)),
            full((W_A, D_MODEL)), full((512, D_MODEL)), full((512, D_MODEL)), full((D_MODEL, D_MODEL)),
            full((1, D_MODEL)), full((1, D_MODEL)),
        ],
        out_specs=pl.BlockSpec((tm, D_MODEL), lambda i: (i, 0)),
        out_shape=jax.ShapeDtypeStruct((t, D_MODEL), F32),
        compiler_params=_cparams(("arbitrary",)),
        name="merge",
    )(p, p, p, p, od, og, x, ada_l, conv_a, wa, wd, wg, wo, lng.reshape(1, D_MODEL), lnb.reshape(1, D_MODEL))


C = CHUNK_D


def _tri_inverse(m, eye, row, col):
    def same_block(shift):
        return jnp.right_shift(row, shift) == jnp.right_shift(col, shift)

    m8 = jnp.where(same_block(3), m, 0.0)
    x = eye - m8
    sq = _mm3(m8, m8)
    x = x + _mm3(x, sq)
    sq = _mm3(sq, sq)
    x = x + _mm3(x, sq)
    for shift in (4, 5, 6):
        e = jnp.where(same_block(shift) & jnp.logical_not(same_block(shift - 1)), m, 0.0)
        x = x - _mm3(_mm3(x, e), x)
    return x


def _delta_kernel(*refs, seq_len, seg, has_init, want_state):
    (q_ref, k_ref, v_ref, z_ref, sm_ref, cw_ref, arow_ref, dtb_ref, ng_ref, cum3_ref, tri3_ref) = refs[:11]
    pos = 11
    s0_ref = None
    if has_init:
        s0_ref = refs[pos]
        pos += 1
    o_ref = refs[pos]
    pos += 1
    sfin_ref = None
    if want_state:
        sfin_ref = refs[pos]
        pos += 1
    u_scr, wq_scr, at_scr, kd_scr, ls_scr, s_scr, of_scr, ob_scr = refs[pos:]

    n_chunks = seq_len // C
    row = lax.broadcasted_iota(jnp.int32, (C, C), 0)
    col = lax.broadcasted_iota(jnp.int32, (C, C), 1)
    eye = jnp.where(row == col, 1.0, 0.0).astype(F32)
    row128 = lax.broadcasted_iota(jnp.int32, (C, 128), 0)

    def conv_block(ref, c0, wc0, r0, n):
        x = ref[pl.ds(r0, C), c0:c0 + 128]
        xp = pltpu.roll(x, 1, 0)
        xn = pltpu.roll(x, C - 1, 0)
        if seg == C:
            xp = jnp.where(row128 == 0, 0.0, xp)
            xn = jnp.where(row128 == C - 1, 0.0, xn)
        else:
            prev8 = ref[pl.ds(pl.multiple_of(jnp.maximum(r0 - 8, 0), 8), 8), c0:c0 + 128]
            next8 = ref[pl.ds(pl.multiple_of(jnp.minimum(r0 + C, seq_len - 8), 8), 8), c0:c0 + 128]
            pm = jnp.where(n > 0, 1.0, 0.0)
            nm = jnp.where(n < n_chunks - 1, 1.0, 0.0)
            xp = jnp.where(row128 == 0, prev8[7:8, :] * pm, xp)
            xn = jnp.where(row128 == C - 1, next8[0:1, :] * nm, xn)
        y = cw_ref[0:1, wc0:wc0 + 128] * xp + cw_ref[1:2, wc0:wc0 + 128] * x + cw_ref[2:3, wc0:wc0 + 128] * xn
        return _silu(y)

    def l2n(x):
        return x * lax.rsqrt(jnp.sum(x * x, axis=-1, keepdims=True) + RMS_EPS)

    def chunk_body(n, carry):
        r0 = pl.multiple_of(n * C, C)
        sm = sm_ref[pl.ds(r0, C), :]
        beta_full = _sigmoid(sm)
        g_full = -jnp.exp(arow_ref[...]) * _softplus(sm + dtb_ref[...])
        cs = _dot(cum3_ref[...], _split3_rows(g_full))
        for h in range(H_D):
            q = l2n(conv_block(q_ref, h * 128, h * 128, r0, n)) * (DK_D ** -0.5)
            k = l2n(conv_block(k_ref, h * 128, 512 + h * 128, r0, n))
            v = conv_block(v_ref, h * 128, 1024 + h * 128, r0, n)
            k16 = k.astype(BF16)
            kk = _dot_nt(k16, k16)
            qk = _dot_nt(q.astype(BF16), k16)
            for d in range(2):
                cb = SM_BETA + d * H_D + h
                cg = SM_A + d * H_D + h
                beta = beta_full[:, cb:cb + 1]
                gcol = g_full[:, cg:cg + 1]
                gsum = cs[d * C:(d + 1) * C, cg:cg + 1]
                gtot = cs[2 * C:3 * C, cg:cg + 1]
                if d == 0:
                    strict, incl = row > col, row >= col
                else:
                    strict, incl = row < col, row <= col
                xg = jnp.where(strict, jnp.broadcast_to(gcol, (C, C)), 0.0)
                diff = _dot(tri3_ref[d], _split3_rows(xg))
                gamma = jnp.where(incl, jnp.exp(jnp.minimum(diff, 0.0)), 0.0)
                m = jnp.where(strict, beta * kk * gamma, 0.0)
                attn = qk * gamma
                eg = jnp.exp(gsum)
                rhs = jnp.concatenate([v * beta, k * (beta * eg)], axis=1)
                tinv = _tri_inverse(m, eye, row, col)
                sol = _mm3(tinv, rhs)
                idx = (d * H_D + h) * n_chunks + n
                u_scr[idx] = sol[:, :128]
                wq_scr[idx, 0:C, :] = sol[:, 128:].astype(BF16)
                wq_scr[idx, C:2 * C, :] = (q * eg).astype(BF16)
                at_scr[idx] = attn.astype(BF16)
                kd_scr[idx] = (k * jnp.exp(gtot - gsum)).astype(BF16)
                ls_scr[idx] = jnp.broadcast_to(jnp.exp(gtot[0:8, :]), (8, 128))
        return carry

    lax.fori_loop(0, n_chunks, chunk_body, 0)

    if has_init:
        s_scr[...] = s0_ref[0]
    else:
        s_scr[...] = jnp.zeros_like(s_scr)

    def scan_body(i, carry):
        for d in range(2):
            n = i if d == 0 else n_chunks - 1 - i
            r0 = pl.multiple_of(n * C, C)
            o_dst = of_scr if d == 0 else ob_scr
            for h in range(H_D):
                idx = (d * H_D + h) * n_chunks + n
                s = s_scr[d * H_D + h]
                r = _dot(wq_scr[idx], s.astype(BF16))
                v_new = u_scr[idx] - r[0:C]
                v16 = v_new.astype(BF16)
                o_dst[pl.ds(r0, C), h * 128:(h + 1) * 128] = r[C:2 * C] + _dot(at_scr[idx], v16)
                s_scr[d * H_D + h] = s * ls_scr[idx][0:1, :] + _dot_tn(kd_scr[idx], v16)
        return carry

    lax.fori_loop(0, n_chunks, scan_body, 0)

    if want_state:
        sfin_ref[0] = s_scr[...]

    def out_body(n, carry):
        r0 = pl.multiple_of(n * C, C)
        for h in range(H_D):
            o = of_scr[pl.ds(r0, C), h * 128:(h + 1) * 128] + ob_scr[pl.ds(r0, C), h * 128:(h + 1) * 128]
            o = o * lax.rsqrt(jnp.mean(o * o, axis=-1, keepdims=True) + RMS_EPS) * ng_ref[...]
            zg = z_ref[pl.ds(r0, C), h * 128:(h + 1) * 128]
            o_ref[pl.ds(r0, C), h * 128:(h + 1) * 128] = (o * _silu(zg)).astype(BF16)
        return carry

    lax.fori_loop(0, n_chunks, out_body, 0)


def _delta(p, consts, conv_qkv, arow, dtb, ng, s0, *, seq_len, n_seq, row_blk0, seg, want_state):
    has_init = s0 is not None
    n_chunks = seq_len // C
    nhd = 2 * H_D
    cq = COL_DQ // 512
    sm_blk = COL_SM // 128
    full = lambda shape: pl.BlockSpec(shape, lambda s: (0,) * len(shape))
    in_specs = [
        pl.BlockSpec((seq_len, 512), lambda s: (s + row_blk0, cq)),
        pl.BlockSpec((seq_len, 512), lambda s: (s + row_blk0, cq + 1)),
        pl.BlockSpec((seq_len, 512), lambda s: (s + row_blk0, cq + 2)),
        pl.BlockSpec((seq_len, 512), lambda s: (s + row_blk0, cq + 3)),
        pl.BlockSpec((seq_len, 128), lambda s: (s + row_blk0, sm_blk)),
        full((3, 1536)), full((1, 128)), full((1, 128)), full((1, 128)),
        full((3 * C, 3 * C)), full((2, C, 3 * C)),
    ]
    args = [p, p, p, p, p, conv_qkv, arow, dtb, ng, consts["cum3_d"], consts["tri3_d"]]
    if has_init:
        in_specs.append(pl.BlockSpec((1, nhd, DK_D, DV_D), lambda s: (s, 0, 0, 0)))
        args.append(s0)
    out_specs = [pl.BlockSpec((seq_len, 512), lambda s: (s, 0))]
    out_shape = [jax.ShapeDtypeStruct((n_seq * seq_len, 512), BF16)]
    if want_state:
        out_specs.append(pl.BlockSpec((1, nhd, DK_D, DV_D), lambda s: (s, 0, 0, 0)))
        out_shape.append(jax.ShapeDtypeStruct((n_seq, nhd, DK_D, DV_D), F32))
    res = pl.pallas_call(
        functools.partial(_delta_kernel, seq_len=seq_len, seg=seg, has_init=has_init, want_state=want_state),
        grid=(n_seq,),
        in_specs=in_specs,
        out_specs=out_specs,
        out_shape=out_shape,
        scratch_shapes=[
            pltpu.VMEM((nhd * n_chunks, C, 128), F32),
            pltpu.VMEM((nhd * n_chunks, 2 * C, 128), BF16),
            pltpu.VMEM((nhd * n_chunks, C, C), BF16),
            pltpu.VMEM((nhd * n_chunks, C, 128), BF16),
            pltpu.VMEM((nhd * n_chunks, 8, 128), F32),
            pltpu.VMEM((nhd, DK_D, DV_D), F32),
            pltpu.VMEM((seq_len, 512), F32),
            pltpu.VMEM((seq_len, 512), F32),
        ],
        compiler_params=_cparams(("arbitrary",)),
        name="delta_lat" if has_init else "delta_ctx",
    )(*args)
    return res


CG = CHUNK_G
GB = 64


def _gla_kernel(*refs, seq_len, has_init, want_state):
    (q_ref, k_ref, v_ref, r_ref, sm_ref, w2_ref, b_ref, ng_ref, cum3_ref, sele_ref, bmask_ref) = refs[:11]
    pos = 11
    s0_ref = None
    if has_init:
        s0_ref = refs[pos]
        pos += 1
    o_ref = refs[pos]
    pos += 1
    sfin_ref = None
    if want_state:
        sfin_ref = refs[pos]
        pos += 1
    bc_scr, tot_scr, oacc_scr, a_scr, st_scr = refs[pos:]

    n_blocks = seq_len // GB
    n_chunks = seq_len // CG
    rowi = lax.broadcasted_iota(jnp.int32, (CG, H_G * DK_G), 0)
    qscale = DK_G ** -0.5

    def block_body(b, carry):
        r0 = pl.multiple_of(b * GB, GB)
        sm16 = sm_ref[pl.ds(r0, GB), :].astype(BF16)
        q = q_ref[pl.ds(r0, GB), :] * qscale
        k = k_ref[pl.ds(r0, GB), :]
        v = v_ref[pl.ds(r0, GB), :]
        o_blk = [None] * (GB // CG)
        for d in range(2):
            logits = _dot(sm16, w2_ref[d]) + b_ref[d]
            la = -_softplus(-logits) * (1.0 / GLA_TAU)
            cs = _dot(cum3_ref[...], _split3_rows(la))
            bc = cs[d * GB:(d + 1) * GB]
            bc_scr[d, pl.ds(r0, GB), :] = bc
            tot_scr[d, pl.ds(r0, GB), :] = cs[2 * GB:3 * GB]
            for c in range(GB // CG):
                sl = slice(c * CG, (c + 1) * CG)
                qc, kc, bcc, vc = q[sl], k[sl], bc[sl], v[sl]
                for j in range(CG):
                    keep = (rowi >= j) if d == 0 else (rowi <= j)
                    e = jnp.exp(jnp.minimum(bcc - bcc[j:j + 1, :], 0.0))
                    a_scr[j * CG:(j + 1) * CG, :] = jnp.where(keep, qc * e * kc[j:j + 1, :], 0.0).astype(BF16)
                rr = _dot(a_scr[...], sele_ref[...])
                acc = rr[0:CG] * vc[0:1, :]
                for j in range(1, CG):
                    acc = acc + rr[j * CG:(j + 1) * CG] * vc[j:j + 1, :]
                o_blk[c] = acc if d == 0 else o_blk[c] + acc
        for c in range(GB // CG):
            oacc_scr[pl.ds(r0 + c * CG, CG), :] = o_blk[c]
        return carry

    lax.fori_loop(0, n_blocks, block_body, 0)

    if has_init:
        st_scr[...] = s0_ref[0]
    else:
        st_scr[...] = jnp.zeros_like(st_scr)

    def scan_body(i, carry):
        for d in range(2):
            n = i if d == 0 else n_chunks - 1 - i
            r0 = pl.multiple_of(n * CG, CG)
            bc = bc_scr[d, pl.ds(r0, CG), :]
            tot = tot_scr[d, pl.ds(r0, CG), :]
            q = q_ref[pl.ds(r0, CG), :] * qscale
            k = k_ref[pl.ds(r0, CG), :]
            v = v_ref[pl.ds(r0, CG), :]
            st = st_scr[d]
            oacc_scr[pl.ds(r0, CG), :] += _dot_nt((q * jnp.exp(bc)).astype(BF16), st.astype(BF16))
            upd = _dot_tn(v.astype(BF16), (k * jnp.exp(tot - bc)).astype(BF16))
            st_scr[d] = st * jnp.exp(tot[0:1, :]) + upd * bmask_ref[...]
        return carry

    lax.fori_loop(0, n_chunks, scan_body, 0)

    if want_state:
        for d in range(2):
            for h in range(H_G):
                sfin_ref[0, d * H_G + h] = st_scr[d, h * DV_G:(h + 1) * DV_G, h * DK_G:(h + 1) * DK_G]

    def out_body(b, carry):
        r0 = pl.multiple_of(b * GB, GB)
        for h in range(H_G):
            o = oacc_scr[pl.ds(r0, GB), h * DV_G:(h + 1) * DV_G]
            o = o * lax.rsqrt(jnp.mean(o * o, axis=-1, keepdims=True) + RMS_EPS) * ng_ref[...]
            rg = r_ref[pl.ds(r0, GB), h * DV_G:(h + 1) * DV_G]
            o_ref[pl.ds(r0, GB), h * DV_G:(h + 1) * DV_G] = (o * _silu(rg)).astype(BF16)
        return carry

    lax.fori_loop(0, n_blocks, out_body, 0)


def _gla(p, consts, w2p, gb, ng, s0, *, seq_len, n_seq, row_blk0, want_state):
    has_init = s0 is not None
    full = lambda shape: pl.BlockSpec(shape, lambda s: (0,) * len(shape))
    hk, hv = H_G * DK_G, H_G * DV_G
    in_specs = [
        pl.BlockSpec((seq_len, hk), lambda s: (s + row_blk0, COL_GQ // hk)),
        pl.BlockSpec((seq_len, hk), lambda s: (s + row_blk0, COL_GQ // hk + 1)),
        pl.BlockSpec((seq_len, hv), lambda s: (s + row_blk0, (COL_GQ + 2 * hk) // hv)),
        pl.BlockSpec((seq_len, hv), lambda s: (s + row_blk0, (COL_GQ + 2 * hk) // hv + 1)),
        pl.BlockSpec((seq_len, 128), lambda s: (s + row_blk0, COL_SM // 128)),
        full((2, 128, hk)), full((2, 1, hk)), full((1, DV_G)),
        full((3 * GB, 3 * GB)), full((CG * CG, hv)), full((hv, hk)),
    ]
    args = [p, p, p, p, p, w2p, gb, ng, consts["cum3_g"], consts["sele"], consts["bmask"]]
    if has_init:
        in_specs.append(pl.BlockSpec((1, 2, hv, hk), lambda s: (s, 0, 0, 0)))
        args.append(s0)
    out_specs = [pl.BlockSpec((seq_len, hv), lambda s: (s, 0))]
    out_shape = [jax.ShapeDtypeStruct((n_seq * seq_len, hv), BF16)]
    if want_state:
        out_specs.append(pl.BlockSpec((1, 2 * H_G, DV_G, DK_G), lambda s: (s, 0, 0, 0)))
        out_shape.append(jax.ShapeDtypeStruct((n_seq, 2 * H_G, DV_G, DK_G), F32))
    return pl.pallas_call(
        functools.partial(_gla_kernel, seq_len=seq_len, has_init=has_init, want_state=want_state),
        grid=(n_seq,),
        in_specs=in_specs,
        out_specs=out_specs,
        out_shape=out_shape,
        scratch_shapes=[
            pltpu.VMEM((2, seq_len, hk), F32),
            pltpu.VMEM((2, seq_len, hk), F32),
            pltpu.VMEM((seq_len, hv), F32),
            pltpu.VMEM((CG * CG, hk), BF16),
            pltpu.VMEM((2, hv, hk), F32),
        ],
        compiler_params=_cparams(("arbitrary",)),
        name="gla_lat" if has_init else "gla_ctx",
    )(*args)


def _constants():
    i = np.arange(C)
    lo = (i[:, None] >= i[None, :]).astype(np.float32)
    up = (i[:, None] <= i[None, :]).astype(np.float32)
    ones = np.ones((C, C), np.float32)
    cum_d = np.concatenate([lo, up, ones], axis=0)
    blk = (i[:, None] // CG == i[None, :] // CG).astype(np.float32)
    cum_g = np.concatenate([lo * blk, up * blk, blk], axis=0)
    hk, hv = H_G * DK_G, H_G * DV_G
    sele = (np.arange(hk)[:, None] // DK_G == np.arange(hv)[None, :] // DV_G).astype(np.float32)
    bmask = (np.arange(hv)[:, None] // DV_G == np.arange(hk)[None, :] // DK_G).astype(np.float32)
    return {
        "cum3_d": jnp.asarray(np.tile(cum_d, (1, 3)), BF16),
        "cum3_g": jnp.asarray(np.tile(cum_g, (1, 3)), BF16),
        "tri3_d": jnp.asarray(np.stack([np.tile(lo, (1, 3)), np.tile(up, (1, 3))]), BF16),
        "sele": jnp.asarray(sele, BF16),
        "bmask": jnp.asarray(bmask, F32),
    }


def _permute_w_in(w_in_l):
    widths = (W_A, W_A, W_A, 512, 512, 512, 512, 8, 8, 256, 256, 512, 512, 32, 3 * D_MODEL)
    offs = np.concatenate([[0], np.cumsum(widths)])
    seg = lambda a, b: w_in_l[:, offs[a]:offs[b]]
    pad = jnp.zeros((D_MODEL, 128 - 48), w_in_l.dtype)
    return jnp.concatenate(
        [seg(0, 3), seg(3, 7), seg(9, 13), seg(14, 15), seg(7, 9), seg(13, 14), pad], axis=1).astype(BF16)


def _lane_row(vals8, lane0):
    return jnp.zeros((1, 128), F32).at[0, lane0:lane0 + 8].set(vals8.reshape(8).astype(F32))


def kernel(x_prompt, x_sample, state_delta, state_gla, c, c_ctx, w_ada, b_ada, ln_g, ln_b, ffn_w1, ffn_w2, w_in,
           conv_a, conv_qkv, delta_a_log, delta_dt_bias, delta_norm_g, gla_w2, gla_b, gla_norm_g,
           w_br_a, w_br_d, w_br_g, w_o):
    n_ctx, ctx_len, _ = x_prompt.shape
    n_lat, lat_len, _ = x_sample.shape
    n_ctx_tok = n_ctx * ctx_len
    n_lat_tok = n_lat * lat_len
    assert ctx_len == MERGE_TM and lat_len % MERGE_TM == 0 and n_ctx_tok % lat_len == 0 and n_lat <= 8

    consts = _constants()
    x = jnp.concatenate([x_prompt.reshape(n_ctx_tok, D_MODEL), x_sample.reshape(n_lat_tok, D_MODEL)], axis=0)
    cond16 = jnp.zeros((16, D_MODEL), F32).at[:n_lat].set(c).at[8].set(c_ctx)
    ada = _ada_table(cond16, w_ada, b_ada)

    sds, sgs = [], []
    for l in range(DEPTH):
        ada_l = ada[l]
        kw = dict(n_ctx_tok=n_ctx_tok, lat_len=lat_len)
        x = _ffn(x, ada_l, ffn_w1[l, 0].astype(BF16), ffn_w2[l, 0].astype(BF16), ln_g[l, 0], ln_b[l, 0], 0, **kw)
        p = _inproj(x, ada_l, _permute_w_in(w_in[l]), **kw)

        arow = _lane_row(delta_a_log[l], SM_A)
        dtb = _lane_row(delta_dt_bias[l], SM_A)
        ngd = delta_norm_g[l].reshape(1, DV_D)
        od_ctx, sd = _delta(p, consts, conv_qkv[l], arow, dtb, ngd, None, seq_len=ctx_len, n_seq=n_ctx,
                            row_blk0=0, seg=ctx_len, want_state=True)
        s0d = state_delta[:, l].astype(F32).reshape(n_lat, 2 * H_D, DK_D, DV_D)
        (od_lat,) = _delta(p, consts, conv_qkv[l], arow, dtb, ngd, s0d, seq_len=lat_len, n_seq=n_lat,
                           row_blk0=n_ctx_tok // lat_len, seg=GRID_W, want_state=False)
        od = jnp.concatenate([od_ctx, od_lat], axis=0)
        sds.append(sd.reshape(n_ctx, 2, H_D, DK_D, DV_D))

        w2p = jnp.zeros((2, 128, H_G * DK_G), F32)
        for d in range(2):
            w2p = w2p.at[d, SM_LR + d * GLA_RANK:SM_LR + (d + 1) * GLA_RANK].set(gla_w2[l, d])
        w2p = w2p.astype(BF16)
        gb = gla_b[l].reshape(2, 1, H_G * DK_G)
        ngg = gla_norm_g[l].reshape(1, DV_G)
        og_ctx, sg = _gla(p, consts, w2p, gb, ngg, None, seq_len=ctx_len, n_seq=n_ctx, row_blk0=0, want_state=True)
        s0t = jnp.swapaxes(state_gla[:, l].astype(F32), -1, -2)
        s0g = jnp.einsum("ndhvk,hg->ndhvgk", s0t, jnp.eye(H_G, dtype=F32)).reshape(
            n_lat, 2, H_G * DV_G, H_G * DK_G)
        (og_lat,) = _gla(p, consts, w2p, gb, ngg, s0g, seq_len=lat_len, n_seq=n_lat,
                         row_blk0=n_ctx_tok // lat_len, want_state=False)
        og = jnp.concatenate([og_ctx, og_lat], axis=0)
        sgs.append(jnp.swapaxes(sg.reshape(n_ctx, 2, H_G, DV_G, DK_G), -1, -2))

        x = _merge(p, od, og, x, ada_l, conv_a[l], w_br_a[l].astype(BF16), w_br_d[l].astype(BF16),
                   w_br_g[l].astype(BF16), w_o[l].astype(BF16), ln_g[l, 1], ln_b[l, 1], **kw)
        x = _ffn(x, ada_l, ffn_w1[l, 1].astype(BF16), ffn_w2[l, 1].astype(BF16), ln_g[l, 2], ln_b[l, 2], 2, **kw)

    y_prompt = x[:n_ctx_tok].reshape(n_ctx, ctx_len, D_MODEL)
    y_sample = x[n_ctx_tok:].reshape(n_lat, lat_len, D_MODEL)
    new_state_delta = jnp.stack(sds, axis=1).astype(x_prompt.dtype)
    new_state_gla = jnp.stack(sgs, axis=1).astype(x_prompt.dtype)
    return (y_prompt, y_sample, new_state_delta, new_state_gla)
```

```python
import functools

import jax
import jax.numpy as jnp
import numpy as np
from jax import lax
from jax.experimental import pallas as pl
from jax.experimental.pallas import tpu as pltpu

F32 = jnp.float32
BF16 = jnp.bfloat16

D_MODEL = 1024
DEPTH = 2
GRID_W = 64
D_FF = 2816
W_A = 512
H_D, DK_D, DV_D, CHUNK_D = 4, 128, 128, 64
H_G, DK_G, DV_G, CHUNK_G = 4, 64, 128, 16
GLA_RANK = 16
GLA_TAU = 16.0
N_ADA = 9
ALPHA = float((2 * DEPTH) ** 0.25)
LN_EPS = 1e-5
RMS_EPS = 1e-6

COL_A = 0
COL_DQ = 1536
COL_GQ = 3584
COL_MG = 5120
COL_SM = 8192
D_PROJ_PAD = 8320
SM_BETA, SM_A, SM_LR = 0, 8, 16

VMEM_LIMIT = 56 * 1024 * 1024


def _cparams(sem):
    return pltpu.CompilerParams(dimension_semantics=sem, vmem_limit_bytes=VMEM_LIMIT)


def _dot(a, b):
    return jnp.dot(a, b, preferred_element_type=F32)


def _dot_nt(a, b):
    return lax.dot_general(a, b, (((1,), (1,)), ((), ())), preferred_element_type=F32)


def _dot_tn(a, b):
    return lax.dot_general(a, b, (((0,), (0,)), ((), ())), preferred_element_type=F32)


def _sigmoid(x):
    return 1.0 / (1.0 + jnp.exp(-x))


def _silu(x):
    return x * _sigmoid(x)


def _softplus(x):
    return jnp.maximum(x, 0.0) + jnp.log1p(jnp.exp(-jnp.abs(x)))


def _split2(x):
    hi = x.astype(BF16)
    lo = (x - hi.astype(F32)).astype(BF16)
    return hi, lo


def _split3_rows(x):
    hi = x.astype(BF16)
    r = x - hi.astype(F32)
    mid = r.astype(BF16)
    lo = (r - mid.astype(F32)).astype(BF16)
    return jnp.concatenate([hi, mid, lo], axis=0)


def _mm3(a, b):
    ah, al = _split2(a)
    bh, bl = _split2(b)
    n = a.shape[0]
    p = _dot(jnp.concatenate([ah, al], axis=0), bh)
    return p[:n] + p[n:] + _dot(ah, bl)


def _mm1(a, b):
    return _dot(a.astype(BF16), b.astype(BF16))


def _layer_norm(y, g, b):
    mu = jnp.mean(y, axis=-1, keepdims=True)
    yc = y - mu
    var = jnp.mean(yc * yc, axis=-1, keepdims=True)
    return yc * lax.rsqrt(var + LN_EPS) * g + b


def _ada_kernel(cond_ref, w_ref, b_ref, o_ref):
    s = _silu(cond_ref[...]).astype(BF16)
    o_ref[0] = _dot(s, w_ref[0].astype(BF16)) + b_ref[0]


def _ada_table(cond16, w_ada, b_ada):
    n_l = w_ada.shape[0]
    tn = 1024
    out = pl.pallas_call(
        _ada_kernel,
        grid=(n_l, N_ADA * D_MODEL // tn),
        in_specs=[
            pl.BlockSpec((16, D_MODEL), lambda l, j: (0, 0)),
            pl.BlockSpec((1, D_MODEL, tn), lambda l, j: (l, 0, j)),
            pl.BlockSpec((1, 1, tn), lambda l, j: (l, 0, j)),
        ],
        out_specs=pl.BlockSpec((1, 16, tn), lambda l, j: (l, 0, j)),
        out_shape=jax.ShapeDtypeStruct((n_l, 16, N_ADA * D_MODEL), F32),
        compiler_params=_cparams(("arbitrary", "arbitrary")),
        name="ada",
    )(cond16, w_ada, b_ada.reshape(n_l, 1, N_ADA * D_MODEL))
    return out.reshape(n_l, 16, N_ADA, D_MODEL)


def _cond_row(tok0, n_ctx_tok, lat_len):
    return jnp.where(tok0 < n_ctx_tok, 8, (tok0 - n_ctx_tok) // lat_len)


MXU_TILE = 256
FF_SPLIT = (D_FF // MXU_TILE // 2) * MXU_TILE
FF_CHUNKS = ((0, FF_SPLIT), (FF_SPLIT, D_FF))


def _ffn_kernel(xc_ref, xp_ref, adac_ref, adap_ref, w1_ref, w2_ref, lng_ref, lnb_ref, o_ref, y_scr, *, j):
    @pl.when(pl.program_id(0) == 0)
    def _():
        y_scr[...] = jnp.zeros_like(y_scr)

    h = (xc_ref[...] * (1.0 + adac_ref[3 * j + 1:3 * j + 2, :]) + adac_ref[3 * j:3 * j + 1, :]).astype(BF16)
    y_new = None
    for c0, c1 in FF_CHUNKS:
        g = _dot(h, w1_ref[:, c0:c1])
        u = _dot(h, w1_ref[:, D_FF + c0:D_FF + c1])
        part = _dot((_silu(g) * u).astype(BF16), w2_ref[c0:c1, :])
        y_new = part if y_new is None else y_new + part

    y = ALPHA * xp_ref[...] + 0.5 * adap_ref[3 * j + 2:3 * j + 3, :] * y_scr[...]
    o_ref[...] = _layer_norm(y, lng_ref[...], lnb_ref[...])
    y_scr[...] = y_new


def _ffn(x, ada_l, w1, w2, lng, lnb, j, n_ctx_tok, lat_len, tm=512):
    t = x.shape[0]
    n = t // tm
    cur = lambda i: jnp.minimum(i, n - 1)
    prev = lambda i: jnp.maximum(i - 1, 0)
    cond = lambda i: _cond_row(i * tm, n_ctx_tok, lat_len)
    resident = lambda shape: pl.BlockSpec(shape, lambda i: (0,) * len(shape), pipeline_mode=pl.Buffered(1))
    return pl.pallas_call(
        functools.partial(_ffn_kernel, j=j),
        grid=(n + 1,),
        in_specs=[
            pl.BlockSpec((tm, D_MODEL), lambda i: (cur(i), 0)),
            pl.BlockSpec((tm, D_MODEL), lambda i: (prev(i), 0)),
            pl.BlockSpec((None, N_ADA, D_MODEL), lambda i: (cond(cur(i)), 0, 0)),
            pl.BlockSpec((None, N_ADA, D_MODEL), lambda i: (cond(prev(i)), 0, 0)),
            resident((D_MODEL, 2 * D_FF)),
            resident((D_FF, D_MODEL)),
            resident((1, D_MODEL)),
            resident((1, D_MODEL)),
        ],
        out_specs=pl.BlockSpec((tm, D_MODEL), lambda i: (prev(i), 0)),
        out_shape=jax.ShapeDtypeStruct((t, D_MODEL), F32),
        scratch_shapes=[pltpu.VMEM((tm, D_MODEL), F32)],
        compiler_params=_cparams(("arbitrary",)),
        name="ffn",
    )(x, x, ada_l, ada_l, w1, w2, lng.reshape(1, D_MODEL), lnb.reshape(1, D_MODEL))


def _inproj_kernel(x_ref, ada_ref, w_ref, o_ref, h_scr):
    @pl.when(pl.program_id(1) == 0)
    def _():
        h_scr[...] = (x_ref[...] * (1.0 + ada_ref[4:5, :]) + ada_ref[3:4, :]).astype(BF16)

    o_ref[...] = _dot(h_scr[...], w_ref[...])


def _inproj(x, ada_l, w_in_p, n_ctx_tok, lat_len, tm=1024, tn=1664):
    t = x.shape[0]
    cond = lambda i: _cond_row(i * tm, n_ctx_tok, lat_len)
    return pl.pallas_call(
        _inproj_kernel,
        grid=(t // tm, D_PROJ_PAD // tn),
        in_specs=[
            pl.BlockSpec((tm, D_MODEL), lambda i, n: (i, 0)),
            pl.BlockSpec((None, N_ADA, D_MODEL), lambda i, n: (cond(i), 0, 0)),
            pl.BlockSpec((D_MODEL, tn), lambda i, n: (0, n)),
        ],
        out_specs=pl.BlockSpec((tm, tn), lambda i, n: (i, n)),
        out_shape=jax.ShapeDtypeStruct((t, D_PROJ_PAD), F32),
        scratch_shapes=[pltpu.VMEM((tm, D_MODEL), BF16)],
        compiler_params=_cparams(("arbitrary", "arbitrary")),
        name="inproj",
    )(x, ada_l, w_in_p)


MERGE_TM = 256


def _merge_kernel(a_ref, m0_ref, m1_ref, m2_ref, od_ref, og_ref, x_ref, ada_ref, cw_ref,
                  wa_ref, wd_ref, wg_ref, wo_ref, lng_ref, lnb_ref, o_ref, *, n_ctx_tiles):
    i = pl.program_id(0)
    seg = jnp.where(i < n_ctx_tiles, MERGE_TM, GRID_W)
    row = lax.broadcasted_iota(jnp.int32, (MERGE_TM, W_A), 0)
    pos = jnp.bitwise_and(row, seg - 1)
    a_x = a_ref[:, 0:W_A]
    a_b = a_ref[:, W_A:2 * W_A]
    a_c = a_ref[:, 2 * W_A:3 * W_A]
    z = a_c * a_x
    z_prev = jnp.where(pos == 0, 0.0, pltpu.roll(z, 1, 0))
    z_next = jnp.where(pos == seg - 1, 0.0, pltpu.roll(z, MERGE_TM - 1, 0))
    y_a = a_b * (cw_ref[0:1, :] * z_prev + cw_ref[1:2, :] * z + cw_ref[2:3, :] * z_next)
    br_a = _dot(y_a.astype(BF16), wa_ref[...])
    br_d = _dot(od_ref[...], wd_ref[...])
    br_g = _dot(og_ref[...], wg_ref[...])
    merged = _sigmoid(m0_ref[...]) * br_a + _sigmoid(m1_ref[...]) * br_d + _sigmoid(m2_ref[...]) * br_g
    y = _dot(merged.astype(BF16), wo_ref[...])
    y = ALPHA * x_ref[...] + ada_ref[5:6, :] * y
    o_ref[...] = _layer_norm(y, lng_ref[...], lnb_ref[...])


def _merge(p, od, og, x, ada_l, conv_a, wa, wd, wg, wo, lng, lnb, n_ctx_tok, lat_len):
    t = x.shape[0]
    tm = MERGE_TM
    cond = lambda i: _cond_row(i * tm, n_ctx_tok, lat_len)
    full = lambda shape: pl.BlockSpec(shape, lambda i: (0,) * len(shape))
    mg0 = COL_MG // D_MODEL
    return pl.pallas_call(
        functools.partial(_merge_kernel, n_ctx_tiles=n_ctx_tok // tm),
        grid=(t // tm,),
        in_specs=[
            pl.BlockSpec((tm, 3 * W_A), lambda i: (i, 0)),
            pl.BlockSpec((tm, D_MODEL), lambda i: (i, mg0)),
            pl.BlockSpec((tm, D_MODEL), lambda i: (i, mg0 + 1)),
            pl.BlockSpec((tm, D_MODEL), lambda i: (i, mg0 + 2)),
            pl.BlockSpec((tm, 512), lambda i: (i, 0)),
            pl.BlockSpec((tm, 512), lambda i: (i, 0)),
            pl.BlockSpec((tm, D_MODEL), lambda i: (i, 0)),
            pl.BlockSpec((None, N_ADA, D_MODEL), lambda i: (cond(i), 0, 0)),
            full((3, W_A)),
            full((W_A, D_MODEL)), full((512, D_MODEL)), full((512, D_MODEL)), full((D_MODEL, D_MODEL)),
            full((1, D_MODEL)), full((1, D_MODEL)),
        ],
        out_specs=pl.BlockSpec((tm, D_MODEL), lambda i: (i, 0)),
        out_shape=jax.ShapeDtypeStruct((t, D_MODEL), F32),
        compiler_params=_cparams(("arbitrary",)),
        name="merge",
    )(p, p, p, p, od, og, x, ada_l, conv_a, wa, wd, wg, wo, lng.reshape(1, D_MODEL), lnb.reshape(1, D_MODEL))


C = CHUNK_D


def _bdot(a, b):
    return lax.dot_general(a, b, (((2,), (1,)), ((0,), (0,))), preferred_element_type=F32)


def _bdot_tn(a, b):
    return lax.dot_general(a, b, (((1,), (1,)), ((0,), (0,))), preferred_element_type=F32)


def _bmm3(a, b):
    ah, al = _split2(a)
    bh, bl = _split2(b)
    return _bdot(jnp.concatenate([ah, ah, al], axis=2), jnp.concatenate([bh, bl, bh], axis=1))


def _tri_inverse(m, eye, row, col):
    def same_block(shift):
        return jnp.right_shift(row, shift) == jnp.right_shift(col, shift)

    m8 = jnp.where(same_block(3), m, 0.0)
    x = eye - m8
    sq = _bmm3(m8, m8)
    x = x + _bmm3(x, sq)
    sq = _bmm3(sq, sq)
    x = x + _bmm3(x, sq)
    for shift in (4, 5, 6):
        e = jnp.where(same_block(shift) & jnp.logical_not(same_block(shift - 1)), m, 0.0)
        x = x - _bmm3(_bmm3(x, e), x)
    return x


NHD = 2 * H_D


def _delta_kernel(*refs, seq_len, seg, has_init, want_state):
    (q_ref, k_ref, v_ref, z_ref, sm_ref, cw_ref, arow_ref, dtb_ref, ng_ref, cum3_ref, tri3_ref) = refs[:11]
    pos = 11
    s0_ref = None
    if has_init:
        s0_ref = refs[pos]
        pos += 1
    o_ref = refs[pos]
    pos += 1
    sfin_ref = None
    if want_state:
        sfin_ref = refs[pos]
        pos += 1
    u_scr, wq_scr, at_scr, kd_scr, ls_scr, s_scr, of_scr, ob_scr = refs[pos:]

    n_chunks = seq_len // C
    row = lax.broadcasted_iota(jnp.int32, (NHD, C, C), 1)
    col = lax.broadcasted_iota(jnp.int32, (NHD, C, C), 2)
    fwd = lax.broadcasted_iota(jnp.int32, (NHD, C, C), 0) < H_D
    dist = jnp.where(fwd, row - col, col - row)
    strict = dist > 0
    incl = dist >= 0
    eye = jnp.where(row == col, 1.0, 0.0).astype(F32)
    row128 = lax.broadcasted_iota(jnp.int32, (C, 128), 0)

    def conv_block(ref, c0, wc0, r0, n):
        x = ref[pl.ds(r0, C), c0:c0 + 128]
        xp = pltpu.roll(x, 1, 0)
        xn = pltpu.roll(x, C - 1, 0)
        if seg == C:
            xp = jnp.where(row128 == 0, 0.0, xp)
            xn = jnp.where(row128 == C - 1, 0.0, xn)
        else:
            prev8 = ref[pl.ds(pl.multiple_of(jnp.maximum(r0 - 8, 0), 8), 8), c0:c0 + 128]
            next8 = ref[pl.ds(pl.multiple_of(jnp.minimum(r0 + C, seq_len - 8), 8), 8), c0:c0 + 128]
            pm = jnp.where(n > 0, 1.0, 0.0)
            nm = jnp.where(n < n_chunks - 1, 1.0, 0.0)
            xp = jnp.where(row128 == 0, prev8[7:8, :] * pm, xp)
            xn = jnp.where(row128 == C - 1, next8[0:1, :] * nm, xn)
        y = cw_ref[0:1, wc0:wc0 + 128] * xp + cw_ref[1:2, wc0:wc0 + 128] * x + cw_ref[2:3, wc0:wc0 + 128] * xn
        return _silu(y)

    def l2n(x):
        return x * lax.rsqrt(jnp.sum(x * x, axis=-1, keepdims=True) + RMS_EPS)

    def chunk_body(n, carry):
        r0 = pl.multiple_of(n * C, C)
        sm = sm_ref[pl.ds(r0, C), :]
        beta_full = _sigmoid(sm)
        g_full = -jnp.exp(arow_ref[...]) * _softplus(sm + dtb_ref[...])
        cs = _dot(cum3_ref[...], _split3_rows(g_full))
        qs, ks, vs, kks, qks = [], [], [], [], []
        for h in range(H_D):
            q = l2n(conv_block(q_ref, h * 128, h * 128, r0, n)) * (DK_D ** -0.5)
            k = l2n(conv_block(k_ref, h * 128, 512 + h * 128, r0, n))
            v = conv_block(v_ref, h * 128, 1024 + h * 128, r0, n)
            k16 = k.astype(BF16)
            qs.append(q)
            ks.append(k)
            vs.append(v)
            kks.append(_dot_nt(k16, k16))
            qks.append(_dot_nt(q.astype(BF16), k16))
        both = lambda xs: jnp.stack(xs + xs, axis=0)
        q, k, v, kk, qk = both(qs), both(ks), both(vs), both(kks), both(qks)
        cols = lambda arr, base: jnp.stack([arr[:, base + b:base + b + 1] for b in range(NHD)], axis=0)
        beta = cols(beta_full, SM_BETA)
        gcol = cols(g_full, SM_A)
        gsum = jnp.stack([cs[(b // H_D) * C:(b // H_D + 1) * C, SM_A + b:SM_A + b + 1] for b in range(NHD)], axis=0)
        gtot = cols(cs[2 * C:3 * C], SM_A)
        xg = jnp.where(strict, jnp.broadcast_to(gcol, (NHD, C, C)), 0.0)
        xh = xg.astype(BF16)
        xr = xg - xh.astype(F32)
        xm = xr.astype(BF16)
        xl = (xr - xm.astype(F32)).astype(BF16)
        diff = _bdot(tri3_ref[...], jnp.concatenate([xh, xm, xl], axis=1))
        gamma = jnp.where(incl, jnp.exp(jnp.minimum(diff, 0.0)), 0.0)
        m = jnp.where(strict, beta * kk * gamma, 0.0)
        eg = jnp.exp(gsum)
        rhs = jnp.concatenate([v * beta, k * (beta * eg)], axis=2)
        sol = _bmm3(_tri_inverse(m, eye, row, col), rhs)
        wq = jnp.concatenate([sol[:, :, 128:], q * eg], axis=1).astype(BF16)
        at = (qk * gamma).astype(BF16)
        kd = (k * jnp.exp(gtot - gsum)).astype(BF16)
        ls = jnp.broadcast_to(jnp.exp(gtot[:, 0:8, :]), (NHD, 8, 128))
        for d in range(2):
            t = n if d == 0 else n_chunks - 1 - n
            dst = pl.ds(t * NHD + d * H_D, H_D)
            src = slice(d * H_D, (d + 1) * H_D)
            u_scr[dst] = sol[src, :, :128]
            wq_scr[dst] = wq[src]
            at_scr[dst] = at[src]
            kd_scr[dst] = kd[src]
            ls_scr[dst] = ls[src]
        return carry

    lax.fori_loop(0, n_chunks, chunk_body, 0)

    if has_init:
        s_scr[...] = s0_ref[0]
    else:
        s_scr[...] = jnp.zeros_like(s_scr)

    def scan_body(i, carry):
        slot = pl.ds(i * NHD, NHD)
        s = s_scr[...]
        r = _bdot(wq_scr[slot], s.astype(BF16))
        v16 = (u_scr[slot] - r[:, 0:C]).astype(BF16)
        o = r[:, C:2 * C] + _bdot(at_scr[slot], v16)
        s_scr[...] = s * ls_scr[slot][:, 0:1, :] + _bdot_tn(kd_scr[slot], v16)
        for d in range(2):
            n = i if d == 0 else n_chunks - 1 - i
            r0 = pl.multiple_of(n * C, C)
            o_dst = of_scr if d == 0 else ob_scr
            for h in range(H_D):
                o_dst[pl.ds(r0, C), h * 128:(h + 1) * 128] = o[d * H_D + h]
        return carry

    lax.fori_loop(0, n_chunks, scan_body, 0)

    if want_state:
        sfin_ref[0] = s_scr[...]

    def out_body(n, carry):
        r0 = pl.multiple_of(n * C, C)
        for h in range(H_D):
            o = of_scr[pl.ds(r0, C), h * 128:(h + 1) * 128] + ob_scr[pl.ds(r0, C), h * 128:(h + 1) * 128]
            o = o * lax.rsqrt(jnp.mean(o * o, axis=-1, keepdims=True) + RMS_EPS) * ng_ref[...]
            zg = z_ref[pl.ds(r0, C), h * 128:(h + 1) * 128]
            o_ref[pl.ds(r0, C), h * 128:(h + 1) * 128] = (o * _silu(zg)).astype(BF16)
        return carry

    lax.fori_loop(0, n_chunks, out_body, 0)


def _delta(p, consts, conv_qkv, arow, dtb, ng, s0, *, seq_len, n_seq, row_blk0, seg, want_state):
    has_init = s0 is not None
    n_chunks = seq_len // C
    nhd = 2 * H_D
    cq = COL_DQ // 512
    sm_blk = COL_SM // 128
    full = lambda shape: pl.BlockSpec(shape, lambda s: (0,) * len(shape))
    in_specs = [
        pl.BlockSpec((seq_len, 512), lambda s: (s + row_blk0, cq)),
        pl.BlockSpec((seq_len, 512), lambda s: (s + row_blk0, cq + 1)),
        pl.BlockSpec((seq_len, 512), lambda s: (s + row_blk0, cq + 2)),
        pl.BlockSpec((seq_len, 512), lambda s: (s + row_blk0, cq + 3)),
        pl.BlockSpec((seq_len, 128), lambda s: (s + row_blk0, sm_blk)),
        full((3, 1536)), full((1, 128)), full((1, 128)), full((1, 128)),
        full((3 * C, 3 * C)), full((NHD, C, 3 * C)),
    ]
    args = [p, p, p, p, p, conv_qkv, arow, dtb, ng, consts["cum3_d"], consts["tri3_d"]]
    if has_init:
        in_specs.append(pl.BlockSpec((1, nhd, DK_D, DV_D), lambda s: (s, 0, 0, 0)))
        args.append(s0)
    out_specs = [pl.BlockSpec((seq_len, 512), lambda s: (s, 0))]
    out_shape = [jax.ShapeDtypeStruct((n_seq * seq_len, 512), BF16)]
    if want_state:
        out_specs.append(pl.BlockSpec((1, nhd, DK_D, DV_D), lambda s: (s, 0, 0, 0)))
        out_shape.append(jax.ShapeDtypeStruct((n_seq, nhd, DK_D, DV_D), F32))
    res = pl.pallas_call(
        functools.partial(_delta_kernel, seq_len=seq_len, seg=seg, has_init=has_init, want_state=want_state),
        grid=(n_seq,),
        in_specs=in_specs,
        out_specs=out_specs,
        out_shape=out_shape,
        scratch_shapes=[
            pltpu.VMEM((nhd * n_chunks, C, 128), F32),
            pltpu.VMEM((nhd * n_chunks, 2 * C, 128), BF16),
            pltpu.VMEM((nhd * n_chunks, C, C), BF16),
            pltpu.VMEM((nhd * n_chunks, C, 128), BF16),
            pltpu.VMEM((nhd * n_chunks, 8, 128), F32),
            pltpu.VMEM((nhd, DK_D, DV_D), F32),
            pltpu.VMEM((seq_len, 512), F32),
            pltpu.VMEM((seq_len, 512), F32),
        ],
        compiler_params=_cparams(("arbitrary",)),
        name="delta_lat" if has_init else "delta_ctx",
    )(*args)
    return res


CG = CHUNK_G
GB = 64


def _gla_kernel(*refs, seq_len, has_init, want_state):
    (q_ref, k_ref, v_ref, r_ref, sm_ref, w2_ref, b_ref, ng_ref, cum3_ref, sele_ref, bmask_ref) = refs[:11]
    pos = 11
    s0_ref = None
    if has_init:
        s0_ref = refs[pos]
        pos += 1
    o_ref = refs[pos]
    pos += 1
    sfin_ref = None
    if want_state:
        sfin_ref = refs[pos]
        pos += 1
    bc_scr, tot_scr, oacc_scr, a_scr, st_scr = refs[pos:]

    n_blocks = seq_len // GB
    n_chunks = seq_len // CG
    rowi = lax.broadcasted_iota(jnp.int32, (CG, H_G * DK_G), 0)
    qscale = DK_G ** -0.5

    def block_body(b, carry):
        r0 = pl.multiple_of(b * GB, GB)
        sm16 = sm_ref[pl.ds(r0, GB), :].astype(BF16)
        q = q_ref[pl.ds(r0, GB), :] * qscale
        k = k_ref[pl.ds(r0, GB), :]
        v = v_ref[pl.ds(r0, GB), :]
        o_blk = [None] * (GB // CG)
        for d in range(2):
            logits = _dot(sm16, w2_ref[d]) + b_ref[d]
            la = -_softplus(-logits) * (1.0 / GLA_TAU)
            cs = _dot(cum3_ref[...], _split3_rows(la))
            bc = cs[d * GB:(d + 1) * GB]
            bc_scr[d, pl.ds(r0, GB), :] = bc
            tot_scr[d, pl.ds(r0, GB), :] = cs[2 * GB:3 * GB]
            for c in range(GB // CG):
                sl = slice(c * CG, (c + 1) * CG)
                qc, kc, bcc, vc = q[sl], k[sl], bc[sl], v[sl]
                for j in range(CG):
                    keep = (rowi >= j) if d == 0 else (rowi <= j)
                    e = jnp.exp(jnp.minimum(bcc - bcc[j:j + 1, :], 0.0))
                    a_scr[j * CG:(j + 1) * CG, :] = jnp.where(keep, qc * e * kc[j:j + 1, :], 0.0).astype(BF16)
                rr = _dot(a_scr[...], sele_ref[...])
                acc = rr[0:CG] * vc[0:1, :]
                for j in range(1, CG):
                    acc = acc + rr[j * CG:(j + 1) * CG] * vc[j:j + 1, :]
                o_blk[c] = acc if d == 0 else o_blk[c] + acc
        for c in range(GB // CG):
            oacc_scr[pl.ds(r0 + c * CG, CG), :] = o_blk[c]
        return carry

    lax.fori_loop(0, n_blocks, block_body, 0)

    if has_init:
        st_scr[...] = s0_ref[0]
    else:
        st_scr[...] = jnp.zeros_like(st_scr)

    def scan_body(i, carry):
        for d in range(2):
            n = i if d == 0 else n_chunks - 1 - i
            r0 = pl.multiple_of(n * CG, CG)
            bc = bc_scr[d, pl.ds(r0, CG), :]
            tot = tot_scr[d, pl.ds(r0, CG), :]
            q = q_ref[pl.ds(r0, CG), :] * qscale
            k = k_ref[pl.ds(r0, CG), :]
            v = v_ref[pl.ds(r0, CG), :]
            st = st_scr[d]
            oacc_scr[pl.ds(r0, CG), :] += _dot_nt((q * jnp.exp(bc)).astype(BF16), st.astype(BF16))
            upd = _dot_tn(v.astype(BF16), (k * jnp.exp(tot - bc)).astype(BF16))
            st_scr[d] = st * jnp.exp(tot[0:1, :]) + upd * bmask_ref[...]
        return carry

    lax.fori_loop(0, n_chunks, scan_body, 0)

    if want_state:
        for d in range(2):
            for h in range(H_G):
                sfin_ref[0, d * H_G + h] = st_scr[d, h * DV_G:(h + 1) * DV_G, h * DK_G:(h + 1) * DK_G]

    def out_body(b, carry):
        r0 = pl.multiple_of(b * GB, GB)
        for h in range(H_G):
            o = oacc_scr[pl.ds(r0, GB), h * DV_G:(h + 1) * DV_G]
            o = o * lax.rsqrt(jnp.mean(o * o, axis=-1, keepdims=True) + RMS_EPS) * ng_ref[...]
            rg = r_ref[pl.ds(r0, GB), h * DV_G:(h + 1) * DV_G]
            o_ref[pl.ds(r0, GB), h * DV_G:(h + 1) * DV_G] = (o * _silu(rg)).astype(BF16)
        return carry

    lax.fori_loop(0, n_blocks, out_body, 0)


def _gla(p, consts, w2p, gb, ng, s0, *, seq_len, n_seq, row_blk0, want_state):
    has_init = s0 is not None
    full = lambda shape: pl.BlockSpec(shape, lambda s: (0,) * len(shape))
    hk, hv = H_G * DK_G, H_G * DV_G
    in_specs = [
        pl.BlockSpec((seq_len, hk), lambda s: (s + row_blk0, COL_GQ // hk)),
        pl.BlockSpec((seq_len, hk), lambda s: (s + row_blk0, COL_GQ // hk + 1)),
        pl.BlockSpec((seq_len, hv), lambda s: (s + row_blk0, (COL_GQ + 2 * hk) // hv)),
        pl.BlockSpec((seq_len, hv), lambda s: (s + row_blk0, (COL_GQ + 2 * hk) // hv + 1)),
        pl.BlockSpec((seq_len, 128), lambda s: (s + row_blk0, COL_SM // 128)),
        full((2, 128, hk)), full((2, 1, hk)), full((1, DV_G)),
        full((3 * GB, 3 * GB)), full((CG * CG, hv)), full((hv, hk)),
    ]
    args = [p, p, p, p, p, w2p, gb, ng, consts["cum3_g"], consts["sele"], consts["bmask"]]
    if has_init:
        in_specs.append(pl.BlockSpec((1, 2, hv, hk), lambda s: (s, 0, 0, 0)))
        args.append(s0)
    out_specs = [pl.BlockSpec((seq_len, hv), lambda s: (s, 0))]
    out_shape = [jax.ShapeDtypeStruct((n_seq * seq_len, hv), BF16)]
    if want_state:
        out_specs.append(pl.BlockSpec((1, 2 * H_G, DV_G, DK_G), lambda s: (s, 0, 0, 0)))
        out_shape.append(jax.ShapeDtypeStruct((n_seq, 2 * H_G, DV_G, DK_G), F32))
    return pl.pallas_call(
        functools.partial(_gla_kernel, seq_len=seq_len, has_init=has_init, want_state=want_state),
        grid=(n_seq,),
        in_specs=in_specs,
        out_specs=out_specs,
        out_shape=out_shape,
        scratch_shapes=[
            pltpu.VMEM((2, seq_len, hk), F32),
            pltpu.VMEM((2, seq_len, hk), F32),
            pltpu.VMEM((seq_len, hv), F32),
            pltpu.VMEM((CG * CG, hk), BF16),
            pltpu.VMEM((2, hv, hk), F32),
        ],
        compiler_params=_cparams(("arbitrary",)),
        name="gla_lat" if has_init else "gla_ctx",
    )(*args)


def _constants():
    i = np.arange(C)
    lo = (i[:, None] >= i[None, :]).astype(np.float32)
    up = (i[:, None] <= i[None, :]).astype(np.float32)
    ones = np.ones((C, C), np.float32)
    cum_d = np.concatenate([lo, up, ones], axis=0)
    blk = (i[:, None] // CG == i[None, :] // CG).astype(np.float32)
    cum_g = np.concatenate([lo * blk, up * blk, blk], axis=0)
    hk, hv = H_G * DK_G, H_G * DV_G
    sele = (np.arange(hk)[:, None] // DK_G == np.arange(hv)[None, :] // DV_G).astype(np.float32)
    bmask = (np.arange(hv)[:, None] // DV_G == np.arange(hk)[None, :] // DK_G).astype(np.float32)
    return {
        "cum3_d": jnp.asarray(np.tile(cum_d, (1, 3)), BF16),
        "cum3_g": jnp.asarray(np.tile(cum_g, (1, 3)), BF16),
        "tri3_d": jnp.asarray(np.stack([np.tile(lo, (1, 3))] * H_D + [np.tile(up, (1, 3))] * H_D), BF16),
        "sele": jnp.asarray(sele, BF16),
        "bmask": jnp.asarray(bmask, F32),
    }


def _permute_w_in(w_in_l):
    widths = (W_A, W_A, W_A, 512, 512, 512, 512, 8, 8, 256, 256, 512, 512, 32, 3 * D_MODEL)
    offs = np.concatenate([[0], np.cumsum(widths)])
    seg = lambda a, b: w_in_l[:, offs[a]:offs[b]]
    pad = jnp.zeros((D_MODEL, 128 - 48), w_in_l.dtype)
    return jnp.concatenate(
        [seg(0, 3), seg(3, 7), seg(9, 13), seg(14, 15), seg(7, 9), seg(13, 14), pad], axis=1).astype(BF16)


def _lane_row(vals8, lane0):
    return jnp.zeros((1, 128), F32).at[0, lane0:lane0 + 8].set(vals8.reshape(8).astype(F32))


def kernel(x_prompt, x_sample, state_delta, state_gla, c, c_ctx, w_ada, b_ada, ln_g, ln_b, ffn_w1, ffn_w2, w_in,
           conv_a, conv_qkv, delta_a_log, delta_dt_bias, delta_norm_g, gla_w2, gla_b, gla_norm_g,
           w_br_a, w_br_d, w_br_g, w_o):
    n_ctx, ctx_len, _ = x_prompt.shape
    n_lat, lat_len, _ = x_sample.shape
    n_ctx_tok = n_ctx * ctx_len
    n_lat_tok = n_lat * lat_len
    assert ctx_len == MERGE_TM and lat_len % MERGE_TM == 0 and n_ctx_tok % lat_len == 0 and n_lat <= 8

    consts = _constants()
    x = jnp.concatenate([x_prompt.reshape(n_ctx_tok, D_MODEL), x_sample.reshape(n_lat_tok, D_MODEL)], axis=0)
    cond16 = jnp.zeros((16, D_MODEL), F32).at[:n_lat].set(c).at[8].set(c_ctx)
    ada = _ada_table(cond16, w_ada, b_ada)

    sds, sgs = [], []
    for l in range(DEPTH):
        ada_l = ada[l]
        kw = dict(n_ctx_tok=n_ctx_tok, lat_len=lat_len)
        x = _ffn(x, ada_l, ffn_w1[l, 0].astype(BF16), ffn_w2[l, 0].astype(BF16), ln_g[l, 0], ln_b[l, 0], 0, **kw)
        p = _inproj(x, ada_l, _permute_w_in(w_in[l]), **kw)

        arow = _lane_row(delta_a_log[l], SM_A)
        dtb = _lane_row(delta_dt_bias[l], SM_A)
        ngd = delta_norm_g[l].reshape(1, DV_D)
        od_ctx, sd = _delta(p, consts, conv_qkv[l], arow, dtb, ngd, None, seq_len=ctx_len, n_seq=n_ctx,
                            row_blk0=0, seg=ctx_len, want_state=True)
        s0d = state_delta[:, l].astype(F32).reshape(n_lat, 2 * H_D, DK_D, DV_D)
        (od_lat,) = _delta(p, consts, conv_qkv[l], arow, dtb, ngd, s0d, seq_len=lat_len, n_seq=n_lat,
                           row_blk0=n_ctx_tok // lat_len, seg=GRID_W, want_state=False)
        od = jnp.concatenate([od_ctx, od_lat], axis=0)
        sds.append(sd.reshape(n_ctx, 2, H_D, DK_D, DV_D))

        w2p = jnp.zeros((2, 128, H_G * DK_G), F32)
        for d in range(2):
            w2p = w2p.at[d, SM_LR + d * GLA_RANK:SM_LR + (d + 1) * GLA_RANK].set(gla_w2[l, d])
        w2p = w2p.astype(BF16)
        gb = gla_b[l].reshape(2, 1, H_G * DK_G)
        ngg = gla_norm_g[l].reshape(1, DV_G)
        og_ctx, sg = _gla(p, consts, w2p, gb, ngg, None, seq_len=ctx_len, n_seq=n_ctx, row_blk0=0, want_state=True)
        s0t = jnp.swapaxes(state_gla[:, l].astype(F32), -1, -2)
        s0g = jnp.einsum("ndhvk,hg->ndhvgk", s0t, jnp.eye(H_G, dtype=F32)).reshape(
            n_lat, 2, H_G * DV_G, H_G * DK_G)
        (og_lat,) = _gla(p, consts, w2p, gb, ngg, s0g, seq_len=lat_len, n_seq=n_lat,
                         row_blk0=n_ctx_tok // lat_len, want_state=False)
        og = jnp.concatenate([og_ctx, og_lat], axis=0)
        sgs.append(jnp.swapaxes(sg.reshape(n_ctx, 2, H_G, DV_G, DK_G), -1, -2))

        x = _merge(p, od, og, x, ada_l, conv_a[l], w_br_a[l].astype(BF16), w_br_d[l].astype(BF16),
                   w_br_g[l].astype(BF16), w_o[l].astype(BF16), ln_g[l, 1], ln_b[l, 1], **kw)
        x = _ffn(x, ada_l, ffn_w1[l, 1].astype(BF16), ffn_w2[l, 1].astype(BF16), ln_g[l, 2], ln_b[l, 2], 2, **kw)

    y_prompt = x[:n_ctx_tok].reshape(n_ctx, ctx_len, D_MODEL)
    y_sample = x[n_ctx_tok:].reshape(n_lat, lat_len, D_MODEL)
    new_state_delta = jnp.stack(sds, axis=1).astype(x_prompt.dtype)
    new_state_gla = jnp.stack(sgs, axis=1).astype(x_prompt.dtype)
    return (y_prompt, y_sample, new_state_delta, new_state_gla)
```

```python
import functools

import jax
import jax.numpy as jnp
import numpy as np
from jax import lax
from jax.experimental import pallas as pl
from jax.experimental.pallas import tpu as pltpu

F32 = jnp.float32
BF16 = jnp.bfloat16

D_MODEL = 1024
DEPTH = 2
GRID_W = 64
D_FF = 2816
W_A = 512
H_D, DK_D, DV_D, CHUNK_D = 4, 128, 128, 64
H_G, DK_G, DV_G, CHUNK_G = 4, 64, 128, 16
GLA_RANK = 16
GLA_TAU = 16.0
N_ADA = 9
ALPHA = float((2 * DEPTH) ** 0.25)
LN_EPS = 1e-5
RMS_EPS = 1e-6

COL_A = 0
COL_DQ = 1536
COL_GQ = 3584
COL_MG = 5120
COL_SM = 8192
D_PROJ_PAD = 8320
SM_BETA, SM_A, SM_LR = 0, 8, 16

VMEM_LIMIT = 56 * 1024 * 1024


def _cparams(sem):
    return pltpu.CompilerParams(dimension_semantics=sem, vmem_limit_bytes=VMEM_LIMIT)


def _dot(a, b):
    return jnp.dot(a, b, preferred_element_type=F32)


def _dot_nt(a, b):
    return lax.dot_general(a, b, (((1,), (1,)), ((), ())), preferred_element_type=F32)


def _dot_tn(a, b):
    return lax.dot_general(a, b, (((0,), (0,)), ((), ())), preferred_element_type=F32)


def _sigmoid(x):
    return 1.0 / (1.0 + jnp.exp(-x))


def _silu(x):
    return x * _sigmoid(x)


def _softplus(x):
    return jnp.maximum(x, 0.0) + jnp.log1p(jnp.exp(-jnp.abs(x)))


def _split2(x):
    hi = x.astype(BF16)
    lo = (x - hi.astype(F32)).astype(BF16)
    return hi, lo


def _split3_rows(x):
    hi = x.astype(BF16)
    r = x - hi.astype(F32)
    mid = r.astype(BF16)
    lo = (r - mid.astype(F32)).astype(BF16)
    return jnp.concatenate([hi, mid, lo], axis=0)


def _mm3(a, b):
    ah, al = _split2(a)
    bh, bl = _split2(b)
    n = a.shape[0]
    p = _dot(jnp.concatenate([ah, al], axis=0), bh)
    return p[:n] + p[n:] + _dot(ah, bl)


def _mm1(a, b):
    return _dot(a.astype(BF16), b.astype(BF16))


def _layer_norm(y, g, b):
    mu = jnp.mean(y, axis=-1, keepdims=True)
    yc = y - mu
    var = jnp.mean(yc * yc, axis=-1, keepdims=True)
    return yc * lax.rsqrt(var + LN_EPS) * g + b


def _ada_kernel(cond_ref, w_ref, b_ref, o_ref):
    s = _silu(cond_ref[...]).astype(BF16)
    o_ref[0] = _dot(s, w_ref[0].astype(BF16)) + b_ref[0]


def _ada_table(cond16, w_ada, b_ada):
    n_l = w_ada.shape[0]
    tn = 1024
    out = pl.pallas_call(
        _ada_kernel,
        grid=(n_l, N_ADA * D_MODEL // tn),
        in_specs=[
            pl.BlockSpec((16, D_MODEL), lambda l, j: (0, 0)),
            pl.BlockSpec((1, D_MODEL, tn), lambda l, j: (l, 0, j)),
            pl.BlockSpec((1, 1, tn), lambda l, j: (l, 0, j)),
        ],
        out_specs=pl.BlockSpec((1, 16, tn), lambda l, j: (l, 0, j)),
        out_shape=jax.ShapeDtypeStruct((n_l, 16, N_ADA * D_MODEL), F32),
        compiler_params=_cparams(("arbitrary", "arbitrary")),
        name="ada",
    )(cond16, w_ada, b_ada.reshape(n_l, 1, N_ADA * D_MODEL))
    return out.reshape(n_l, 16, N_ADA, D_MODEL)


def _cond_row(tok0, n_ctx_tok, lat_len):
    return jnp.where(tok0 < n_ctx_tok, 8, (tok0 - n_ctx_tok) // lat_len)


MXU_TILE = 256
FF_SPLIT = (D_FF // MXU_TILE // 2) * MXU_TILE
FF_CHUNKS = ((0, FF_SPLIT), (FF_SPLIT, D_FF))


def _ffn_kernel(xc_ref, xp_ref, adac_ref, adap_ref, w1_ref, w2_ref, lng_ref, lnb_ref, o_ref, y_scr, *, j):
    @pl.when(pl.program_id(0) == 0)
    def _():
        y_scr[...] = jnp.zeros_like(y_scr)

    h = (xc_ref[...] * (1.0 + adac_ref[3 * j + 1:3 * j + 2, :]) + adac_ref[3 * j:3 * j + 1, :]).astype(BF16)
    y_new = None
    for c0, c1 in FF_CHUNKS:
        g = _dot(h, w1_ref[:, c0:c1])
        u = _dot(h, w1_ref[:, D_FF + c0:D_FF + c1])
        part = _dot((_silu(g) * u).astype(BF16), w2_ref[c0:c1, :])
        y_new = part if y_new is None else y_new + part

    y = ALPHA * xp_ref[...] + 0.5 * adap_ref[3 * j + 2:3 * j + 3, :] * y_scr[...]
    o_ref[...] = _layer_norm(y, lng_ref[...], lnb_ref[...])
    y_scr[...] = y_new


def _ffn(x, ada_l, w1, w2, lng, lnb, j, n_ctx_tok, lat_len, tm=512):
    t = x.shape[0]
    n = t // tm
    cur = lambda i: jnp.minimum(i, n - 1)
    prev = lambda i: jnp.maximum(i - 1, 0)
    cond = lambda i: _cond_row(i * tm, n_ctx_tok, lat_len)
    resident = lambda shape: pl.BlockSpec(shape, lambda i: (0,) * len(shape), pipeline_mode=pl.Buffered(1))
    return pl.pallas_call(
        functools.partial(_ffn_kernel, j=j),
        grid=(n + 1,),
        in_specs=[
            pl.BlockSpec((tm, D_MODEL), lambda i: (cur(i), 0)),
            pl.BlockSpec((tm, D_MODEL), lambda i: (prev(i), 0)),
            pl.BlockSpec((None, N_ADA, D_MODEL), lambda i: (cond(cur(i)), 0, 0)),
            pl.BlockSpec((None, N_ADA, D_MODEL), lambda i: (cond(prev(i)), 0, 0)),
            resident((D_MODEL, 2 * D_FF)),
            resident((D_FF, D_MODEL)),
            resident((1, D_MODEL)),
            resident((1, D_MODEL)),
        ],
        out_specs=pl.BlockSpec((tm, D_MODEL), lambda i: (prev(i), 0)),
        out_shape=jax.ShapeDtypeStruct((t, D_MODEL), F32),
        scratch_shapes=[pltpu.VMEM((tm, D_MODEL), F32)],
        compiler_params=_cparams(("arbitrary",)),
        name="ffn",
    )(x, x, ada_l, ada_l, w1, w2, lng.reshape(1, D_MODEL), lnb.reshape(1, D_MODEL))


def _inproj_kernel(x_ref, ada_ref, w_ref, o_ref, h_scr):
    @pl.when(pl.program_id(1) == 0)
    def _():
        h_scr[...] = (x_ref[...] * (1.0 + ada_ref[4:5, :]) + ada_ref[3:4, :]).astype(BF16)

    o_ref[...] = _dot(h_scr[...], w_ref[...])


def _inproj(x, ada_l, w_in_p, n_ctx_tok, lat_len, tm=1024, tn=1664):
    t = x.shape[0]
    cond = lambda i: _cond_row(i * tm, n_ctx_tok, lat_len)
    return pl.pallas_call(
        _inproj_kernel,
        grid=(t // tm, D_PROJ_PAD // tn),
        in_specs=[
            pl.BlockSpec((tm, D_MODEL), lambda i, n: (i, 0)),
            pl.BlockSpec((None, N_ADA, D_MODEL), lambda i, n: (cond(i), 0, 0)),
            pl.BlockSpec((D_MODEL, tn), lambda i, n: (0, n)),
        ],
        out_specs=pl.BlockSpec((tm, tn), lambda i, n: (i, n)),
        out_shape=jax.ShapeDtypeStruct((t, D_PROJ_PAD), F32),
        scratch_shapes=[pltpu.VMEM((tm, D_MODEL), BF16)],
        compiler_params=_cparams(("arbitrary", "arbitrary")),
        name="inproj",
    )(x, ada_l, w_in_p)


MERGE_TM = 256


def _merge_kernel(a_ref, m0_ref, m1_ref, m2_ref, od_ref, og_ref, x_ref, ada_ref, cw_ref,
                  wa_ref, wd_ref, wg_ref, wo_ref, lng_ref, lnb_ref, o_ref, *, n_ctx_tiles):
    i = pl.program_id(0)
    seg = jnp.where(i < n_ctx_tiles, MERGE_TM, GRID_W)
    row = lax.broadcasted_iota(jnp.int32, (MERGE_TM, W_A), 0)
    pos = jnp.bitwise_and(row, seg - 1)
    a_x = a_ref[:, 0:W_A]
    a_b = a_ref[:, W_A:2 * W_A]
    a_c = a_ref[:, 2 * W_A:3 * W_A]
    z = a_c * a_x
    z_prev = jnp.where(pos == 0, 0.0, pltpu.roll(z, 1, 0))
    z_next = jnp.where(pos == seg - 1, 0.0, pltpu.roll(z, MERGE_TM - 1, 0))
    y_a = a_b * (cw_ref[0:1, :] * z_prev + cw_ref[1:2, :] * z + cw_ref[2:3, :] * z_next)
    br_a = _dot(y_a.astype(BF16), wa_ref[...])
    br_d = _dot(od_ref[...], wd_ref[...])
    br_g = _dot(og_ref[...], wg_ref[...])
    merged = _sigmoid(m0_ref[...]) * br_a + _sigmoid(m1_ref[...]) * br_d + _sigmoid(m2_ref[...]) * br_g
    y = _dot(merged.astype(BF16), wo_ref[...])
    y = ALPHA * x_ref[...] + ada_ref[5:6, :] * y
    o_ref[...] = _layer_norm(y, lng_ref[...], lnb_ref[...])


def _merge(p, od, og, x, ada_l, conv_a, wa, wd, wg, wo, lng, lnb, n_ctx_tok, lat_len):
    t = x.shape[0]
    tm = MERGE_TM
    cond = lambda i: _cond_row(i * tm, n_ctx_tok, lat_len)
    full = lambda shape: pl.BlockSpec(shape, lambda i: (0,) * len(shape))
    mg0 = COL_MG // D_MODEL
    return pl.pallas_call(
        functools.partial(_merge_kernel, n_ctx_tiles=n_ctx_tok // tm),
        grid=(t // tm,),
        in_specs=[
            pl.BlockSpec((tm, 3 * W_A), lambda i: (i, 0)),
            pl.BlockSpec((tm, D_MODEL), lambda i: (i, mg0)),
            pl.BlockSpec((tm, D_MODEL), lambda i: (i, mg0 + 1)),
            pl.BlockSpec((tm, D_MODEL), lambda i: (i, mg0 + 2)),
            pl.BlockSpec((tm, 512), lambda i: (i, 0)),
            pl.BlockSpec((tm, 512), lambda i: (i, 0)),
            pl.BlockSpec((tm, D_MODEL), lambda i: (i, 0)),
            pl.BlockSpec((None, N_ADA, D_MODEL), lambda i: (cond(i), 0, 0)),
            full((3, W_A)),
            full((W_A, D_MODEL)), full((512, D_MODEL)), full((512, D_MODEL)), full((D_MODEL, D_MODEL)),
            full((1, D_MODEL)), full((1, D_MODEL)),
        ],
        out_specs=pl.BlockSpec((tm, D_MODEL), lambda i: (i, 0)),
        out_shape=jax.ShapeDtypeStruct((t, D_MODEL), F32),
        compiler_params=_cparams(("arbitrary",)),
        name="merge",
    )(p, p, p, p, od, og, x, ada_l, conv_a, wa, wd, wg, wo, lng.reshape(1, D_MODEL), lnb.reshape(1, D_MODEL))


C = CHUNK_D


def _bdot(a, b):
    return lax.dot_general(a, b, (((2,), (1,)), ((0,), (0,))), preferred_element_type=F32)


def _bdot_tn(a, b):
    return lax.dot_general(a, b, (((1,), (1,)), ((0,), (0,))), preferred_element_type=F32)


def _bmm3(a, b):
    ah, al = _split2(a)
    bh, bl = _split2(b)
    return _bdot(jnp.concatenate([ah, ah, al], axis=2), jnp.concatenate([bh, bl, bh], axis=1))


def _tri_inverse(m, eye, row, col):
    def same_block(shift):
        return jnp.right_shift(row, shift) == jnp.right_shift(col, shift)

    m8 = jnp.where(same_block(3), m, 0.0)
    x = eye - m8
    sq = _bmm3(m8, m8)
    x = x + _bmm3(x, sq)
    sq = _bmm3(sq, sq)
    x = x + _bmm3(x, sq)
    for shift in (4, 5, 6):
        e = jnp.where(same_block(shift) & jnp.logical_not(same_block(shift - 1)), m, 0.0)
        x16 = x.astype(BF16)
        x = x - _bdot(_bdot(x16, e.astype(BF16)).astype(BF16), x16)
    return x


NHD = 2 * H_D
CB = 2


def _delta_kernel(*refs, seq_len, seg, has_init, want_state):
    (q_ref, k_ref, v_ref, z_ref, sm_ref, cw_ref, arow_ref, dtb_ref, ng_ref, cum3_ref, tri3_ref) = refs[:11]
    pos = 11
    s0_ref = None
    if has_init:
        s0_ref = refs[pos]
        pos += 1
    o_ref = refs[pos]
    pos += 1
    sfin_ref = None
    if want_state:
        sfin_ref = refs[pos]
        pos += 1
    u_scr, wq_scr, at_scr, kd_scr, ls_scr, s_scr, of_scr, ob_scr = refs[pos:]

    n_chunks = seq_len // C
    nb = CB * NHD
    row = lax.broadcasted_iota(jnp.int32, (nb, C, C), 1)
    col = lax.broadcasted_iota(jnp.int32, (nb, C, C), 2)
    fwd = jnp.bitwise_and(lax.broadcasted_iota(jnp.int32, (nb, C, C), 0), NHD - 1) < H_D
    dist = jnp.where(fwd, row - col, col - row)
    strict = dist > 0
    incl = dist >= 0
    eye = jnp.where(row == col, 1.0, 0.0).astype(F32)
    row128 = lax.broadcasted_iota(jnp.int32, (C, 128), 0)

    def conv_block(ref, c0, wc0, r0, n):
        x = ref[pl.ds(r0, C), c0:c0 + 128]
        xp = pltpu.roll(x, 1, 0)
        xn = pltpu.roll(x, C - 1, 0)
        if seg == C:
            xp = jnp.where(row128 == 0, 0.0, xp)
            xn = jnp.where(row128 == C - 1, 0.0, xn)
        else:
            prev8 = ref[pl.ds(pl.multiple_of(jnp.maximum(r0 - 8, 0), 8), 8), c0:c0 + 128]
            next8 = ref[pl.ds(pl.multiple_of(jnp.minimum(r0 + C, seq_len - 8), 8), 8), c0:c0 + 128]
            pm = jnp.where(n > 0, 1.0, 0.0)
            nm = jnp.where(n < n_chunks - 1, 1.0, 0.0)
            xp = jnp.where(row128 == 0, prev8[7:8, :] * pm, xp)
            xn = jnp.where(row128 == C - 1, next8[0:1, :] * nm, xn)
        y = cw_ref[0:1, wc0:wc0 + 128] * xp + cw_ref[1:2, wc0:wc0 + 128] * x + cw_ref[2:3, wc0:wc0 + 128] * xn
        return _silu(y)

    def l2n(x):
        return x * lax.rsqrt(jnp.sum(x * x, axis=-1, keepdims=True) + RMS_EPS)

    def chunk_body(it, carry):
        qs, ks, vs, kks, qks, betas, gcols, gsums, gtots = [], [], [], [], [], [], [], [], []
        for cc in range(CB):
            n = it * CB + cc
            r0 = pl.multiple_of(n * C, C)
            sm = sm_ref[pl.ds(r0, C), :]
            beta_full = _sigmoid(sm)
            g_full = -jnp.exp(arow_ref[...]) * _softplus(sm + dtb_ref[...])
            cs = _dot(cum3_ref[...], _split3_rows(g_full))
            qh, kh, vh, kkh, qkh = [], [], [], [], []
            for h in range(H_D):
                q = l2n(conv_block(q_ref, h * 128, h * 128, r0, n)) * (DK_D ** -0.5)
                k = l2n(conv_block(k_ref, h * 128, 512 + h * 128, r0, n))
                v = conv_block(v_ref, h * 128, 1024 + h * 128, r0, n)
                k16 = k.astype(BF16)
                qh.append(q)
                kh.append(k)
                vh.append(v)
                kkh.append(_dot_nt(k16, k16))
                qkh.append(_dot_nt(q.astype(BF16), k16))
            for dst, src in ((qs, qh), (ks, kh), (vs, vh), (kks, kkh), (qks, qkh)):
                dst.extend(src + src)
            for b in range(NHD):
                betas.append(beta_full[:, SM_BETA + b:SM_BETA + b + 1])
                gcols.append(g_full[:, SM_A + b:SM_A + b + 1])
                gsums.append(cs[(b // H_D) * C:(b // H_D + 1) * C, SM_A + b:SM_A + b + 1])
                gtots.append(cs[2 * C:3 * C, SM_A + b:SM_A + b + 1])
        st = lambda xs: jnp.stack(xs, axis=0)
        q, k, v, kk, qk = st(qs), st(ks), st(vs), st(kks), st(qks)
        beta, gcol, gsum, gtot = st(betas), st(gcols), st(gsums), st(gtots)
        xg = jnp.where(strict, jnp.broadcast_to(gcol, (nb, C, C)), 0.0)
        xh = xg.astype(BF16)
        xr = xg - xh.astype(F32)
        xm = xr.astype(BF16)
        xl = (xr - xm.astype(F32)).astype(BF16)
        tri3 = jnp.concatenate([tri3_ref[...]] * CB, axis=0)
        diff = _bdot(tri3, jnp.concatenate([xh, xm, xl], axis=1))
        gamma = jnp.where(incl, jnp.exp(jnp.minimum(diff, 0.0)), 0.0)
        m = jnp.where(strict, beta * kk * gamma, 0.0)
        eg = jnp.exp(gsum)
        rhs = jnp.concatenate([v * beta, k * (beta * eg)], axis=2)
        sol = _bmm3(_tri_inverse(m, eye, row, col), rhs)
        wq = jnp.concatenate([sol[:, :, 128:], q * eg], axis=1).astype(BF16)
        at = (qk * gamma).astype(BF16)
        kd = (k * jnp.exp(gtot - gsum)).astype(BF16)
        ls = jnp.broadcast_to(jnp.exp(gtot[:, 0:8, :]), (nb, 8, 128))
        for cc in range(CB):
            n = it * CB + cc
            for d in range(2):
                t = n if d == 0 else n_chunks - 1 - n
                dst = pl.ds(t * NHD + d * H_D, H_D)
                src = slice(cc * NHD + d * H_D, cc * NHD + (d + 1) * H_D)
                u_scr[dst] = sol[src, :, :128]
                wq_scr[dst] = wq[src]
                at_scr[dst] = at[src]
                kd_scr[dst] = kd[src]
                ls_scr[dst] = ls[src]
        return carry

    lax.fori_loop(0, n_chunks // CB, chunk_body, 0)

    if has_init:
        s_scr[...] = s0_ref[0]
    else:
        s_scr[...] = jnp.zeros_like(s_scr)

    def scan_body(i, carry):
        slot = pl.ds(i * NHD, NHD)
        s = s_scr[...]
        r = _bdot(wq_scr[slot], s.astype(BF16))
        v16 = (u_scr[slot] - r[:, 0:C]).astype(BF16)
        o = r[:, C:2 * C] + _bdot(at_scr[slot], v16)
        s_scr[...] = s * ls_scr[slot][:, 0:1, :] + _bdot_tn(kd_scr[slot], v16)
        for d in range(2):
            n = i if d == 0 else n_chunks - 1 - i
            r0 = pl.multiple_of(n * C, C)
            o_dst = of_scr if d == 0 else ob_scr
            for h in range(H_D):
                o_dst[pl.ds(r0, C), h * 128:(h + 1) * 128] = o[d * H_D + h]
        return carry

    lax.fori_loop(0, n_chunks, scan_body, 0)

    if want_state:
        sfin_ref[0] = s_scr[...]

    def out_body(n, carry):
        r0 = pl.multiple_of(n * C, C)
        for h in range(H_D):
            o = of_scr[pl.ds(r0, C), h * 128:(h + 1) * 128] + ob_scr[pl.ds(r0, C), h * 128:(h + 1) * 128]
            o = o * lax.rsqrt(jnp.mean(o * o, axis=-1, keepdims=True) + RMS_EPS) * ng_ref[...]
            zg = z_ref[pl.ds(r0, C), h * 128:(h + 1) * 128]
            o_ref[pl.ds(r0, C), h * 128:(h + 1) * 128] = (o * _silu(zg)).astype(BF16)
        return carry

    lax.fori_loop(0, n_chunks, out_body, 0)


def _delta(p, consts, conv_qkv, arow, dtb, ng, s0, *, seq_len, n_seq, row_blk0, seg, want_state):
    has_init = s0 is not None
    n_chunks = seq_len // C
    nhd = 2 * H_D
    cq = COL_DQ // 512
    sm_blk = COL_SM // 128
    full = lambda shape: pl.BlockSpec(shape, lambda s: (0,) * len(shape))
    in_specs = [
        pl.BlockSpec((seq_len, 512), lambda s: (s + row_blk0, cq)),
        pl.BlockSpec((seq_len, 512), lambda s: (s + row_blk0, cq + 1)),
        pl.BlockSpec((seq_len, 512), lambda s: (s + row_blk0, cq + 2)),
        pl.BlockSpec((seq_len, 512), lambda s: (s + row_blk0, cq + 3)),
        pl.BlockSpec((seq_len, 128), lambda s: (s + row_blk0, sm_blk)),
        full((3, 1536)), full((1, 128)), full((1, 128)), full((1, 128)),
        full((3 * C, 3 * C)), full((NHD, C, 3 * C)),
    ]
    args = [p, p, p, p, p, conv_qkv, arow, dtb, ng, consts["cum3_d"], consts["tri3_d"]]
    if has_init:
        in_specs.append(pl.BlockSpec((1, nhd, DK_D, DV_D), lambda s: (s, 0, 0, 0)))
        args.append(s0)
    out_specs = [pl.BlockSpec((seq_len, 512), lambda s: (s, 0))]
    out_shape = [jax.ShapeDtypeStruct((n_seq * seq_len, 512), BF16)]
    if want_state:
        out_specs.append(pl.BlockSpec((1, nhd, DK_D, DV_D), lambda s: (s, 0, 0, 0)))
        out_shape.append(jax.ShapeDtypeStruct((n_seq, nhd, DK_D, DV_D), F32))
    res = pl.pallas_call(
        functools.partial(_delta_kernel, seq_len=seq_len, seg=seg, has_init=has_init, want_state=want_state),
        grid=(n_seq,),
        in_specs=in_specs,
        out_specs=out_specs,
        out_shape=out_shape,
        scratch_shapes=[
            pltpu.VMEM((nhd * n_chunks, C, 128), F32),
            pltpu.VMEM((nhd * n_chunks, 2 * C, 128), BF16),
            pltpu.VMEM((nhd * n_chunks, C, C), BF16),
            pltpu.VMEM((nhd * n_chunks, C, 128), BF16),
            pltpu.VMEM((nhd * n_chunks, 8, 128), F32),
            pltpu.VMEM((nhd, DK_D, DV_D), F32),
            pltpu.VMEM((seq_len, 512), F32),
            pltpu.VMEM((seq_len, 512), F32),
        ],
        compiler_params=_cparams(("arbitrary",)),
        name="delta_lat" if has_init else "delta_ctx",
    )(*args)
    return res


CG = CHUNK_G
GB = 64
HP = H_G // 2


def _gla_kernel(*refs, seq_len, has_init, want_state):
    (q_ref, k_ref, v_ref, r_ref, sm_ref, w2_ref, b_ref, ng_ref, cum3_ref, sele_ref) = refs[:10]
    pos = 10
    s0_ref = None
    if has_init:
        s0_ref = refs[pos]
        pos += 1
    o_ref = refs[pos]
    pos += 1
    sfin_ref = None
    if want_state:
        sfin_ref = refs[pos]
        pos += 1
    bc_scr, tot_scr, oacc_scr = refs[pos:]

    n_blocks = seq_len // GB
    n_chunks = seq_len // CG
    rowi = lax.broadcasted_iota(jnp.int32, (CG, H_G * DK_G), 0)
    half = CG // 2
    rowh = lax.broadcasted_iota(jnp.int32, (half, H_G * DK_G), 0)
    zero_half = jnp.zeros((half, H_G * DK_G), F32)
    qscale = DK_G ** -0.5

    def pairs(x, w):
        return jnp.stack([x[:, p * w:(p + 1) * w] for p in range(HP)], axis=0)

    def block_body(b, carry):
        r0 = pl.multiple_of(b * GB, GB)
        sm16 = sm_ref[pl.ds(r0, GB), :].astype(BF16)
        q = q_ref[pl.ds(r0, GB), :] * qscale
        k = k_ref[pl.ds(r0, GB), :]
        v = v_ref[pl.ds(r0, GB), :]
        o_blk = [None] * (GB // CG)
        rrs = []
        for d in range(2):
            logits = _dot(sm16, w2_ref[d]) + b_ref[d]
            la = -_softplus(-logits) * (1.0 / GLA_TAU)
            cs = _dot(cum3_ref[...], _split3_rows(la))
            bc = cs[d * GB:(d + 1) * GB]
            bc_scr[d, pl.ds(r0, GB), :] = bc
            tot_scr[d, pl.ds(r0, GB), :] = cs[2 * GB:3 * GB]
            for c in range(GB // CG):
                sl = slice(c * CG, (c + 1) * CG)
                qc, kc, bcc, vc = q[sl], k[sl], bc[sl], v[sl]
                pieces = []
                for j in range(CG):
                    if d == 0:
                        rs = slice(half if j >= half else 0, CG)
                        keep = (rowh >= j - half) if j >= half else (rowi >= j)
                    else:
                        rs = slice(0, half if j < half else CG)
                        keep = (rowh <= j) if j < half else (rowi <= j)
                    e = jnp.exp(jnp.minimum(bcc[rs] - bcc[j:j + 1, :], 0.0))
                    a = jnp.where(keep, qc[rs] * e * kc[j:j + 1, :], 0.0)
                    if rs.stop - rs.start < CG:
                        a = jnp.concatenate([zero_half, a] if rs.start else [a, zero_half], axis=0)
                    pieces.append(a.astype(BF16))
                rrs.append(_dot(jnp.concatenate(pieces, axis=0), sele_ref[...]))
        for d in range(2):
            for c in range(GB // CG):
                rr = rrs[d * (GB // CG) + c]
                vc = v[c * CG:(c + 1) * CG]
                top = [j for j in range(CG) if d == 1 or j < half]
                bot = [j for j in range(CG) if d == 0 or j >= half]
                acc_t = functools.reduce(jnp.add, [rr[j * CG:j * CG + half] * vc[j:j + 1, :] for j in top])
                acc_b = functools.reduce(jnp.add, [rr[j * CG + half:(j + 1) * CG] * vc[j:j + 1, :] for j in bot])
                acc = jnp.concatenate([acc_t, acc_b], axis=0)
                o_blk[c] = acc if d == 0 else o_blk[c] + acc
        for c in range(GB // CG):
            oacc_scr[pl.ds(r0 + c * CG, CG), :] = o_blk[c]
        return carry

    lax.fori_loop(0, n_blocks, block_body, 0)

    if has_init:
        st0 = (s0_ref[0, 0:HP], s0_ref[0, HP:2 * HP])
    else:
        st0 = (jnp.zeros((HP, 2 * DV_G, 2 * DK_G), F32),) * 2
    prow = lax.broadcasted_iota(jnp.int32, (2 * DV_G, 2 * DK_G), 0) // DV_G
    pcol = lax.broadcasted_iota(jnp.int32, (2 * DV_G, 2 * DK_G), 1) // DK_G
    pmask = jnp.where(prow == pcol, 1.0, 0.0).astype(F32)

    def scan_body(i, carry):
        new = []
        for d in range(2):
            st = carry[d]
            n = i if d == 0 else n_chunks - 1 - i
            r0 = pl.multiple_of(n * CG, CG)
            bc = bc_scr[d, pl.ds(r0, CG), :]
            tot = tot_scr[d, pl.ds(r0, CG), :]
            q = q_ref[pl.ds(r0, CG), :] * qscale
            k = k_ref[pl.ds(r0, CG), :]
            v = v_ref[pl.ds(r0, CG), :]
            qd = pairs(q * jnp.exp(bc), 2 * DK_G).astype(BF16)
            kd = pairs(k * jnp.exp(tot - bc), 2 * DK_G).astype(BF16)
            vp = pairs(v, 2 * DV_G).astype(BF16)
            o = lax.dot_general(qd, st.astype(BF16), (((2,), (2,)), ((0,), (0,))), preferred_element_type=F32)
            oacc_scr[pl.ds(r0, CG), :] += jnp.concatenate([o[p] for p in range(HP)], axis=1)
            upd = _bdot_tn(vp, kd)
            new.append(st * jnp.exp(pairs(tot, 2 * DK_G)[:, 0:1, :]) + upd * pmask)
        return tuple(new)

    st_fin = lax.fori_loop(0, n_chunks, scan_body, st0, unroll=4)

    if want_state:
        sfin_ref[0, 0:HP] = st_fin[0]
        sfin_ref[0, HP:2 * HP] = st_fin[1]

    def out_body(b, carry):
        r0 = pl.multiple_of(b * GB, GB)
        for h in range(H_G):
            o = oacc_scr[pl.ds(r0, GB), h * DV_G:(h + 1) * DV_G]
            o = o * lax.rsqrt(jnp.mean(o * o, axis=-1, keepdims=True) + RMS_EPS) * ng_ref[...]
            rg = r_ref[pl.ds(r0, GB), h * DV_G:(h + 1) * DV_G]
            o_ref[pl.ds(r0, GB), h * DV_G:(h + 1) * DV_G] = (o * _silu(rg)).astype(BF16)
        return carry

    lax.fori_loop(0, n_blocks, out_body, 0)


def _gla(p, consts, w2p, gb, ng, s0, *, seq_len, n_seq, row_blk0, want_state):
    has_init = s0 is not None
    full = lambda shape: pl.BlockSpec(shape, lambda s: (0,) * len(shape))
    hk, hv = H_G * DK_G, H_G * DV_G
    in_specs = [
        pl.BlockSpec((seq_len, hk), lambda s: (s + row_blk0, COL_GQ // hk)),
        pl.BlockSpec((seq_len, hk), lambda s: (s + row_blk0, COL_GQ // hk + 1)),
        pl.BlockSpec((seq_len, hv), lambda s: (s + row_blk0, (COL_GQ + 2 * hk) // hv)),
        pl.BlockSpec((seq_len, hv), lambda s: (s + row_blk0, (COL_GQ + 2 * hk) // hv + 1)),
        pl.BlockSpec((seq_len, 128), lambda s: (s + row_blk0, COL_SM // 128)),
        full((2, 128, hk)), full((2, 1, hk)), full((1, DV_G)),
        full((3 * GB, 3 * GB)), full((CG * CG, hv)),
    ]
    args = [p, p, p, p, p, w2p, gb, ng, consts["cum3_g"], consts["sele"]]
    if has_init:
        in_specs.append(pl.BlockSpec((1, 2 * HP, 2 * DV_G, 2 * DK_G), lambda s: (s, 0, 0, 0)))
        args.append(s0)
    out_specs = [pl.BlockSpec((seq_len, hv), lambda s: (s, 0))]
    out_shape = [jax.ShapeDtypeStruct((n_seq * seq_len, hv), BF16)]
    if want_state:
        out_specs.append(pl.BlockSpec((1, 2 * HP, 2 * DV_G, 2 * DK_G), lambda s: (s, 0, 0, 0)))
        out_shape.append(jax.ShapeDtypeStruct((n_seq, 2 * HP, 2 * DV_G, 2 * DK_G), F32))
    return pl.pallas_call(
        functools.partial(_gla_kernel, seq_len=seq_len, has_init=has_init, want_state=want_state),
        grid=(n_seq,),
        in_specs=in_specs,
        out_specs=out_specs,
        out_shape=out_shape,
        scratch_shapes=[
            pltpu.VMEM((2, seq_len, hk), F32),
            pltpu.VMEM((2, seq_len, hk), F32),
            pltpu.VMEM((seq_len, hv), F32),
        ],
        compiler_params=_cparams(("arbitrary",)),
        name="gla_lat" if has_init else "gla_ctx",
    )(*args)


def _constants():
    i = np.arange(C)
    lo = (i[:, None] >= i[None, :]).astype(np.float32)
    up = (i[:, None] <= i[None, :]).astype(np.float32)
    ones = np.ones((C, C), np.float32)
    cum_d = np.concatenate([lo, up, ones], axis=0)
    blk = (i[:, None] // CG == i[None, :] // CG).astype(np.float32)
    cum_g = np.concatenate([lo * blk, up * blk, blk], axis=0)
    hk, hv = H_G * DK_G, H_G * DV_G
    sele = (np.arange(hk)[:, None] // DK_G == np.arange(hv)[None, :] // DV_G).astype(np.float32)
    return {
        "cum3_d": jnp.asarray(np.tile(cum_d, (1, 3)), BF16),
        "cum3_g": jnp.asarray(np.tile(cum_g, (1, 3)), BF16),
        "tri3_d": jnp.asarray(np.stack([np.tile(lo, (1, 3))] * H_D + [np.tile(up, (1, 3))] * H_D), BF16),
        "sele": jnp.asarray(sele, BF16),
    }


def _permute_w_in(w_in_l):
    widths = (W_A, W_A, W_A, 512, 512, 512, 512, 8, 8, 256, 256, 512, 512, 32, 3 * D_MODEL)
    offs = np.concatenate([[0], np.cumsum(widths)])
    seg = lambda a, b: w_in_l[:, offs[a]:offs[b]]
    pad = jnp.zeros((D_MODEL, 128 - 48), w_in_l.dtype)
    return jnp.concatenate(
        [seg(0, 3), seg(3, 7), seg(9, 13), seg(14, 15), seg(7, 9), seg(13, 14), pad], axis=1).astype(BF16)


def _gla_pack_state(s):
    n = s.shape[0]
    st = jnp.swapaxes(s, -1, -2).reshape(n, 2, HP, 2, DV_G, DK_G)
    packed = jnp.einsum("ndpavk,ab->ndpavbk", st, jnp.eye(2, dtype=s.dtype))
    return packed.reshape(n, 2 * HP, 2 * DV_G, 2 * DK_G)


def _gla_unpack_state(sp):
    n = sp.shape[0]
    s6 = sp.reshape(n, 2, HP, 2, DV_G, 2, DK_G)
    diag = jnp.stack([s6[:, :, :, a, :, a, :] for a in range(2)], axis=3)
    return jnp.swapaxes(diag.reshape(n, 2, H_G, DV_G, DK_G), -1, -2)


def _lane_row(vals8, lane0):
    return jnp.zeros((1, 128), F32).at[0, lane0:lane0 + 8].set(vals8.reshape(8).astype(F32))


def kernel(x_prompt, x_sample, state_delta, state_gla, c, c_ctx, w_ada, b_ada, ln_g, ln_b, ffn_w1, ffn_w2, w_in,
           conv_a, conv_qkv, delta_a_log, delta_dt_bias, delta_norm_g, gla_w2, gla_b, gla_norm_g,
           w_br_a, w_br_d, w_br_g, w_o):
    n_ctx, ctx_len, _ = x_prompt.shape
    n_lat, lat_len, _ = x_sample.shape
    n_ctx_tok = n_ctx * ctx_len
    n_lat_tok = n_lat * lat_len
    assert ctx_len == MERGE_TM and lat_len % MERGE_TM == 0 and n_ctx_tok % lat_len == 0 and n_lat <= 8

    consts = _constants()
    x = jnp.concatenate([x_prompt.reshape(n_ctx_tok, D_MODEL), x_sample.reshape(n_lat_tok, D_MODEL)], axis=0)
    cond16 = jnp.zeros((16, D_MODEL), F32).at[:n_lat].set(c).at[8].set(c_ctx)
    ada = _ada_table(cond16, w_ada, b_ada)

    sds, sgs = [], []
    for l in range(DEPTH):
        ada_l = ada[l]
        kw = dict(n_ctx_tok=n_ctx_tok, lat_len=lat_len)
        x = _ffn(x, ada_l, ffn_w1[l, 0].astype(BF16), ffn_w2[l, 0].astype(BF16), ln_g[l, 0], ln_b[l, 0], 0, **kw)
        p = _inproj(x, ada_l, _permute_w_in(w_in[l]), **kw)

        arow = _lane_row(delta_a_log[l], SM_A)
        dtb = _lane_row(delta_dt_bias[l], SM_A)
        ngd = delta_norm_g[l].reshape(1, DV_D)
        od_ctx, sd = _delta(p, consts, conv_qkv[l], arow, dtb, ngd, None, seq_len=ctx_len, n_seq=n_ctx,
                            row_blk0=0, seg=ctx_len, want_state=True)
        s0d = state_delta[:, l].astype(F32).reshape(n_lat, 2 * H_D, DK_D, DV_D)
        (od_lat,) = _delta(p, consts, conv_qkv[l], arow, dtb, ngd, s0d, seq_len=lat_len, n_seq=n_lat,
                           row_blk0=n_ctx_tok // lat_len, seg=GRID_W, want_state=False)
        od = jnp.concatenate([od_ctx, od_lat], axis=0)
        sds.append(sd.reshape(n_ctx, 2, H_D, DK_D, DV_D))

        w2p = jnp.zeros((2, 128, H_G * DK_G), F32)
        for d in range(2):
            w2p = w2p.at[d, SM_LR + d * GLA_RANK:SM_LR + (d + 1) * GLA_RANK].set(gla_w2[l, d])
        w2p = w2p.astype(BF16)
        gb = gla_b[l].reshape(2, 1, H_G * DK_G)
        ngg = gla_norm_g[l].reshape(1, DV_G)
        og_ctx, sg = _gla(p, consts, w2p, gb, ngg, None, seq_len=ctx_len, n_seq=n_ctx, row_blk0=0, want_state=True)
        s0g = _gla_pack_state(state_gla[:, l].astype(F32))
        (og_lat,) = _gla(p, consts, w2p, gb, ngg, s0g, seq_len=lat_len, n_seq=n_lat,
                         row_blk0=n_ctx_tok // lat_len, want_state=False)
        og = jnp.concatenate([og_ctx, og_lat], axis=0)
        sgs.append(_gla_unpack_state(sg))

        x = _merge(p, od, og, x, ada_l, conv_a[l], w_br_a[l].astype(BF16), w_br_d[l].astype(BF16),
                   w_br_g[l].astype(BF16), w_o[l].astype(BF16), ln_g[l, 1], ln_b[l, 1], **kw)
        x = _ffn(x, ada_l, ffn_w1[l, 1].astype(BF16), ffn_w2[l, 1].astype(BF16), ln_g[l, 2], ln_b[l, 2], 2, **kw)

    y_prompt = x[:n_ctx_tok].reshape(n_ctx, ctx_len, D_MODEL)
    y_sample = x[n_ctx_tok:].reshape(n_lat, lat_len, D_MODEL)
    new_state_delta = jnp.stack(sds, axis=1).astype(x_prompt.dtype)
    new_state_gla = jnp.stack(sgs, axis=1).astype(x_prompt.dtype)
    return (y_prompt, y_sample, new_state_delta, new_state_gla)
```

```python
import functools

import jax
import jax.numpy as jnp
import numpy as np
from jax import lax
from jax.experimental import pallas as pl
from jax.experimental.pallas import tpu as pltpu

F32 = jnp.float32
BF16 = jnp.bfloat16

D_MODEL = 1024
DEPTH = 2
GRID_W = 64
D_FF = 2816
W_A = 512
H_D, DK_D, DV_D, CHUNK_D = 4, 128, 128, 64
H_G, DK_G, DV_G, CHUNK_G = 4, 64, 128, 16
GLA_RANK = 16
GLA_TAU = 16.0
N_ADA = 9
ALPHA = float((2 * DEPTH) ** 0.25)
LN_EPS = 1e-5
RMS_EPS = 1e-6

COL_A = 0
COL_DQ = 1536
COL_GQ = 3584
COL_MG = 5120
COL_SM = 8192
D_PROJ_PAD = 8320
SM_BETA, SM_A, SM_LR = 0, 8, 16

VMEM_LIMIT = 56 * 1024 * 1024


def _cparams(sem):
    return pltpu.CompilerParams(dimension_semantics=sem, vmem_limit_bytes=VMEM_LIMIT)


def _dot(a, b):
    return jnp.dot(a, b, preferred_element_type=F32)


def _dot_nt(a, b):
    return lax.dot_general(a, b, (((1,), (1,)), ((), ())), preferred_element_type=F32)


def _dot_tn(a, b):
    return lax.dot_general(a, b, (((0,), (0,)), ((), ())), preferred_element_type=F32)


def _sigmoid(x):
    return 1.0 / (1.0 + jnp.exp(-x))


def _silu(x):
    return x * _sigmoid(x)


def _softplus(x):
    return jnp.maximum(x, 0.0) + jnp.log1p(jnp.exp(-jnp.abs(x)))


def _split2(x):
    hi = x.astype(BF16)
    lo = (x - hi.astype(F32)).astype(BF16)
    return hi, lo


def _split3_rows(x):
    hi = x.astype(BF16)
    r = x - hi.astype(F32)
    mid = r.astype(BF16)
    lo = (r - mid.astype(F32)).astype(BF16)
    return jnp.concatenate([hi, mid, lo], axis=0)


def _mm3(a, b):
    ah, al = _split2(a)
    bh, bl = _split2(b)
    n = a.shape[0]
    p = _dot(jnp.concatenate([ah, al], axis=0), bh)
    return p[:n] + p[n:] + _dot(ah, bl)


def _mm1(a, b):
    return _dot(a.astype(BF16), b.astype(BF16))


def _layer_norm(y, g, b):
    mu = jnp.mean(y, axis=-1, keepdims=True)
    yc = y - mu
    var = jnp.mean(yc * yc, axis=-1, keepdims=True)
    return yc * lax.rsqrt(var + LN_EPS) * g + b


def _ada_kernel(cond_ref, w_ref, b_ref, o_ref):
    s = _silu(cond_ref[...]).astype(BF16)
    o_ref[0] = _dot(s, w_ref[0].astype(BF16)) + b_ref[0]


def _ada_table(cond16, w_ada, b_ada):
    n_l = w_ada.shape[0]
    tn = 1024
    out = pl.pallas_call(
        _ada_kernel,
        grid=(n_l, N_ADA * D_MODEL // tn),
        in_specs=[
            pl.BlockSpec((16, D_MODEL), lambda l, j: (0, 0)),
            pl.BlockSpec((1, D_MODEL, tn), lambda l, j: (l, 0, j)),
            pl.BlockSpec((1, 1, tn), lambda l, j: (l, 0, j)),
        ],
        out_specs=pl.BlockSpec((1, 16, tn), lambda l, j: (l, 0, j)),
        out_shape=jax.ShapeDtypeStruct((n_l, 16, N_ADA * D_MODEL), F32),
        compiler_params=_cparams(("arbitrary", "arbitrary")),
        name="ada",
    )(cond16, w_ada, b_ada.reshape(n_l, 1, N_ADA * D_MODEL))
    return out.reshape(n_l, 16, N_ADA, D_MODEL)


def _cond_row(tok0, n_ctx_tok, lat_len):
    return jnp.where(tok0 < n_ctx_tok, 8, (tok0 - n_ctx_tok) // lat_len)


MXU_TILE = 256
FF_SPLIT = (D_FF // MXU_TILE // 2) * MXU_TILE
FF_CHUNKS = ((0, FF_SPLIT), (FF_SPLIT, D_FF))


def _ffn_kernel(xc_ref, xp_ref, adac_ref, adap_ref, w1_ref, w2_ref, lng_ref, lnb_ref, o_ref, y_scr, *, j):
    @pl.when(pl.program_id(0) == 0)
    def _():
        y_scr[...] = jnp.zeros_like(y_scr)

    h = (xc_ref[...] * (1.0 + adac_ref[3 * j + 1:3 * j + 2, :]) + adac_ref[3 * j:3 * j + 1, :]).astype(BF16)
    y_new = None
    for c0, c1 in FF_CHUNKS:
        g = _dot(h, w1_ref[:, c0:c1])
        u = _dot(h, w1_ref[:, D_FF + c0:D_FF + c1])
        part = _dot((_silu(g) * u).astype(BF16), w2_ref[c0:c1, :])
        y_new = part if y_new is None else y_new + part

    y = ALPHA * xp_ref[...] + 0.5 * adap_ref[3 * j + 2:3 * j + 3, :] * y_scr[...]
    o_ref[...] = _layer_norm(y, lng_ref[...], lnb_ref[...])
    y_scr[...] = y_new


def _ffn(x, ada_l, w1, w2, lng, lnb, j, n_ctx_tok, lat_len, tm=512):
    t = x.shape[0]
    n = t // tm
    cur = lambda i: jnp.minimum(i, n - 1)
    prev = lambda i: jnp.maximum(i - 1, 0)
    cond = lambda i: _cond_row(i * tm, n_ctx_tok, lat_len)
    resident = lambda shape: pl.BlockSpec(shape, lambda i: (0,) * len(shape), pipeline_mode=pl.Buffered(1))
    return pl.pallas_call(
        functools.partial(_ffn_kernel, j=j),
        grid=(n + 1,),
        in_specs=[
            pl.BlockSpec((tm, D_MODEL), lambda i: (cur(i), 0)),
            pl.BlockSpec((tm, D_MODEL), lambda i: (prev(i), 0)),
            pl.BlockSpec((None, N_ADA, D_MODEL), lambda i: (cond(cur(i)), 0, 0)),
            pl.BlockSpec((None, N_ADA, D_MODEL), lambda i: (cond(prev(i)), 0, 0)),
            resident((D_MODEL, 2 * D_FF)),
            resident((D_FF, D_MODEL)),
            resident((1, D_MODEL)),
            resident((1, D_MODEL)),
        ],
        out_specs=pl.BlockSpec((tm, D_MODEL), lambda i: (prev(i), 0)),
        out_shape=jax.ShapeDtypeStruct((t, D_MODEL), F32),
        scratch_shapes=[pltpu.VMEM((tm, D_MODEL), F32)],
        compiler_params=_cparams(("arbitrary",)),
        name="ffn",
    )(x, x, ada_l, ada_l, w1, w2, lng.reshape(1, D_MODEL), lnb.reshape(1, D_MODEL))


INPROJ_NC = 8 * MXU_TILE


def _inproj_kernel(x_ref, ada_ref, w_ref, p_ref, sm_ref):
    h = (x_ref[...] * (1.0 + ada_ref[4:5, :]) + ada_ref[3:4, :]).astype(BF16)
    for c0 in range(0, COL_SM, INPROJ_NC):
        p_ref[:, c0:c0 + INPROJ_NC] = _dot(h, w_ref[:, c0:c0 + INPROJ_NC]).astype(BF16)
    sm_ref[...] = _dot(h, w_ref[:, COL_SM:D_PROJ_PAD])


def _inproj(x, ada_l, w_in_p, n_ctx_tok, lat_len, tm=512):
    t = x.shape[0]
    cond = lambda i: _cond_row(i * tm, n_ctx_tok, lat_len)
    return pl.pallas_call(
        _inproj_kernel,
        grid=(t // tm,),
        in_specs=[
            pl.BlockSpec((tm, D_MODEL), lambda i: (i, 0)),
            pl.BlockSpec((None, N_ADA, D_MODEL), lambda i: (cond(i), 0, 0)),
            pl.BlockSpec((D_MODEL, D_PROJ_PAD), lambda i: (0, 0), pipeline_mode=pl.Buffered(1)),
        ],
        out_specs=[pl.BlockSpec((tm, COL_SM), lambda i: (i, 0)),
                   pl.BlockSpec((tm, D_PROJ_PAD - COL_SM), lambda i: (i, 0))],
        out_shape=[jax.ShapeDtypeStruct((t, COL_SM), BF16),
                   jax.ShapeDtypeStruct((t, D_PROJ_PAD - COL_SM), F32)],
        compiler_params=_cparams(("arbitrary",)),
        name="inproj",
    )(x, ada_l, w_in_p)


MERGE_TM = 256


def _merge_kernel(a_ref, m0_ref, m1_ref, m2_ref, od_ref, og_ref, x_ref, ada_ref, cw_ref,
                  wa_ref, wd_ref, wg_ref, wo_ref, lng_ref, lnb_ref, o_ref, *, n_ctx_tiles):
    i = pl.program_id(0)
    seg = jnp.where(i < n_ctx_tiles, MERGE_TM, GRID_W)
    row = lax.broadcasted_iota(jnp.int32, (MERGE_TM, W_A), 0)
    pos = jnp.bitwise_and(row, seg - 1)
    a_x = a_ref[:, 0:W_A].astype(F32)
    a_b = a_ref[:, W_A:2 * W_A].astype(F32)
    a_c = a_ref[:, 2 * W_A:3 * W_A].astype(F32)
    z = a_c * a_x
    z_prev = jnp.where(pos == 0, 0.0, pltpu.roll(z, 1, 0))
    z_next = jnp.where(pos == seg - 1, 0.0, pltpu.roll(z, MERGE_TM - 1, 0))
    y_a = a_b * (cw_ref[0:1, :] * z_prev + cw_ref[1:2, :] * z + cw_ref[2:3, :] * z_next)
    br_a = _dot(y_a.astype(BF16), wa_ref[...])
    br_d = _dot(od_ref[...], wd_ref[...])
    br_g = _dot(og_ref[...], wg_ref[...])
    gate = lambda m_ref: _sigmoid(m_ref[...].astype(F32))
    merged = gate(m0_ref) * br_a + gate(m1_ref) * br_d + gate(m2_ref) * br_g
    y = _dot(merged.astype(BF16), wo_ref[...])
    y = ALPHA * x_ref[...] + ada_ref[5:6, :] * y
    o_ref[...] = _layer_norm(y, lng_ref[...], lnb_ref[...])


def _merge(p, od, og, x, ada_l, conv_a, wa, wd, wg, wo, lng, lnb, n_ctx_tok, lat_len):
    t = x.shape[0]
    tm = MERGE_TM
    cond = lambda i: _cond_row(i * tm, n_ctx_tok, lat_len)
    full = lambda shape: pl.BlockSpec(shape, lambda i: (0,) * len(shape))
    mg0 = COL_MG // D_MODEL
    return pl.pallas_call(
        functools.partial(_merge_kernel, n_ctx_tiles=n_ctx_tok // tm),
        grid=(t // tm,),
        in_specs=[
            pl.BlockSpec((tm, 3 * W_A), lambda i: (i, 0)),
            pl.BlockSpec((tm, D_MODEL), lambda i: (i, mg0)),
            pl.BlockSpec((tm, D_MODEL), lambda i: (i, mg0 + 1)),
            pl.BlockSpec((tm, D_MODEL), lambda i: (i, mg0 + 2)),
            pl.BlockSpec((tm, 512), lambda i: (i, 0)),
            pl.BlockSpec((tm, 512), lambda i: (i, 0)),
            pl.BlockSpec((tm, D_MODEL), lambda i: (i, 0)),
            pl.BlockSpec((None, N_ADA, D_MODEL), lambda i: (cond(i), 0, 0)),
            full((3, W_A)),
            full((W_A, D_MODEL)), full((512, D_MODEL)), full((512, D_MODEL)), full((D_MODEL, D_MODEL)),
            full((1, D_MODEL)), full((1, D_MODEL)),
        ],
        out_specs=pl.BlockSpec((tm, D_MODEL), lambda i: (i, 0)),
        out_shape=jax.ShapeDtypeStruct((t, D_MODEL), F32),
        compiler_params=_cparams(("arbitrary",)),
        name="merge",
    )(p, p, p, p, od, og, x, ada_l, conv_a, wa, wd, wg, wo, lng.reshape(1, D_MODEL), lnb.reshape(1, D_MODEL))


C = CHUNK_D


def _bdot(a, b):
    return lax.dot_general(a, b, (((2,), (1,)), ((0,), (0,))), preferred_element_type=F32)


def _bdot_tn(a, b):
    return lax.dot_general(a, b, (((1,), (1,)), ((0,), (0,))), preferred_element_type=F32)


def _bmm3(a, b):
    ah, al = _split2(a)
    bh, bl = _split2(b)
    return _bdot(jnp.concatenate([ah, ah, al], axis=2), jnp.concatenate([bh, bl, bh], axis=1))


def _tri_inverse(m, eye, row, col):
    def same_block(shift):
        return jnp.right_shift(row, shift) == jnp.right_shift(col, shift)

    m8 = jnp.where(same_block(3), m, 0.0)
    x = eye - m8
    sq = _bmm3(m8, m8)
    x = x + _bmm3(x, sq)
    sq = _bmm3(sq, sq)
    x = x + _bmm3(x, sq)
    for shift in (4, 5, 6):
        e = jnp.where(same_block(shift) & jnp.logical_not(same_block(shift - 1)), m, 0.0)
        x16 = x.astype(BF16)
        x = x - _bdot(_bdot(x16, e.astype(BF16)).astype(BF16), x16)
    return x


NHD = 2 * H_D
CB = 2
HALO = 16


def _delta_kernel(*refs, seq_len, seg, has_init, want_state):
    (q_ref, k_ref, v_ref, z_ref, sm_ref, cw_ref, arow_ref, dtb_ref, ng_ref, cum3_ref, tri3_ref) = refs[:11]
    pos = 11
    s0_ref = None
    if has_init:
        s0_ref = refs[pos]
        pos += 1
    o_ref = refs[pos]
    pos += 1
    sfin_ref = None
    if want_state:
        sfin_ref = refs[pos]
        pos += 1
    u_scr, wq_scr, at_scr, kd_scr, ls_scr, s_scr, of_scr, ob_scr = refs[pos:]

    n_chunks = seq_len // C
    nb = CB * NHD
    row = lax.broadcasted_iota(jnp.int32, (nb, C, C), 1)
    col = lax.broadcasted_iota(jnp.int32, (nb, C, C), 2)
    fwd = jnp.bitwise_and(lax.broadcasted_iota(jnp.int32, (nb, C, C), 0), NHD - 1) < H_D
    dist = jnp.where(fwd, row - col, col - row)
    strict = dist > 0
    incl = dist >= 0
    eye = jnp.where(row == col, 1.0, 0.0).astype(F32)
    row128 = lax.broadcasted_iota(jnp.int32, (C, 128), 0)

    def conv_block(ref, c0, wc0, r0, n):
        x = ref[pl.ds(r0, C), c0:c0 + 128].astype(F32)
        xp = pltpu.roll(x, 1, 0)
        xn = pltpu.roll(x, C - 1, 0)
        if seg == C:
            xp = jnp.where(row128 == 0, 0.0, xp)
            xn = jnp.where(row128 == C - 1, 0.0, xn)
        else:
            prev = ref[pl.ds(pl.multiple_of(jnp.maximum(r0 - HALO, 0), HALO), HALO), c0:c0 + 128].astype(F32)
            nxt = ref[pl.ds(pl.multiple_of(jnp.minimum(r0 + C, seq_len - HALO), HALO), HALO), c0:c0 + 128].astype(F32)
            pm = jnp.where(n > 0, 1.0, 0.0)
            nm = jnp.where(n < n_chunks - 1, 1.0, 0.0)
            xp = jnp.where(row128 == 0, prev[HALO - 1:HALO, :] * pm, xp)
            xn = jnp.where(row128 == C - 1, nxt[0:1, :] * nm, xn)
        y = cw_ref[0:1, wc0:wc0 + 128] * xp + cw_ref[1:2, wc0:wc0 + 128] * x + cw_ref[2:3, wc0:wc0 + 128] * xn
        return _silu(y)

    def l2n(x):
        return x * lax.rsqrt(jnp.sum(x * x, axis=-1, keepdims=True) + RMS_EPS)

    def chunk_body(it, carry):
        qs, ks, vs, kks, qks, betas, gcols, gsums, gtots = [], [], [], [], [], [], [], [], []
        for cc in range(CB):
            n = it * CB + cc
            r0 = pl.multiple_of(n * C, C)
            sm = sm_ref[pl.ds(r0, C), :]
            beta_full = _sigmoid(sm)
            g_full = -jnp.exp(arow_ref[...]) * _softplus(sm + dtb_ref[...])
            cs = _dot(cum3_ref[...], _split3_rows(g_full))
            qh, kh, vh, kkh, qkh = [], [], [], [], []
            for h in range(H_D):
                q = l2n(conv_block(q_ref, h * 128, h * 128, r0, n)) * (DK_D ** -0.5)
                k = l2n(conv_block(k_ref, h * 128, 512 + h * 128, r0, n))
                v = conv_block(v_ref, h * 128, 1024 + h * 128, r0, n)
                k16 = k.astype(BF16)
                qh.append(q)
                kh.append(k)
                vh.append(v)
                kkh.append(_dot_nt(k16, k16))
                qkh.append(_dot_nt(q.astype(BF16), k16))
            for dst, src in ((qs, qh), (ks, kh), (vs, vh), (kks, kkh), (qks, qkh)):
                dst.extend(src + src)
            for b in range(NHD):
                betas.append(beta_full[:, SM_BETA + b:SM_BETA + b + 1])
                gcols.append(g_full[:, SM_A + b:SM_A + b + 1])
                gsums.append(cs[(b // H_D) * C:(b // H_D + 1) * C, SM_A + b:SM_A + b + 1])
                gtots.append(cs[2 * C:3 * C, SM_A + b:SM_A + b + 1])
        st = lambda xs: jnp.stack(xs, axis=0)
        q, k, v, kk, qk = st(qs), st(ks), st(vs), st(kks), st(qks)
        beta, gcol, gsum, gtot = st(betas), st(gcols), st(gsums), st(gtots)
        xg = jnp.where(strict, jnp.broadcast_to(gcol, (nb, C, C)), 0.0)
        xh = xg.astype(BF16)
        xr = xg - xh.astype(F32)
        xm = xr.astype(BF16)
        xl = (xr - xm.astype(F32)).astype(BF16)
        tri3 = jnp.concatenate([tri3_ref[...]] * CB, axis=0)
        diff = _bdot(tri3, jnp.concatenate([xh, xm, xl], axis=1))
        gamma = jnp.where(incl, jnp.exp(jnp.minimum(diff, 0.0)), 0.0)
        m = jnp.where(strict, beta * kk * gamma, 0.0)
        eg = jnp.exp(gsum)
        rhs = jnp.concatenate([v * beta, k * (beta * eg)], axis=2)
        sol = _bmm3(_tri_inverse(m, eye, row, col), rhs)
        wq = jnp.concatenate([sol[:, :, 128:], q * eg], axis=1).astype(BF16)
        at = (qk * gamma).astype(BF16)
        kd = (k * jnp.exp(gtot - gsum)).astype(BF16)
        ls = jnp.broadcast_to(jnp.exp(gtot[:, 0:8, :]), (nb, 8, 128))
        for cc in range(CB):
            n = it * CB + cc
            for d in range(2):
                t = n if d == 0 else n_chunks - 1 - n
                dst = pl.ds(t * NHD + d * H_D, H_D)
                src = slice(cc * NHD + d * H_D, cc * NHD + (d + 1) * H_D)
                u_scr[dst] = sol[src, :, :128]
                wq_scr[dst] = wq[src]
                at_scr[dst] = at[src]
                kd_scr[dst] = kd[src]
                ls_scr[dst] = ls[src]
        return carry

    lax.fori_loop(0, n_chunks // CB, chunk_body, 0)

    if has_init:
        s_scr[...] = s0_ref[0]
    else:
        s_scr[...] = jnp.zeros_like(s_scr)

    def scan_body(i, carry):
        slot = pl.ds(i * NHD, NHD)
        s = s_scr[...]
        r = _bdot(wq_scr[slot], s.astype(BF16))
        v16 = (u_scr[slot] - r[:, 0:C]).astype(BF16)
        o = r[:, C:2 * C] + _bdot(at_scr[slot], v16)
        s_scr[...] = s * ls_scr[slot][:, 0:1, :] + _bdot_tn(kd_scr[slot], v16)
        for d in range(2):
            n = i if d == 0 else n_chunks - 1 - i
            r0 = pl.multiple_of(n * C, C)
            o_dst = of_scr if d == 0 else ob_scr
            for h in range(H_D):
                o_dst[pl.ds(r0, C), h * 128:(h + 1) * 128] = o[d * H_D + h]
        return carry

    lax.fori_loop(0, n_chunks, scan_body, 0)

    if want_state:
        sfin_ref[0] = s_scr[...]

    def out_body(n, carry):
        r0 = pl.multiple_of(n * C, C)
        for h in range(H_D):
            o = of_scr[pl.ds(r0, C), h * 128:(h + 1) * 128] + ob_scr[pl.ds(r0, C), h * 128:(h + 1) * 128]
            o = o * lax.rsqrt(jnp.mean(o * o, axis=-1, keepdims=True) + RMS_EPS) * ng_ref[...]
            zg = z_ref[pl.ds(r0, C), h * 128:(h + 1) * 128].astype(F32)
            o_ref[pl.ds(r0, C), h * 128:(h + 1) * 128] = (o * _silu(zg)).astype(BF16)
        return carry

    lax.fori_loop(0, n_chunks, out_body, 0)


def _delta(p, sm, consts, conv_qkv, arow, dtb, ng, s0, *, seq_len, n_seq, row_blk0, seg, want_state):
    has_init = s0 is not None
    n_chunks = seq_len // C
    nhd = 2 * H_D
    cq = COL_DQ // 512
    full = lambda shape: pl.BlockSpec(shape, lambda s: (0,) * len(shape))
    in_specs = [
        pl.BlockSpec((seq_len, 512), lambda s: (s + row_blk0, cq)),
        pl.BlockSpec((seq_len, 512), lambda s: (s + row_blk0, cq + 1)),
        pl.BlockSpec((seq_len, 512), lambda s: (s + row_blk0, cq + 2)),
        pl.BlockSpec((seq_len, 512), lambda s: (s + row_blk0, cq + 3)),
        pl.BlockSpec((seq_len, 128), lambda s: (s + row_blk0, 0)),
        full((3, 1536)), full((1, 128)), full((1, 128)), full((1, 128)),
        full((3 * C, 3 * C)), full((NHD, C, 3 * C)),
    ]
    args = [p, p, p, p, sm, conv_qkv, arow, dtb, ng, consts["cum3_d"], consts["tri3_d"]]
    if has_init:
        in_specs.append(pl.BlockSpec((1, nhd, DK_D, DV_D), lambda s: (s, 0, 0, 0)))
        args.append(s0)
    out_specs = [pl.BlockSpec((seq_len, 512), lambda s: (s, 0))]
    out_shape = [jax.ShapeDtypeStruct((n_seq * seq_len, 512), BF16)]
    if want_state:
        out_specs.append(pl.BlockSpec((1, nhd, DK_D, DV_D), lambda s: (s, 0, 0, 0)))
        out_shape.append(jax.ShapeDtypeStruct((n_seq, nhd, DK_D, DV_D), F32))
    res = pl.pallas_call(
        functools.partial(_delta_kernel, seq_len=seq_len, seg=seg, has_init=has_init, want_state=want_state),
        grid=(n_seq,),
        in_specs=in_specs,
        out_specs=out_specs,
        out_shape=out_shape,
        scratch_shapes=[
            pltpu.VMEM((nhd * n_chunks, C, 128), F32),
            pltpu.VMEM((nhd * n_chunks, 2 * C, 128), BF16),
            pltpu.VMEM((nhd * n_chunks, C, C), BF16),
            pltpu.VMEM((nhd * n_chunks, C, 128), BF16),
            pltpu.VMEM((nhd * n_chunks, 8, 128), F32),
            pltpu.VMEM((nhd, DK_D, DV_D), F32),
            pltpu.VMEM((seq_len, 512), F32),
            pltpu.VMEM((seq_len, 512), F32),
        ],
        compiler_params=_cparams(("arbitrary",)),
        name="delta_lat" if has_init else "delta_ctx",
    )(*args)
    return res


CG = CHUNK_G
GB = 64
HP = H_G // 2


def _gla_kernel(*refs, seq_len, has_init, want_state):
    (q_ref, k_ref, v_ref, r_ref, sm_ref, w2_ref, b_ref, ng_ref, cum3_ref, sele_ref) = refs[:10]
    pos = 10
    s0_ref = None
    if has_init:
        s0_ref = refs[pos]
        pos += 1
    o_ref = refs[pos]
    pos += 1
    sfin_ref = None
    if want_state:
        sfin_ref = refs[pos]
        pos += 1
    bc_scr, tot_scr, oacc_scr = refs[pos:]

    n_blocks = seq_len // GB
    n_chunks = seq_len // CG
    rowi = lax.broadcasted_iota(jnp.int32, (CG, H_G * DK_G), 0)
    half = CG // 2
    rowh = lax.broadcasted_iota(jnp.int32, (half, H_G * DK_G), 0)
    zero_half = jnp.zeros((half, H_G * DK_G), F32)
    qscale = DK_G ** -0.5

    def pairs(x, w):
        return jnp.stack([x[:, p * w:(p + 1) * w] for p in range(HP)], axis=0)

    def block_body(b, carry):
        r0 = pl.multiple_of(b * GB, GB)
        sm16 = sm_ref[pl.ds(r0, GB), :].astype(BF16)
        q = q_ref[pl.ds(r0, GB), :].astype(F32) * qscale
        k = k_ref[pl.ds(r0, GB), :].astype(F32)
        v = v_ref[pl.ds(r0, GB), :].astype(F32)
        o_blk = [None] * (GB // CG)
        rrs = []
        for d in range(2):
            logits = _dot(sm16, w2_ref[d]) + b_ref[d]
            la = -_softplus(-logits) * (1.0 / GLA_TAU)
            cs = _dot(cum3_ref[...], _split3_rows(la))
            bc = cs[d * GB:(d + 1) * GB]
            bc_scr[d, pl.ds(r0, GB), :] = bc
            tot_scr[d, pl.ds(r0, GB), :] = cs[2 * GB:3 * GB]
            for c in range(GB // CG):
                sl = slice(c * CG, (c + 1) * CG)
                qc, kc, bcc, vc = q[sl], k[sl], bc[sl], v[sl]
                pieces = []
                for j in range(CG):
                    if d == 0:
                        rs = slice(half if j >= half else 0, CG)
                        keep = (rowh >= j - half) if j >= half else (rowi >= j)
                    else:
                        rs = slice(0, half if j < half else CG)
                        keep = (rowh <= j) if j < half else (rowi <= j)
                    e = jnp.exp(jnp.minimum(bcc[rs] - bcc[j:j + 1, :], 0.0))
                    a = jnp.where(keep, qc[rs] * e * kc[j:j + 1, :], 0.0)
                    if rs.stop - rs.start < CG:
                        a = jnp.concatenate([zero_half, a] if rs.start else [a, zero_half], axis=0)
                    pieces.append(a.astype(BF16))
                rrs.append(_dot(jnp.concatenate(pieces, axis=0), sele_ref[...]))
        for d in range(2):
            for c in range(GB // CG):
                rr = rrs[d * (GB // CG) + c]
                vc = v[c * CG:(c + 1) * CG]
                top = [j for j in range(CG) if d == 1 or j < half]
                bot = [j for j in range(CG) if d == 0 or j >= half]
                acc_t = functools.reduce(jnp.add, [rr[j * CG:j * CG + half] * vc[j:j + 1, :] for j in top])
                acc_b = functools.reduce(jnp.add, [rr[j * CG + half:(j + 1) * CG] * vc[j:j + 1, :] for j in bot])
                acc = jnp.concatenate([acc_t, acc_b], axis=0)
                o_blk[c] = acc if d == 0 else o_blk[c] + acc
        for c in range(GB // CG):
            oacc_scr[pl.ds(r0 + c * CG, CG), :] = o_blk[c]
        return carry

    lax.fori_loop(0, n_blocks, block_body, 0)

    if has_init:
        st0 = (s0_ref[0, 0:HP], s0_ref[0, HP:2 * HP])
    else:
        st0 = (jnp.zeros((HP, 2 * DV_G, 2 * DK_G), F32),) * 2
    prow = lax.broadcasted_iota(jnp.int32, (2 * DV_G, 2 * DK_G), 0) // DV_G
    pcol = lax.broadcasted_iota(jnp.int32, (2 * DV_G, 2 * DK_G), 1) // DK_G
    pmask = jnp.where(prow == pcol, 1.0, 0.0).astype(F32)

    def scan_body(i, carry):
        new = []
        for d in range(2):
            st = carry[d]
            n = i if d == 0 else n_chunks - 1 - i
            r0 = pl.multiple_of(n * CG, CG)
            bc = bc_scr[d, pl.ds(r0, CG), :]
            tot = tot_scr[d, pl.ds(r0, CG), :]
            q = q_ref[pl.ds(r0, CG), :].astype(F32) * qscale
            k = k_ref[pl.ds(r0, CG), :].astype(F32)
            qd = pairs(q * jnp.exp(bc), 2 * DK_G).astype(BF16)
            kd = pairs(k * jnp.exp(tot - bc), 2 * DK_G).astype(BF16)
            vp = pairs(v_ref[pl.ds(r0, CG), :], 2 * DV_G)
            o = lax.dot_general(qd, st.astype(BF16), (((2,), (2,)), ((0,), (0,))), preferred_element_type=F32)
            oacc_scr[pl.ds(r0, CG), :] += jnp.concatenate([o[p] for p in range(HP)], axis=1)
            upd = _bdot_tn(vp, kd)
            new.append(st * jnp.exp(pairs(tot, 2 * DK_G)[:, 0:1, :]) + upd * pmask)
        return tuple(new)

    st_fin = lax.fori_loop(0, n_chunks, scan_body, st0, unroll=4)

    if want_state:
        sfin_ref[0, 0:HP] = st_fin[0]
        sfin_ref[0, HP:2 * HP] = st_fin[1]

    def out_body(b, carry):
        r0 = pl.multiple_of(b * GB, GB)
        for h in range(H_G):
            o = oacc_scr[pl.ds(r0, GB), h * DV_G:(h + 1) * DV_G]
            o = o * lax.rsqrt(jnp.mean(o * o, axis=-1, keepdims=True) + RMS_EPS) * ng_ref[...]
            rg = r_ref[pl.ds(r0, GB), h * DV_G:(h + 1) * DV_G].astype(F32)
            o_ref[pl.ds(r0, GB), h * DV_G:(h + 1) * DV_G] = (o * _silu(rg)).astype(BF16)
        return carry

    lax.fori_loop(0, n_blocks, out_body, 0)


def _gla(p, sm, consts, w2p, gb, ng, s0, *, seq_len, n_seq, row_blk0, want_state):
    has_init = s0 is not None
    full = lambda shape: pl.BlockSpec(shape, lambda s: (0,) * len(shape))
    hk, hv = H_G * DK_G, H_G * DV_G
    in_specs = [
        pl.BlockSpec((seq_len, hk), lambda s: (s + row_blk0, COL_GQ // hk)),
        pl.BlockSpec((seq_len, hk), lambda s: (s + row_blk0, COL_GQ // hk + 1)),
        pl.BlockSpec((seq_len, hv), lambda s: (s + row_blk0, (COL_GQ + 2 * hk) // hv)),
        pl.BlockSpec((seq_len, hv), lambda s: (s + row_blk0, (COL_GQ + 2 * hk) // hv + 1)),
        pl.BlockSpec((seq_len, 128), lambda s: (s + row_blk0, 0)),
        full((2, 128, hk)), full((2, 1, hk)), full((1, DV_G)),
        full((3 * GB, 3 * GB)), full((CG * CG, hv)),
    ]
    args = [p, p, p, p, sm, w2p, gb, ng, consts["cum3_g"], consts["sele"]]
    if has_init:
        in_specs.append(pl.BlockSpec((1, 2 * HP, 2 * DV_G, 2 * DK_G), lambda s: (s, 0, 0, 0)))
        args.append(s0)
    out_specs = [pl.BlockSpec((seq_len, hv), lambda s: (s, 0))]
    out_shape = [jax.ShapeDtypeStruct((n_seq * seq_len, hv), BF16)]
    if want_state:
        out_specs.append(pl.BlockSpec((1, 2 * HP, 2 * DV_G, 2 * DK_G), lambda s: (s, 0, 0, 0)))
        out_shape.append(jax.ShapeDtypeStruct((n_seq, 2 * HP, 2 * DV_G, 2 * DK_G), F32))
    return pl.pallas_call(
        functools.partial(_gla_kernel, seq_len=seq_len, has_init=has_init, want_state=want_state),
        grid=(n_seq,),
        in_specs=in_specs,
        out_specs=out_specs,
        out_shape=out_shape,
        scratch_shapes=[
            pltpu.VMEM((2, seq_len, hk), F32),
            pltpu.VMEM((2, seq_len, hk), F32),
            pltpu.VMEM((seq_len, hv), F32),
        ],
        compiler_params=_cparams(("arbitrary",)),
        name="gla_lat" if has_init else "gla_ctx",
    )(*args)


def _constants():
    i = np.arange(C)
    lo = (i[:, None] >= i[None, :]).astype(np.float32)
    up = (i[:, None] <= i[None, :]).astype(np.float32)
    ones = np.ones((C, C), np.float32)
    cum_d = np.concatenate([lo, up, ones], axis=0)
    blk = (i[:, None] // CG == i[None, :] // CG).astype(np.float32)
    cum_g = np.concatenate([lo * blk, up * blk, blk], axis=0)
    hk, hv = H_G * DK_G, H_G * DV_G
    sele = (np.arange(hk)[:, None] // DK_G == np.arange(hv)[None, :] // DV_G).astype(np.float32)
    return {
        "cum3_d": jnp.asarray(np.tile(cum_d, (1, 3)), BF16),
        "cum3_g": jnp.asarray(np.tile(cum_g, (1, 3)), BF16),
        "tri3_d": jnp.asarray(np.stack([np.tile(lo, (1, 3))] * H_D + [np.tile(up, (1, 3))] * H_D), BF16),
        "sele": jnp.asarray(sele, BF16),
    }


def _permute_w_in(w_in_l):
    widths = (W_A, W_A, W_A, 512, 512, 512, 512, 8, 8, 256, 256, 512, 512, 32, 3 * D_MODEL)
    offs = np.concatenate([[0], np.cumsum(widths)])
    seg = lambda a, b: w_in_l[:, offs[a]:offs[b]]
    pad = jnp.zeros((D_MODEL, 128 - 48), w_in_l.dtype)
    return jnp.concatenate(
        [seg(0, 3), seg(3, 7), seg(9, 13), seg(14, 15), seg(7, 9), seg(13, 14), pad], axis=1).astype(BF16)


def _gla_pack_state(s):
    n = s.shape[0]
    st = jnp.swapaxes(s, -1, -2).reshape(n, 2, HP, 2, DV_G, DK_G)
    packed = jnp.einsum("ndpavk,ab->ndpavbk", st, jnp.eye(2, dtype=s.dtype))
    return packed.reshape(n, 2 * HP, 2 * DV_G, 2 * DK_G)


def _gla_unpack_state(sp):
    n = sp.shape[0]
    s6 = sp.reshape(n, 2, HP, 2, DV_G, 2, DK_G)
    diag = jnp.stack([s6[:, :, :, a, :, a, :] for a in range(2)], axis=3)
    return jnp.swapaxes(diag.reshape(n, 2, H_G, DV_G, DK_G), -1, -2)


def _lane_row(vals8, lane0):
    return jnp.zeros((1, 128), F32).at[0, lane0:lane0 + 8].set(vals8.reshape(8).astype(F32))


def kernel(x_prompt, x_sample, state_delta, state_gla, c, c_ctx, w_ada, b_ada, ln_g, ln_b, ffn_w1, ffn_w2, w_in,
           conv_a, conv_qkv, delta_a_log, delta_dt_bias, delta_norm_g, gla_w2, gla_b, gla_norm_g,
           w_br_a, w_br_d, w_br_g, w_o):
    n_ctx, ctx_len, _ = x_prompt.shape
    n_lat, lat_len, _ = x_sample.shape
    n_ctx_tok = n_ctx * ctx_len
    n_lat_tok = n_lat * lat_len
    assert ctx_len == MERGE_TM and lat_len % MERGE_TM == 0 and n_ctx_tok % lat_len == 0 and n_lat <= 8

    consts = _constants()
    x = jnp.concatenate([x_prompt.reshape(n_ctx_tok, D_MODEL), x_sample.reshape(n_lat_tok, D_MODEL)], axis=0)
    cond16 = jnp.zeros((16, D_MODEL), F32).at[:n_lat].set(c).at[8].set(c_ctx)
    ada = _ada_table(cond16, w_ada, b_ada)

    sds, sgs = [], []
    for l in range(DEPTH):
        ada_l = ada[l]
        kw = dict(n_ctx_tok=n_ctx_tok, lat_len=lat_len)
        x = _ffn(x, ada_l, ffn_w1[l, 0].astype(BF16), ffn_w2[l, 0].astype(BF16), ln_g[l, 0], ln_b[l, 0], 0, **kw)
        p, sm = _inproj(x, ada_l, _permute_w_in(w_in[l]), **kw)

        arow = _lane_row(delta_a_log[l], SM_A)
        dtb = _lane_row(delta_dt_bias[l], SM_A)
        ngd = delta_norm_g[l].reshape(1, DV_D)
        od_ctx, sd = _delta(p, sm, consts, conv_qkv[l], arow, dtb, ngd, None, seq_len=ctx_len, n_seq=n_ctx,
                            row_blk0=0, seg=ctx_len, want_state=True)
        s0d = state_delta[:, l].astype(F32).reshape(n_lat, 2 * H_D, DK_D, DV_D)
        (od_lat,) = _delta(p, sm, consts, conv_qkv[l], arow, dtb, ngd, s0d, seq_len=lat_len, n_seq=n_lat,
                           row_blk0=n_ctx_tok // lat_len, seg=GRID_W, want_state=False)
        od = jnp.concatenate([od_ctx, od_lat], axis=0)
        sds.append(sd.reshape(n_ctx, 2, H_D, DK_D, DV_D))

        w2p = jnp.zeros((2, 128, H_G * DK_G), F32)
        for d in range(2):
            w2p = w2p.at[d, SM_LR + d * GLA_RANK:SM_LR + (d + 1) * GLA_RANK].set(gla_w2[l, d])
        w2p = w2p.astype(BF16)
        gb = gla_b[l].reshape(2, 1, H_G * DK_G)
        ngg = gla_norm_g[l].reshape(1, DV_G)
        og_ctx, sg = _gla(p, sm, consts, w2p, gb, ngg, None, seq_len=ctx_len, n_seq=n_ctx, row_blk0=0,
                          want_state=True)
        s0g = _gla_pack_state(state_gla[:, l].astype(F32))
        (og_lat,) = _gla(p, sm, consts, w2p, gb, ngg, s0g, seq_len=lat_len, n_seq=n_lat,
                         row_blk0=n_ctx_tok // lat_len, want_state=False)
        og = jnp.concatenate([og_ctx, og_lat], axis=0)
        sgs.append(_gla_unpack_state(sg))

        x = _merge(p, od, og, x, ada_l, conv_a[l], w_br_a[l].astype(BF16), w_br_d[l].astype(BF16),
                   w_br_g[l].astype(BF16), w_o[l].astype(BF16), ln_g[l, 1], ln_b[l, 1], **kw)
        x = _ffn(x, ada_l, ffn_w1[l, 1].astype(BF16), ffn_w2[l, 1].astype(BF16), ln_g[l, 2], ln_b[l, 2], 2, **kw)

    y_prompt = x[:n_ctx_tok].reshape(n_ctx, ctx_len, D_MODEL)
    y_sample = x[n_ctx_tok:].reshape(n_lat, lat_len, D_MODEL)
    new_state_delta = jnp.stack(sds, axis=1).astype(x_prompt.dtype)
    new_state_gla = jnp.stack(sgs, axis=1).astype(x_prompt.dtype)
    return (y_prompt, y_sample, new_state_delta, new_state_gla)
```

```python
import functools

import jax
import jax.numpy as jnp
import numpy as np
from jax import lax
from jax.experimental import pallas as pl
from jax.experimental.pallas import tpu as pltpu

F32 = jnp.float32
BF16 = jnp.bfloat16

D_MODEL = 1024
DEPTH = 2
GRID_W = 64
D_FF = 2816
W_A = 512
H_D, DK_D, DV_D, CHUNK_D = 4, 128, 128, 64
H_G, DK_G, DV_G, CHUNK_G = 4, 64, 128, 16
GLA_RANK = 16
GLA_TAU = 16.0
N_ADA = 9
ALPHA = float((2 * DEPTH) ** 0.25)
LN_EPS = 1e-5
RMS_EPS = 1e-6

COL_A = 0
COL_DQ = 1536
COL_GQ = 3584
COL_MG = 5120
COL_SM = 8192
SM_W = 256
D_PROJ_PAD = COL_SM + SM_W
SM_BETA, SM_LR, SM_A = 0, 16, 128

VMEM_LIMIT = 56 * 1024 * 1024


def _cparams(sem):
    return pltpu.CompilerParams(dimension_semantics=sem, vmem_limit_bytes=VMEM_LIMIT)


def _dot(a, b):
    return jnp.dot(a, b, preferred_element_type=F32)


def _dot_nt(a, b):
    return lax.dot_general(a, b, (((1,), (1,)), ((), ())), preferred_element_type=F32)


def _dot_tn(a, b):
    return lax.dot_general(a, b, (((0,), (0,)), ((), ())), preferred_element_type=F32)


def _sigmoid(x):
    return 1.0 / (1.0 + jnp.exp(-x))


def _silu(x):
    return x * _sigmoid(x)


def _softplus(x):
    return jnp.maximum(x, 0.0) + jnp.log1p(jnp.exp(-jnp.abs(x)))


def _split2(x):
    hi = x.astype(BF16)
    lo = (x - hi.astype(F32)).astype(BF16)
    return hi, lo


def _split3_rows(x):
    hi = x.astype(BF16)
    r = x - hi.astype(F32)
    mid = r.astype(BF16)
    lo = (r - mid.astype(F32)).astype(BF16)
    return jnp.concatenate([hi, mid, lo], axis=0)


def _mm3(a, b):
    ah, al = _split2(a)
    bh, bl = _split2(b)
    n = a.shape[0]
    p = _dot(jnp.concatenate([ah, al], axis=0), bh)
    return p[:n] + p[n:] + _dot(ah, bl)


def _mm1(a, b):
    return _dot(a.astype(BF16), b.astype(BF16))


def _layer_norm(y, g, b):
    mu = jnp.mean(y, axis=-1, keepdims=True)
    yc = y - mu
    var = jnp.mean(yc * yc, axis=-1, keepdims=True)
    return yc * lax.rsqrt(var + LN_EPS) * g + b


def _ada_kernel(cond_ref, w_ref, b_ref, o_ref):
    s = _silu(cond_ref[...]).astype(BF16)
    o_ref[0] = _dot(s, w_ref[0].astype(BF16)) + b_ref[0]


def _ada_table(cond16, w_ada, b_ada):
    n_l = w_ada.shape[0]
    tn = 1024
    out = pl.pallas_call(
        _ada_kernel,
        grid=(n_l, N_ADA * D_MODEL // tn),
        in_specs=[
            pl.BlockSpec((16, D_MODEL), lambda l, j: (0, 0)),
            pl.BlockSpec((1, D_MODEL, tn), lambda l, j: (l, 0, j)),
            pl.BlockSpec((1, 1, tn), lambda l, j: (l, 0, j)),
        ],
        out_specs=pl.BlockSpec((1, 16, tn), lambda l, j: (l, 0, j)),
        out_shape=jax.ShapeDtypeStruct((n_l, 16, N_ADA * D_MODEL), F32),
        compiler_params=_cparams(("arbitrary", "arbitrary")),
        name="ada",
    )(cond16, w_ada, b_ada.reshape(n_l, 1, N_ADA * D_MODEL))
    return out.reshape(n_l, 16, N_ADA, D_MODEL)


def _cond_row(tok0, n_ctx_tok, lat_len):
    return jnp.where(tok0 < n_ctx_tok, 8, (tok0 - n_ctx_tok) // lat_len)


MXU_TILE = 256
FF_SPLIT = (D_FF // MXU_TILE // 2) * MXU_TILE
FF_CHUNKS = ((0, FF_SPLIT), (FF_SPLIT, D_FF))


def _ffn_kernel(xc_ref, xp_ref, adac_ref, adap_ref, w1_ref, w2_ref, lng_ref, lnb_ref, o_ref, y_scr, *, j):
    @pl.when(pl.program_id(0) == 0)
    def _():
        y_scr[...] = jnp.zeros_like(y_scr)

    h = (xc_ref[...] * (1.0 + adac_ref[3 * j + 1:3 * j + 2, :]) + adac_ref[3 * j:3 * j + 1, :]).astype(BF16)
    y_new = None
    for c0, c1 in FF_CHUNKS:
        g = _dot(h, w1_ref[:, c0:c1])
        u = _dot(h, w1_ref[:, D_FF + c0:D_FF + c1])
        part = _dot((_silu(g) * u).astype(BF16), w2_ref[c0:c1, :])
        y_new = part if y_new is None else y_new + part

    y = ALPHA * xp_ref[...] + 0.5 * adap_ref[3 * j + 2:3 * j + 3, :] * y_scr[...]
    o_ref[...] = _layer_norm(y, lng_ref[...], lnb_ref[...])
    y_scr[...] = y_new


def _ffn(x, ada_l, w1, w2, lng, lnb, j, n_ctx_tok, lat_len, tm=512):
    t = x.shape[0]
    n = t // tm
    cur = lambda i: jnp.minimum(i, n - 1)
    prev = lambda i: jnp.maximum(i - 1, 0)
    cond = lambda i: _cond_row(i * tm, n_ctx_tok, lat_len)
    resident = lambda shape: pl.BlockSpec(shape, lambda i: (0,) * len(shape), pipeline_mode=pl.Buffered(1))
    return pl.pallas_call(
        functools.partial(_ffn_kernel, j=j),
        grid=(n + 1,),
        in_specs=[
            pl.BlockSpec((tm, D_MODEL), lambda i: (cur(i), 0)),
            pl.BlockSpec((tm, D_MODEL), lambda i: (prev(i), 0)),
            pl.BlockSpec((None, N_ADA, D_MODEL), lambda i: (cond(cur(i)), 0, 0)),
            pl.BlockSpec((None, N_ADA, D_MODEL), lambda i: (cond(prev(i)), 0, 0)),
            resident((D_MODEL, 2 * D_FF)),
            resident((D_FF, D_MODEL)),
            resident((1, D_MODEL)),
            resident((1, D_MODEL)),
        ],
        out_specs=pl.BlockSpec((tm, D_MODEL), lambda i: (prev(i), 0)),
        out_shape=jax.ShapeDtypeStruct((t, D_MODEL), F32),
        scratch_shapes=[pltpu.VMEM((tm, D_MODEL), F32)],
        compiler_params=_cparams(("arbitrary",)),
        name="ffn",
    )(x, x, ada_l, ada_l, w1, w2, lng.reshape(1, D_MODEL), lnb.reshape(1, D_MODEL))


INPROJ_NC = 8 * MXU_TILE


def _inproj_kernel(x_ref, ada_ref, w_ref, p_ref, sm_ref):
    h = (x_ref[...] * (1.0 + ada_ref[4:5, :]) + ada_ref[3:4, :]).astype(BF16)
    for c0 in range(0, COL_SM, INPROJ_NC):
        p_ref[:, c0:c0 + INPROJ_NC] = _dot(h, w_ref[:, c0:c0 + INPROJ_NC]).astype(BF16)
    sm_ref[...] = _dot(h, w_ref[:, COL_SM:D_PROJ_PAD])


def _inproj(x, ada_l, w_in_p, n_ctx_tok, lat_len, tm=512):
    t = x.shape[0]
    cond = lambda i: _cond_row(i * tm, n_ctx_tok, lat_len)
    return pl.pallas_call(
        _inproj_kernel,
        grid=(t // tm,),
        in_specs=[
            pl.BlockSpec((tm, D_MODEL), lambda i: (i, 0)),
            pl.BlockSpec((None, N_ADA, D_MODEL), lambda i: (cond(i), 0, 0)),
            pl.BlockSpec((D_MODEL, D_PROJ_PAD), lambda i: (0, 0), pipeline_mode=pl.Buffered(1)),
        ],
        out_specs=[pl.BlockSpec((tm, COL_SM), lambda i: (i, 0)),
                   pl.BlockSpec((tm, D_PROJ_PAD - COL_SM), lambda i: (i, 0))],
        out_shape=[jax.ShapeDtypeStruct((t, COL_SM), BF16),
                   jax.ShapeDtypeStruct((t, D_PROJ_PAD - COL_SM), F32)],
        compiler_params=_cparams(("arbitrary",)),
        name="inproj",
    )(x, ada_l, w_in_p)


MERGE_TM = 256


def _merge_kernel(a_ref, m0_ref, m1_ref, m2_ref, od_ref, og_ref, x_ref, ada_ref, cw_ref,
                  wa_ref, wd_ref, wg_ref, wo_ref, lng_ref, lnb_ref, o_ref, *, n_ctx_tiles):
    i = pl.program_id(0)
    seg = jnp.where(i < n_ctx_tiles, MERGE_TM, GRID_W)
    row = lax.broadcasted_iota(jnp.int32, (MERGE_TM, W_A), 0)
    pos = jnp.bitwise_and(row, seg - 1)
    a_x = a_ref[:, 0:W_A].astype(F32)
    a_b = a_ref[:, W_A:2 * W_A].astype(F32)
    a_c = a_ref[:, 2 * W_A:3 * W_A].astype(F32)
    z = a_c * a_x
    z_prev = jnp.where(pos == 0, 0.0, pltpu.roll(z, 1, 0))
    z_next = jnp.where(pos == seg - 1, 0.0, pltpu.roll(z, MERGE_TM - 1, 0))
    y_a = a_b * (cw_ref[0:1, :] * z_prev + cw_ref[1:2, :] * z + cw_ref[2:3, :] * z_next)
    br_a = _dot(y_a.astype(BF16), wa_ref[...])
    br_d = _dot(od_ref[...], wd_ref[...])
    br_g = _dot(og_ref[...], wg_ref[...])
    gate = lambda m_ref: _sigmoid(m_ref[...].astype(F32))
    merged = gate(m0_ref) * br_a + gate(m1_ref) * br_d + gate(m2_ref) * br_g
    y = _dot(merged.astype(BF16), wo_ref[...])
    y = ALPHA * x_ref[...] + ada_ref[5:6, :] * y
    o_ref[...] = _layer_norm(y, lng_ref[...], lnb_ref[...])


def _merge(p, od, og, x, ada_l, conv_a, wa, wd, wg, wo, lng, lnb, n_ctx_tok, lat_len):
    t = x.shape[0]
    tm = MERGE_TM
    cond = lambda i: _cond_row(i * tm, n_ctx_tok, lat_len)
    full = lambda shape: pl.BlockSpec(shape, lambda i: (0,) * len(shape))
    mg0 = COL_MG // D_MODEL
    return pl.pallas_call(
        functools.partial(_merge_kernel, n_ctx_tiles=n_ctx_tok // tm),
        grid=(t // tm,),
        in_specs=[
            pl.BlockSpec((tm, 3 * W_A), lambda i: (i, 0)),
            pl.BlockSpec((tm, D_MODEL), lambda i: (i, mg0)),
            pl.BlockSpec((tm, D_MODEL), lambda i: (i, mg0 + 1)),
            pl.BlockSpec((tm, D_MODEL), lambda i: (i, mg0 + 2)),
            pl.BlockSpec((tm, 512), lambda i: (i, 0)),
            pl.BlockSpec((tm, 512), lambda i: (i, 0)),
            pl.BlockSpec((tm, D_MODEL), lambda i: (i, 0)),
            pl.BlockSpec((None, N_ADA, D_MODEL), lambda i: (cond(i), 0, 0)),
            full((3, W_A)),
            full((W_A, D_MODEL)), full((512, D_MODEL)), full((512, D_MODEL)), full((D_MODEL, D_MODEL)),
            full((1, D_MODEL)), full((1, D_MODEL)),
        ],
        out_specs=pl.BlockSpec((tm, D_MODEL), lambda i: (i, 0)),
        out_shape=jax.ShapeDtypeStruct((t, D_MODEL), F32),
        compiler_params=_cparams(("arbitrary",)),
        name="merge",
    )(p, p, p, p, od, og, x, ada_l, conv_a, wa, wd, wg, wo, lng.reshape(1, D_MODEL), lnb.reshape(1, D_MODEL))


C = CHUNK_D


def _bdot(a, b):
    return lax.dot_general(a, b, (((2,), (1,)), ((0,), (0,))), preferred_element_type=F32)


def _bdot_tn(a, b):
    return lax.dot_general(a, b, (((1,), (1,)), ((0,), (0,))), preferred_element_type=F32)


def _tri_inverse(m, eye, row, col):
    def same_block(shift):
        return jnp.right_shift(row, shift) == jnp.right_shift(col, shift)

    m8 = jnp.where(same_block(3), m, 0.0)
    m16 = m8.astype(BF16)
    x = eye - m8
    sq = _bdot(m16, m16).astype(BF16)
    x = x + _bdot(x.astype(BF16), sq)
    sq = _bdot(sq, sq).astype(BF16)
    x = x + _bdot(x.astype(BF16), sq)
    for shift in (4, 5, 6):
        e = jnp.where(same_block(shift) & jnp.logical_not(same_block(shift - 1)), m, 0.0)
        x16 = x.astype(BF16)
        x = x - _bdot(_bdot(x16, e.astype(BF16)).astype(BF16), x16)
    return x


NHD = 2 * H_D
CB = 4
HALO = 16


def _delta_kernel(*refs, seq_len, seg, has_init, want_state):
    (q_ref, k_ref, v_ref, z_ref, sm_ref, cw_ref, arow_ref, dtb_ref, ng_ref, cum3_ref, tri2_ref, e2_ref) = refs[:12]
    pos = 12
    s0_ref = None
    if has_init:
        s0_ref = refs[pos]
        pos += 1
    o_ref = refs[pos]
    pos += 1
    sfin_ref = None
    if want_state:
        sfin_ref = refs[pos]
        pos += 1
    u_scr, wq_scr, at_scr, kd_scr, ls_scr, s_scr, of_scr, ob_scr = refs[pos:]

    n_chunks = seq_len // C
    nb = CB * NHD
    row = lax.broadcasted_iota(jnp.int32, (nb, C, C), 1)
    col = lax.broadcasted_iota(jnp.int32, (nb, C, C), 2)
    fwd = jnp.bitwise_and(lax.broadcasted_iota(jnp.int32, (nb, C, C), 0), NHD - 1) < H_D
    dist = jnp.where(fwd, row - col, col - row)
    strict = dist > 0
    incl = dist >= 0
    eye = jnp.where(row == col, 1.0, 0.0).astype(F32)
    row128 = lax.broadcasted_iota(jnp.int32, (C, 128), 0)

    def conv_block(ref, c0, wc0, r0, n):
        x = ref[pl.ds(r0, C), c0:c0 + 128].astype(F32)
        xp = pltpu.roll(x, 1, 0)
        xn = pltpu.roll(x, C - 1, 0)
        if seg == C:
            xp = jnp.where(row128 == 0, 0.0, xp)
            xn = jnp.where(row128 == C - 1, 0.0, xn)
        else:
            prev = ref[pl.ds(pl.multiple_of(jnp.maximum(r0 - HALO, 0), HALO), HALO), c0:c0 + 128].astype(F32)
            nxt = ref[pl.ds(pl.multiple_of(jnp.minimum(r0 + C, seq_len - HALO), HALO), HALO), c0:c0 + 128].astype(F32)
            pm = jnp.where(n > 0, 1.0, 0.0)
            nm = jnp.where(n < n_chunks - 1, 1.0, 0.0)
            xp = jnp.where(row128 == 0, prev[HALO - 1:HALO, :] * pm, xp)
            xn = jnp.where(row128 == C - 1, nxt[0:1, :] * nm, xn)
        y = cw_ref[0:1, wc0:wc0 + 128] * xp + cw_ref[1:2, wc0:wc0 + 128] * x + cw_ref[2:3, wc0:wc0 + 128] * xn
        return _silu(y)

    def l2n(x):
        return x * lax.rsqrt(jnp.sum(x * x, axis=-1, keepdims=True) + RMS_EPS)

    def chunk_body(it, carry):
        qs, ks, vs, kks, qks, betas, gcols, gsums, gtots = [], [], [], [], [], [], [], [], []
        for cc in range(CB):
            n = it * CB + cc
            r0 = pl.multiple_of(n * C, C)
            beta_full = _sigmoid(sm_ref[pl.ds(r0, C), 0:128])
            g_full = -jnp.exp(arow_ref[...]) * _softplus(sm_ref[pl.ds(r0, C), 128:256] + dtb_ref[...])
            cs = _dot(cum3_ref[...], _split3_rows(g_full))
            cols = jnp.concatenate([g_full, beta_full, cs], axis=0)
            chi, clo = _split2(cols)
            bcast = _dot(jnp.concatenate([chi, clo], axis=1), e2_ref[...])
            qh, kh, vh, kkh, qkh = [], [], [], [], []
            for h in range(H_D):
                q = l2n(conv_block(q_ref, h * 128, h * 128, r0, n)) * (DK_D ** -0.5)
                k = l2n(conv_block(k_ref, h * 128, 512 + h * 128, r0, n))
                v = conv_block(v_ref, h * 128, 1024 + h * 128, r0, n)
                k16 = k.astype(BF16)
                qh.append(q)
                kh.append(k)
                vh.append(v)
                kkh.append(_dot_nt(k16, k16))
                qkh.append(_dot_nt(q.astype(BF16), k16))
            for dst, src in ((qs, qh), (ks, kh), (vs, vh), (kks, kkh), (qks, qkh)):
                dst.extend(src + src)
            for b in range(NHD):
                lanes = slice(b * 128, (b + 1) * 128)
                d = b // H_D
                gcols.append(bcast[0:C, lanes])
                betas.append(bcast[C:2 * C, lanes])
                gsums.append(bcast[(2 + d) * C:(3 + d) * C, lanes])
                gtots.append(bcast[4 * C:5 * C, lanes])
        st = lambda xs: jnp.stack(xs, axis=0)
        q, k, v, kk, qk = st(qs), st(ks), st(vs), st(kks), st(qks)
        beta, gcol, gsum, gtot = st(betas), st(gcols), st(gsums), st(gtots)
        xg = jnp.where(strict, gcol[:, :, 0:C], 0.0)
        xh, xl = _split2(xg)
        tri2 = jnp.concatenate([tri2_ref[...]] * CB, axis=0)
        diff = _bdot(tri2, jnp.concatenate([xh, xl], axis=1))
        gamma = jnp.where(incl, jnp.exp(jnp.minimum(diff, 0.0)), 0.0)
        m = jnp.where(strict, beta[:, :, 0:C] * kk * gamma, 0.0)
        eg = jnp.exp(gsum)
        rhs = jnp.concatenate([v * beta, k * (beta * eg)], axis=2)
        sol = _bdot(_tri_inverse(m, eye, row, col).astype(BF16), rhs.astype(BF16))
        wq = jnp.concatenate([sol[:, :, 128:], q * eg], axis=1).astype(BF16)
        at = (qk * gamma).astype(BF16)
        kd = (k * jnp.exp(gtot - gsum)).astype(BF16)
        ls = jnp.exp(gtot[:, 0:8, :])
        for cc in range(CB):
            n = it * CB + cc
            for d in range(2):
                t = n if d == 0 else n_chunks - 1 - n
                dst = pl.ds(t * NHD + d * H_D, H_D)
                src = slice(cc * NHD + d * H_D, cc * NHD + (d + 1) * H_D)
                u_scr[dst] = sol[src, :, :128]
                wq_scr[dst] = wq[src]
                at_scr[dst] = at[src]
                kd_scr[dst] = kd[src]
                ls_scr[dst] = ls[src]
        return carry

    lax.fori_loop(0, n_chunks // CB, chunk_body, 0)

    if has_init:
        s_scr[...] = s0_ref[0]
    else:
        s_scr[...] = jnp.zeros_like(s_scr)

    def scan_body(i, carry):
        slot = pl.ds(i * NHD, NHD)
        s = s_scr[...]
        r = _bdot(wq_scr[slot], s.astype(BF16))
        v16 = (u_scr[slot] - r[:, 0:C]).astype(BF16)
        o = r[:, C:2 * C] + _bdot(at_scr[slot], v16)
        s_scr[...] = s * ls_scr[slot][:, 0:1, :] + _bdot_tn(kd_scr[slot], v16)
        for d in range(2):
            n = i if d == 0 else n_chunks - 1 - i
            r0 = pl.multiple_of(n * C, C)
            o_dst = of_scr if d == 0 else ob_scr
            for h in range(H_D):
                o_dst[pl.ds(r0, C), h * 128:(h + 1) * 128] = o[d * H_D + h]
        return carry

    lax.fori_loop(0, n_chunks, scan_body, 0)

    if want_state:
        sfin_ref[0] = s_scr[...]

    def out_body(n, carry):
        r0 = pl.multiple_of(n * C, C)
        for h in range(H_D):
            o = of_scr[pl.ds(r0, C), h * 128:(h + 1) * 128] + ob_scr[pl.ds(r0, C), h * 128:(h + 1) * 128]
            o = o * lax.rsqrt(jnp.mean(o * o, axis=-1, keepdims=True) + RMS_EPS) * ng_ref[...]
            zg = z_ref[pl.ds(r0, C), h * 128:(h + 1) * 128].astype(F32)
            o_ref[pl.ds(r0, C), h * 128:(h + 1) * 128] = (o * _silu(zg)).astype(BF16)
        return carry

    lax.fori_loop(0, n_chunks, out_body, 0)


def _delta(p, sm, consts, conv_qkv, arow, dtb, ng, s0, *, seq_len, n_seq, row_blk0, seg, want_state):
    has_init = s0 is not None
    n_chunks = seq_len // C
    nhd = 2 * H_D
    cq = COL_DQ // 512
    full = lambda shape: pl.BlockSpec(shape, lambda s: (0,) * len(shape))
    in_specs = [
        pl.BlockSpec((seq_len, 512), lambda s: (s + row_blk0, cq)),
        pl.BlockSpec((seq_len, 512), lambda s: (s + row_blk0, cq + 1)),
        pl.BlockSpec((seq_len, 512), lambda s: (s + row_blk0, cq + 2)),
        pl.BlockSpec((seq_len, 512), lambda s: (s + row_blk0, cq + 3)),
        pl.BlockSpec((seq_len, SM_W), lambda s: (s + row_blk0, 0)),
        full((3, 1536)), full((1, 128)), full((1, 128)), full((1, 128)),
        full((3 * C, 3 * C)), full((NHD, C, 2 * C)), full((2 * 128, NHD * 128)),
    ]
    args = [p, p, p, p, sm, conv_qkv, arow, dtb, ng, consts["cum3_d"], consts["tri2_d"], consts["e2_d"]]
    if has_init:
        in_specs.append(pl.BlockSpec((1, nhd, DK_D, DV_D), lambda s: (s, 0, 0, 0)))
        args.append(s0)
    out_specs = [pl.BlockSpec((seq_len, 512), lambda s: (s, 0))]
    out_shape = [jax.ShapeDtypeStruct((n_seq * seq_len, 512), BF16)]
    if want_state:
        out_specs.append(pl.BlockSpec((1, nhd, DK_D, DV_D), lambda s: (s, 0, 0, 0)))
        out_shape.append(jax.ShapeDtypeStruct((n_seq, nhd, DK_D, DV_D), F32))
    res = pl.pallas_call(
        functools.partial(_delta_kernel, seq_len=seq_len, seg=seg, has_init=has_init, want_state=want_state),
        grid=(n_seq,),
        in_specs=in_specs,
        out_specs=out_specs,
        out_shape=out_shape,
        scratch_shapes=[
            pltpu.VMEM((nhd * n_chunks, C, 128), F32),
            pltpu.VMEM((nhd * n_chunks, 2 * C, 128), BF16),
            pltpu.VMEM((nhd * n_chunks, C, C), BF16),
            pltpu.VMEM((nhd * n_chunks, C, 128), BF16),
            pltpu.VMEM((nhd * n_chunks, 8, 128), F32),
            pltpu.VMEM((nhd, DK_D, DV_D), F32),
            pltpu.VMEM((seq_len, 512), F32),
            pltpu.VMEM((seq_len, 512), F32),
        ],
        compiler_params=_cparams(("arbitrary",)),
        name="delta_lat" if has_init else "delta_ctx",
    )(*args)
    return res


CG = CHUNK_G
GB = 64
HP = H_G // 2


def _gla_kernel(*refs, seq_len, has_init, want_state):
    (q_ref, k_ref, v_ref, r_ref, sm_ref, w2_ref, b_ref, ng_ref, cum3_ref, sele_ref) = refs[:10]
    pos = 10
    s0_ref = None
    if has_init:
        s0_ref = refs[pos]
        pos += 1
    o_ref = refs[pos]
    pos += 1
    sfin_ref = None
    if want_state:
        sfin_ref = refs[pos]
        pos += 1
    bc_scr, tot_scr, oacc_scr = refs[pos:]

    n_blocks = seq_len // GB
    n_chunks = seq_len // CG
    rowi = lax.broadcasted_iota(jnp.int32, (CG, H_G * DK_G), 0)
    half = CG // 2
    rowh = lax.broadcasted_iota(jnp.int32, (half, H_G * DK_G), 0)
    zero_half = jnp.zeros((half, H_G * DK_G), F32)
    qscale = DK_G ** -0.5

    def pairs(x, w):
        return jnp.stack([x[:, p * w:(p + 1) * w] for p in range(HP)], axis=0)

    def block_body(b, carry):
        r0 = pl.multiple_of(b * GB, GB)
        sm16 = sm_ref[pl.ds(r0, GB), :].astype(BF16)
        q = q_ref[pl.ds(r0, GB), :].astype(F32) * qscale
        k = k_ref[pl.ds(r0, GB), :].astype(F32)
        v = v_ref[pl.ds(r0, GB), :].astype(F32)
        o_blk = [None] * (GB // CG)
        rrs = []
        for d in range(2):
            logits = _dot(sm16, w2_ref[d]) + b_ref[d]
            la = -_softplus(-logits) * (1.0 / GLA_TAU)
            cs = _dot(cum3_ref[...], _split3_rows(la))
            bc = cs[d * GB:(d + 1) * GB]
            bc_scr[d, pl.ds(r0, GB), :] = bc
            tot_scr[d, pl.ds(r0, GB), :] = cs[2 * GB:3 * GB]
            for c in range(GB // CG):
                sl = slice(c * CG, (c + 1) * CG)
                qc, kc, bcc, vc = q[sl], k[sl], bc[sl], v[sl]
                pieces = []
                for j in range(CG):
                    if d == 0:
                        rs = slice(half if j >= half else 0, CG)
                        keep = (rowh >= j - half) if j >= half else (rowi >= j)
                    else:
                        rs = slice(0, half if j < half else CG)
                        keep = (rowh <= j) if j < half else (rowi <= j)
                    e = jnp.exp(jnp.minimum(bcc[rs] - bcc[j:j + 1, :], 0.0))
                    a = jnp.where(keep, qc[rs] * e * kc[j:j + 1, :], 0.0)
                    if rs.stop - rs.start < CG:
                        a = jnp.concatenate([zero_half, a] if rs.start else [a, zero_half], axis=0)
                    pieces.append(a.astype(BF16))
                rrs.append(_dot(jnp.concatenate(pieces, axis=0), sele_ref[...]))
        for d in range(2):
            for c in range(GB // CG):
                rr = rrs[d * (GB // CG) + c]
                vc = v[c * CG:(c + 1) * CG]
                top = [j for j in range(CG) if d == 1 or j < half]
                bot = [j for j in range(CG) if d == 0 or j >= half]
                acc_t = functools.reduce(jnp.add, [rr[j * CG:j * CG + half] * vc[j:j + 1, :] for j in top])
                acc_b = functools.reduce(jnp.add, [rr[j * CG + half:(j + 1) * CG] * vc[j:j + 1, :] for j in bot])
                acc = jnp.concatenate([acc_t, acc_b], axis=0)
                o_blk[c] = acc if d == 0 else o_blk[c] + acc
        for c in range(GB // CG):
            oacc_scr[pl.ds(r0 + c * CG, CG), :] = o_blk[c]
        return carry

    lax.fori_loop(0, n_blocks, block_body, 0)

    if has_init:
        st0 = (s0_ref[0, 0:HP], s0_ref[0, HP:2 * HP])
    else:
        st0 = (jnp.zeros((HP, 2 * DV_G, 2 * DK_G), F32),) * 2
    prow = lax.broadcasted_iota(jnp.int32, (2 * DV_G, 2 * DK_G), 0) // DV_G
    pcol = lax.broadcasted_iota(jnp.int32, (2 * DV_G, 2 * DK_G), 1) // DK_G
    pmask = jnp.where(prow == pcol, 1.0, 0.0).astype(F32)

    def scan_body(i, carry):
        new = []
        for d in range(2):
            st = carry[d]
            n = i if d == 0 else n_chunks - 1 - i
            r0 = pl.multiple_of(n * CG, CG)
            bc = bc_scr[d, pl.ds(r0, CG), :]
            tot = tot_scr[d, pl.ds(r0, CG), :]
            q = q_ref[pl.ds(r0, CG), :].astype(F32) * qscale
            k = k_ref[pl.ds(r0, CG), :].astype(F32)
            qd = pairs(q * jnp.exp(bc), 2 * DK_G).astype(BF16)
            kd = pairs(k * jnp.exp(tot - bc), 2 * DK_G).astype(BF16)
            vp = pairs(v_ref[pl.ds(r0, CG), :], 2 * DV_G)
            o = lax.dot_general(qd, st.astype(BF16), (((2,), (2,)), ((0,), (0,))), preferred_element_type=F32)
            oacc_scr[pl.ds(r0, CG), :] += jnp.concatenate([o[p] for p in range(HP)], axis=1)
            upd = _bdot_tn(vp, kd)
            new.append(st * jnp.exp(pairs(tot, 2 * DK_G)[:, 0:1, :]) + upd * pmask)
        return tuple(new)

    st_fin = lax.fori_loop(0, n_chunks, scan_body, st0, unroll=4)

    if want_state:
        sfin_ref[0, 0:HP] = st_fin[0]
        sfin_ref[0, HP:2 * HP] = st_fin[1]

    def out_body(b, carry):
        r0 = pl.multiple_of(b * GB, GB)
        for h in range(H_G):
            o = oacc_scr[pl.ds(r0, GB), h * DV_G:(h + 1) * DV_G]
            o = o * lax.rsqrt(jnp.mean(o * o, axis=-1, keepdims=True) + RMS_EPS) * ng_ref[...]
            rg = r_ref[pl.ds(r0, GB), h * DV_G:(h + 1) * DV_G].astype(F32)
            o_ref[pl.ds(r0, GB), h * DV_G:(h + 1) * DV_G] = (o * _silu(rg)).astype(BF16)
        return carry

    lax.fori_loop(0, n_blocks, out_body, 0)


def _gla(p, sm, consts, w2p, gb, ng, s0, *, seq_len, n_seq, row_blk0, want_state):
    has_init = s0 is not None
    full = lambda shape: pl.BlockSpec(shape, lambda s: (0,) * len(shape))
    hk, hv = H_G * DK_G, H_G * DV_G
    in_specs = [
        pl.BlockSpec((seq_len, hk), lambda s: (s + row_blk0, COL_GQ // hk)),
        pl.BlockSpec((seq_len, hk), lambda s: (s + row_blk0, COL_GQ // hk + 1)),
        pl.BlockSpec((seq_len, hv), lambda s: (s + row_blk0, (COL_GQ + 2 * hk) // hv)),
        pl.BlockSpec((seq_len, hv), lambda s: (s + row_blk0, (COL_GQ + 2 * hk) // hv + 1)),
        pl.BlockSpec((seq_len, 128), lambda s: (s + row_blk0, 0)),
        full((2, 128, hk)), full((2, 1, hk)), full((1, DV_G)),
        full((3 * GB, 3 * GB)), full((CG * CG, hv)),
    ]
    args = [p, p, p, p, sm, w2p, gb, ng, consts["cum3_g"], consts["sele"]]
    if has_init:
        in_specs.append(pl.BlockSpec((1, 2 * HP, 2 * DV_G, 2 * DK_G), lambda s: (s, 0, 0, 0)))
        args.append(s0)
    out_specs = [pl.BlockSpec((seq_len, hv), lambda s: (s, 0))]
    out_shape = [jax.ShapeDtypeStruct((n_seq * seq_len, hv), BF16)]
    if want_state:
        out_specs.append(pl.BlockSpec((1, 2 * HP, 2 * DV_G, 2 * DK_G), lambda s: (s, 0, 0, 0)))
        out_shape.append(jax.ShapeDtypeStruct((n_seq, 2 * HP, 2 * DV_G, 2 * DK_G), F32))
    return pl.pallas_call(
        functools.partial(_gla_kernel, seq_len=seq_len, has_init=has_init, want_state=want_state),
        grid=(n_seq,),
        in_specs=in_specs,
        out_specs=out_specs,
        out_shape=out_shape,
        scratch_shapes=[
            pltpu.VMEM((2, seq_len, hk), F32),
            pltpu.VMEM((2, seq_len, hk), F32),
            pltpu.VMEM((seq_len, hv), F32),
        ],
        compiler_params=_cparams(("arbitrary",)),
        name="gla_lat" if has_init else "gla_ctx",
    )(*args)


def _constants():
    i = np.arange(C)
    lo = (i[:, None] >= i[None, :]).astype(np.float32)
    up = (i[:, None] <= i[None, :]).astype(np.float32)
    ones = np.ones((C, C), np.float32)
    cum_d = np.concatenate([lo, up, ones], axis=0)
    blk = (i[:, None] // CG == i[None, :] // CG).astype(np.float32)
    cum_g = np.concatenate([lo * blk, up * blk, blk], axis=0)
    hk, hv = H_G * DK_G, H_G * DV_G
    sele = (np.arange(hk)[:, None] // DK_G == np.arange(hv)[None, :] // DV_G).astype(np.float32)
    lane_sel = (np.arange(128)[:, None] == np.arange(NHD * 128)[None, :] // 128).astype(np.float32)
    return {
        "e2_d": jnp.asarray(np.tile(lane_sel, (2, 1)), BF16),
        "cum3_d": jnp.asarray(np.tile(cum_d, (1, 3)), BF16),
        "cum3_g": jnp.asarray(np.tile(cum_g, (1, 3)), BF16),
        "tri2_d": jnp.asarray(np.stack([np.tile(lo, (1, 2))] * H_D + [np.tile(up, (1, 2))] * H_D), BF16),
        "sele": jnp.asarray(sele, BF16),
    }


def _permute_w_in(w_in_l):
    widths = (W_A, W_A, W_A, 512, 512, 512, 512, 8, 8, 256, 256, 512, 512, 32, 3 * D_MODEL)
    offs = np.concatenate([[0], np.cumsum(widths)])
    seg = lambda a, b: w_in_l[:, offs[a]:offs[b]]
    zeros = lambda n: jnp.zeros((D_MODEL, n), w_in_l.dtype)
    narrow = [seg(7, 8), zeros(SM_LR - 8), seg(13, 14), zeros(SM_A - SM_LR - 2 * GLA_RANK),
              seg(8, 9), zeros(SM_W - SM_A - 8)]
    return jnp.concatenate([seg(0, 3), seg(3, 7), seg(9, 13), seg(14, 15)] + narrow, axis=1).astype(BF16)


def _gla_pack_state(s):
    n = s.shape[0]
    st = jnp.swapaxes(s, -1, -2).reshape(n, 2, HP, 2, DV_G, DK_G)
    packed = jnp.einsum("ndpavk,ab->ndpavbk", st, jnp.eye(2, dtype=s.dtype))
    return packed.reshape(n, 2 * HP, 2 * DV_G, 2 * DK_G)


def _gla_unpack_state(sp):
    n = sp.shape[0]
    s6 = sp.reshape(n, 2, HP, 2, DV_G, 2, DK_G)
    diag = jnp.stack([s6[:, :, :, a, :, a, :] for a in range(2)], axis=3)
    return jnp.swapaxes(diag.reshape(n, 2, H_G, DV_G, DK_G), -1, -2)


def _lane_row(vals8, lane0):
    return jnp.zeros((1, 128), F32).at[0, lane0:lane0 + 8].set(vals8.reshape(8).astype(F32))


def kernel(x_prompt, x_sample, state_delta, state_gla, c, c_ctx, w_ada, b_ada, ln_g, ln_b, ffn_w1, ffn_w2, w_in,
           conv_a, conv_qkv, delta_a_log, delta_dt_bias, delta_norm_g, gla_w2, gla_b, gla_norm_g,
           w_br_a, w_br_d, w_br_g, w_o):
    n_ctx, ctx_len, _ = x_prompt.shape
    n_lat, lat_len, _ = x_sample.shape
    n_ctx_tok = n_ctx * ctx_len
    n_lat_tok = n_lat * lat_len
    assert ctx_len == MERGE_TM and lat_len % MERGE_TM == 0 and n_ctx_tok % lat_len == 0 and n_lat <= 8

    consts = _constants()
    x = jnp.concatenate([x_prompt.reshape(n_ctx_tok, D_MODEL), x_sample.reshape(n_lat_tok, D_MODEL)], axis=0)
    cond16 = jnp.zeros((16, D_MODEL), F32).at[:n_lat].set(c).at[8].set(c_ctx)
    ada = _ada_table(cond16, w_ada, b_ada)

    sds, sgs = [], []
    for l in range(DEPTH):
        ada_l = ada[l]
        kw = dict(n_ctx_tok=n_ctx_tok, lat_len=lat_len)
        x = _ffn(x, ada_l, ffn_w1[l, 0].astype(BF16), ffn_w2[l, 0].astype(BF16), ln_g[l, 0], ln_b[l, 0], 0, **kw)
        p, sm = _inproj(x, ada_l, _permute_w_in(w_in[l]), **kw)

        arow = _lane_row(delta_a_log[l], SM_A % 128)
        dtb = _lane_row(delta_dt_bias[l], SM_A % 128)
        ngd = delta_norm_g[l].reshape(1, DV_D)
        od_ctx, sd = _delta(p, sm, consts, conv_qkv[l], arow, dtb, ngd, None, seq_len=ctx_len, n_seq=n_ctx,
                            row_blk0=0, seg=ctx_len, want_state=True)
        s0d = state_delta[:, l].astype(F32).reshape(n_lat, 2 * H_D, DK_D, DV_D)
        (od_lat,) = _delta(p, sm, consts, conv_qkv[l], arow, dtb, ngd, s0d, seq_len=lat_len, n_seq=n_lat,
                           row_blk0=n_ctx_tok // lat_len, seg=GRID_W, want_state=False)
        od = jnp.concatenate([od_ctx, od_lat], axis=0)
        sds.append(sd.reshape(n_ctx, 2, H_D, DK_D, DV_D))

        w2p = jnp.zeros((2, 128, H_G * DK_G), F32)
        for d in range(2):
            w2p = w2p.at[d, SM_LR + d * GLA_RANK:SM_LR + (d + 1) * GLA_RANK].set(gla_w2[l, d])
        w2p = w2p.astype(BF16)
        gb = gla_b[l].reshape(2, 1, H_G * DK_G)
        ngg = gla_norm_g[l].reshape(1, DV_G)
        og_ctx, sg = _gla(p, sm, consts, w2p, gb, ngg, None, seq_len=ctx_len, n_seq=n_ctx, row_blk0=0,
                          want_state=True)
        s0g = _gla_pack_state(state_gla[:, l].astype(F32))
        (og_lat,) = _gla(p, sm, consts, w2p, gb, ngg, s0g, seq_len=lat_len, n_seq=n_lat,
                         row_blk0=n_ctx_tok // lat_len, want_state=False)
        og = jnp.concatenate([og_ctx, og_lat], axis=0)
        sgs.append(_gla_unpack_state(sg))

        x = _merge(p, od, og, x, ada_l, conv_a[l], w_br_a[l].astype(BF16), w_br_d[l].astype(BF16),
                   w_br_g[l].astype(BF16), w_o[l].astype(BF16), ln_g[l, 1], ln_b[l, 1], **kw)
        x = _ffn(x, ada_l, ffn_w1[l, 1].astype(BF16), ffn_w2[l, 1].astype(BF16), ln_g[l, 2], ln_b[l, 2], 2, **kw)

    y_prompt = x[:n_ctx_tok].reshape(n_ctx, ctx_len, D_MODEL)
    y_sample = x[n_ctx_tok:].reshape(n_lat, lat_len, D_MODEL)
    new_state_delta = jnp.stack(sds, axis=1).astype(x_prompt.dtype)
    new_state_gla = jnp.stack(sgs, axis=1).astype(x_prompt.dtype)
    return (y_prompt, y_sample, new_state_delta, new_state_gla)
```

```python
import functools

import jax
import jax.numpy as jnp
import numpy as np
from jax import lax
from jax.experimental import pallas as pl
from jax.experimental.pallas import tpu as pltpu

F32 = jnp.float32
BF16 = jnp.bfloat16

D_MODEL = 1024
DEPTH = 2
GRID_W = 64
D_FF = 2816
W_A = 512
H_D, DK_D, DV_D, CHUNK_D = 4, 128, 128, 64
H_G, DK_G, DV_G, CHUNK_G = 4, 64, 128, 16
GLA_RANK = 16
GLA_TAU = 16.0
N_ADA = 9
ALPHA = float((2 * DEPTH) ** 0.25)
LN_EPS = 1e-5
RMS_EPS = 1e-6

COL_A = 0
COL_DQ = 1536
COL_GQ = 3584
COL_MG = 5120
COL_SM = 8192
SM_W = 256
D_PROJ_PAD = COL_SM + SM_W
SM_BETA, SM_LR, SM_A = 0, 16, 128

VMEM_LIMIT = 56 * 1024 * 1024


def _cparams(sem):
    return pltpu.CompilerParams(dimension_semantics=sem, vmem_limit_bytes=VMEM_LIMIT)


def _dot(a, b):
    return jnp.dot(a, b, preferred_element_type=F32)


def _dot_nt(a, b):
    return lax.dot_general(a, b, (((1,), (1,)), ((), ())), preferred_element_type=F32)


def _dot_tn(a, b):
    return lax.dot_general(a, b, (((0,), (0,)), ((), ())), preferred_element_type=F32)


def _sigmoid(x):
    return 1.0 / (1.0 + jnp.exp(-x))


def _silu(x):
    return x * _sigmoid(x)


def _softplus(x):
    return jnp.maximum(x, 0.0) + jnp.log1p(jnp.exp(-jnp.abs(x)))


def _split2(x):
    hi = x.astype(BF16)
    lo = (x - hi.astype(F32)).astype(BF16)
    return hi, lo


def _split3_rows(x):
    hi = x.astype(BF16)
    r = x - hi.astype(F32)
    mid = r.astype(BF16)
    lo = (r - mid.astype(F32)).astype(BF16)
    return jnp.concatenate([hi, mid, lo], axis=0)


def _mm3(a, b):
    ah, al = _split2(a)
    bh, bl = _split2(b)
    n = a.shape[0]
    p = _dot(jnp.concatenate([ah, al], axis=0), bh)
    return p[:n] + p[n:] + _dot(ah, bl)


def _mm1(a, b):
    return _dot(a.astype(BF16), b.astype(BF16))


def _layer_norm(y, g, b):
    mu = jnp.mean(y, axis=-1, keepdims=True)
    yc = y - mu
    var = jnp.mean(yc * yc, axis=-1, keepdims=True)
    return yc * lax.rsqrt(var + LN_EPS) * g + b


def _ada_kernel(cond_ref, w_ref, b_ref, o_ref):
    s = _silu(cond_ref[...]).astype(BF16)
    o_ref[0] = _dot(s, w_ref[0].astype(BF16)) + b_ref[0]


def _ada_table(cond16, w_ada, b_ada):
    n_l = w_ada.shape[0]
    tn = 1024
    out = pl.pallas_call(
        _ada_kernel,
        grid=(n_l, N_ADA * D_MODEL // tn),
        in_specs=[
            pl.BlockSpec((16, D_MODEL), lambda l, j: (0, 0)),
            pl.BlockSpec((1, D_MODEL, tn), lambda l, j: (l, 0, j)),
            pl.BlockSpec((1, 1, tn), lambda l, j: (l, 0, j)),
        ],
        out_specs=pl.BlockSpec((1, 16, tn), lambda l, j: (l, 0, j)),
        out_shape=jax.ShapeDtypeStruct((n_l, 16, N_ADA * D_MODEL), F32),
        compiler_params=_cparams(("arbitrary", "arbitrary")),
        name="ada",
    )(cond16, w_ada, b_ada.reshape(n_l, 1, N_ADA * D_MODEL))
    return out.reshape(n_l, 16, N_ADA, D_MODEL)


def _cond_row(tok0, n_ctx_tok, lat_len):
    return jnp.where(tok0 < n_ctx_tok, 8, (tok0 - n_ctx_tok) // lat_len)


MXU_TILE = 256
FF_SPLIT = (D_FF // MXU_TILE // 2) * MXU_TILE
FF_CHUNKS = ((0, FF_SPLIT), (FF_SPLIT, D_FF))


def _ffn_kernel(xc_ref, xp_ref, adac_ref, adap_ref, w1_ref, w2_ref, lng_ref, lnb_ref, o_ref, y_scr, *, j):
    @pl.when(pl.program_id(0) == 0)
    def _():
        y_scr[...] = jnp.zeros_like(y_scr)

    h = (xc_ref[...] * (1.0 + adac_ref[3 * j + 1:3 * j + 2, :]) + adac_ref[3 * j:3 * j + 1, :]).astype(BF16)
    y_new = None
    for c0, c1 in FF_CHUNKS:
        g = _dot(h, w1_ref[:, c0:c1])
        u = _dot(h, w1_ref[:, D_FF + c0:D_FF + c1])
        part = _dot((_silu(g) * u).astype(BF16), w2_ref[c0:c1, :])
        y_new = part if y_new is None else y_new + part

    y = ALPHA * xp_ref[...] + 0.5 * adap_ref[3 * j + 2:3 * j + 3, :] * y_scr[...]
    o_ref[...] = _layer_norm(y, lng_ref[...], lnb_ref[...])
    y_scr[...] = y_new


def _ffn(x, ada_l, w1, w2, lng, lnb, j, n_ctx_tok, lat_len, tm=512):
    t = x.shape[0]
    n = t // tm
    cur = lambda i: jnp.minimum(i, n - 1)
    prev = lambda i: jnp.maximum(i - 1, 0)
    cond = lambda i: _cond_row(i * tm, n_ctx_tok, lat_len)
    resident = lambda shape: pl.BlockSpec(shape, lambda i: (0,) * len(shape), pipeline_mode=pl.Buffered(1))
    return pl.pallas_call(
        functools.partial(_ffn_kernel, j=j),
        grid=(n + 1,),
        in_specs=[
            pl.BlockSpec((tm, D_MODEL), lambda i: (cur(i), 0)),
            pl.BlockSpec((tm, D_MODEL), lambda i: (prev(i), 0)),
            pl.BlockSpec((None, N_ADA, D_MODEL), lambda i: (cond(cur(i)), 0, 0)),
            pl.BlockSpec((None, N_ADA, D_MODEL), lambda i: (cond(prev(i)), 0, 0)),
            resident((D_MODEL, 2 * D_FF)),
            resident((D_FF, D_MODEL)),
            resident((1, D_MODEL)),
            resident((1, D_MODEL)),
        ],
        out_specs=pl.BlockSpec((tm, D_MODEL), lambda i: (prev(i), 0)),
        out_shape=jax.ShapeDtypeStruct((t, D_MODEL), F32),
        scratch_shapes=[pltpu.VMEM((tm, D_MODEL), F32)],
        compiler_params=_cparams(("arbitrary",)),
        name="ffn",
    )(x, x, ada_l, ada_l, w1, w2, lng.reshape(1, D_MODEL), lnb.reshape(1, D_MODEL))


INPROJ_NC = 8 * MXU_TILE


def _inproj_kernel(x_ref, ada_ref, w_ref, p_ref, sm_ref):
    h = (x_ref[...] * (1.0 + ada_ref[4:5, :]) + ada_ref[3:4, :]).astype(BF16)
    for c0 in range(0, COL_SM, INPROJ_NC):
        p_ref[:, c0:c0 + INPROJ_NC] = _dot(h, w_ref[:, c0:c0 + INPROJ_NC]).astype(BF16)
    sm_ref[...] = _dot(h, w_ref[:, COL_SM:D_PROJ_PAD])


def _inproj(x, ada_l, w_in_p, n_ctx_tok, lat_len, tm=512):
    t = x.shape[0]
    cond = lambda i: _cond_row(i * tm, n_ctx_tok, lat_len)
    return pl.pallas_call(
        _inproj_kernel,
        grid=(t // tm,),
        in_specs=[
            pl.BlockSpec((tm, D_MODEL), lambda i: (i, 0)),
            pl.BlockSpec((None, N_ADA, D_MODEL), lambda i: (cond(i), 0, 0)),
            pl.BlockSpec((D_MODEL, D_PROJ_PAD), lambda i: (0, 0), pipeline_mode=pl.Buffered(1)),
        ],
        out_specs=[pl.BlockSpec((tm, COL_SM), lambda i: (i, 0)),
                   pl.BlockSpec((tm, D_PROJ_PAD - COL_SM), lambda i: (i, 0))],
        out_shape=[jax.ShapeDtypeStruct((t, COL_SM), BF16),
                   jax.ShapeDtypeStruct((t, D_PROJ_PAD - COL_SM), F32)],
        compiler_params=_cparams(("arbitrary",)),
        name="inproj",
    )(x, ada_l, w_in_p)


MERGE_TM = 256


def _merge_kernel(a_ref, m0_ref, m1_ref, m2_ref, od_ref, og_ref, x_ref, ada_ref, cw_ref,
                  wa_ref, wd_ref, wg_ref, wo_ref, lng_ref, lnb_ref, o_ref, *, n_ctx_tiles):
    i = pl.program_id(0)
    seg = jnp.where(i < n_ctx_tiles, MERGE_TM, GRID_W)
    row = lax.broadcasted_iota(jnp.int32, (MERGE_TM, W_A), 0)
    pos = jnp.bitwise_and(row, seg - 1)
    a_x = a_ref[:, 0:W_A].astype(F32)
    a_b = a_ref[:, W_A:2 * W_A].astype(F32)
    a_c = a_ref[:, 2 * W_A:3 * W_A].astype(F32)
    z = a_c * a_x
    z_prev = jnp.where(pos == 0, 0.0, pltpu.roll(z, 1, 0))
    z_next = jnp.where(pos == seg - 1, 0.0, pltpu.roll(z, MERGE_TM - 1, 0))
    y_a = a_b * (cw_ref[0:1, :] * z_prev + cw_ref[1:2, :] * z + cw_ref[2:3, :] * z_next)
    br_a = _dot(y_a.astype(BF16), wa_ref[...])
    br_d = _dot(od_ref[...], wd_ref[...])
    br_g = _dot(og_ref[...], wg_ref[...])
    gate = lambda m_ref: _sigmoid(m_ref[...].astype(F32))
    merged = gate(m0_ref) * br_a + gate(m1_ref) * br_d + gate(m2_ref) * br_g
    y = _dot(merged.astype(BF16), wo_ref[...])
    y = ALPHA * x_ref[...] + ada_ref[5:6, :] * y
    o_ref[...] = _layer_norm(y, lng_ref[...], lnb_ref[...])


def _merge(p, od, og, x, ada_l, conv_a, wa, wd, wg, wo, lng, lnb, n_ctx_tok, lat_len):
    t = x.shape[0]
    tm = MERGE_TM
    cond = lambda i: _cond_row(i * tm, n_ctx_tok, lat_len)
    full = lambda shape: pl.BlockSpec(shape, lambda i: (0,) * len(shape))
    mg0 = COL_MG // D_MODEL
    return pl.pallas_call(
        functools.partial(_merge_kernel, n_ctx_tiles=n_ctx_tok // tm),
        grid=(t // tm,),
        in_specs=[
            pl.BlockSpec((tm, 3 * W_A), lambda i: (i, 0)),
            pl.BlockSpec((tm, D_MODEL), lambda i: (i, mg0)),
            pl.BlockSpec((tm, D_MODEL), lambda i: (i, mg0 + 1)),
            pl.BlockSpec((tm, D_MODEL), lambda i: (i, mg0 + 2)),
            pl.BlockSpec((tm, 512), lambda i: (i, 0)),
            pl.BlockSpec((tm, 512), lambda i: (i, 0)),
            pl.BlockSpec((tm, D_MODEL), lambda i: (i, 0)),
            pl.BlockSpec((None, N_ADA, D_MODEL), lambda i: (cond(i), 0, 0)),
            full((3, W_A)),
            full((W_A, D_MODEL)), full((512, D_MODEL)), full((512, D_MODEL)), full((D_MODEL, D_MODEL)),
            full((1, D_MODEL)), full((1, D_MODEL)),
        ],
        out_specs=pl.BlockSpec((tm, D_MODEL), lambda i: (i, 0)),
        out_shape=jax.ShapeDtypeStruct((t, D_MODEL), F32),
        compiler_params=_cparams(("arbitrary",)),
        name="merge",
    )(p, p, p, p, od, og, x, ada_l, conv_a, wa, wd, wg, wo, lng.reshape(1, D_MODEL), lnb.reshape(1, D_MODEL))


C = CHUNK_D


def _bdot(a, b):
    return lax.dot_general(a, b, (((2,), (1,)), ((0,), (0,))), preferred_element_type=F32)


def _bdot_tn(a, b):
    return lax.dot_general(a, b, (((1,), (1,)), ((0,), (0,))), preferred_element_type=F32)


def _tri_inverse(m, eye, row, col):
    def same_block(shift):
        return jnp.right_shift(row, shift) == jnp.right_shift(col, shift)

    m8 = jnp.where(same_block(3), m, 0.0)
    m16 = m8.astype(BF16)
    x = eye - m8
    sq = _bdot(m16, m16).astype(BF16)
    x = x + _bdot(x.astype(BF16), sq)
    sq = _bdot(sq, sq).astype(BF16)
    x = x + _bdot(x.astype(BF16), sq)
    for shift in (4, 5, 6):
        e = jnp.where(same_block(shift) & jnp.logical_not(same_block(shift - 1)), m, 0.0)
        x16 = x.astype(BF16)
        x = x - _bdot(_bdot(x16, e.astype(BF16)).astype(BF16), x16)
    return x


NHD = 2 * H_D
CB = 4
HALO = 16


def _delta_kernel(*refs, seq_len, seg, has_init, want_state):
    (q_ref, k_ref, v_ref, z_ref, sm_ref, cw_ref, arow_ref, dtb_ref, ng_ref, cum3_ref, tri2_ref, e2_ref) = refs[:12]
    pos = 12
    s0_ref = None
    if has_init:
        s0_ref = refs[pos]
        pos += 1
    o_ref = refs[pos]
    pos += 1
    sfin_ref = None
    if want_state:
        sfin_ref = refs[pos]
        pos += 1
    u_scr, wq_scr, at_scr, kd_scr, ls_scr, s_scr, of_scr, ob_scr = refs[pos:]

    n_chunks = seq_len // C
    nb = CB * NHD
    row = lax.broadcasted_iota(jnp.int32, (nb, C, C), 1)
    col = lax.broadcasted_iota(jnp.int32, (nb, C, C), 2)
    fwd = jnp.bitwise_and(lax.broadcasted_iota(jnp.int32, (nb, C, C), 0), NHD - 1) < H_D
    dist = jnp.where(fwd, row - col, col - row)
    strict = dist > 0
    incl = dist >= 0
    eye = jnp.where(row == col, 1.0, 0.0).astype(F32)
    row128 = lax.broadcasted_iota(jnp.int32, (C, 128), 0)

    def conv_block(ref, c0, wc0, r0, n):
        x = ref[pl.ds(r0, C), c0:c0 + 128].astype(F32)
        xp = pltpu.roll(x, 1, 0)
        xn = pltpu.roll(x, C - 1, 0)
        if seg == C:
            xp = jnp.where(row128 == 0, 0.0, xp)
            xn = jnp.where(row128 == C - 1, 0.0, xn)
        else:
            prev = ref[pl.ds(pl.multiple_of(jnp.maximum(r0 - HALO, 0), HALO), HALO), c0:c0 + 128].astype(F32)
            nxt = ref[pl.ds(pl.multiple_of(jnp.minimum(r0 + C, seq_len - HALO), HALO), HALO), c0:c0 + 128].astype(F32)
            pm = jnp.where(n > 0, 1.0, 0.0)
            nm = jnp.where(n < n_chunks - 1, 1.0, 0.0)
            xp = jnp.where(row128 == 0, prev[HALO - 1:HALO, :] * pm, xp)
            xn = jnp.where(row128 == C - 1, nxt[0:1, :] * nm, xn)
        y = cw_ref[0:1, wc0:wc0 + 128] * xp + cw_ref[1:2, wc0:wc0 + 128] * x + cw_ref[2:3, wc0:wc0 + 128] * xn
        return _silu(y)

    def l2n(x):
        return x * lax.rsqrt(jnp.sum(x * x, axis=-1, keepdims=True) + RMS_EPS)

    def chunk_body(it, carry):
        qs, ks, vs, kks, qks, betas, gcols, gsums, gtots = [], [], [], [], [], [], [], [], []
        for cc in range(CB):
            n = it * CB + cc
            r0 = pl.multiple_of(n * C, C)
            beta_full = _sigmoid(sm_ref[pl.ds(r0, C), 0:128])
            g_full = -jnp.exp(arow_ref[...]) * _softplus(sm_ref[pl.ds(r0, C), 128:256] + dtb_ref[...])
            cs = _dot(cum3_ref[...], _split3_rows(g_full))
            cols = jnp.concatenate([g_full, beta_full, cs], axis=0)
            chi, clo = _split2(cols)
            bcast = _dot(jnp.concatenate([chi, clo], axis=1), e2_ref[...])
            qh, kh, vh, kkh, qkh = [], [], [], [], []
            for h in range(H_D):
                q = l2n(conv_block(q_ref, h * 128, h * 128, r0, n)) * (DK_D ** -0.5)
                k = l2n(conv_block(k_ref, h * 128, 512 + h * 128, r0, n))
                v = conv_block(v_ref, h * 128, 1024 + h * 128, r0, n)
                k16 = k.astype(BF16)
                qh.append(q)
                kh.append(k)
                vh.append(v)
                kkh.append(_dot_nt(k16, k16))
                qkh.append(_dot_nt(q.astype(BF16), k16))
            for dst, src in ((qs, qh), (ks, kh), (vs, vh), (kks, kkh), (qks, qkh)):
                dst.extend(src + src)
            for b in range(NHD):
                lanes = slice(b * 128, (b + 1) * 128)
                d = b // H_D
                gcols.append(bcast[0:C, lanes])
                betas.append(bcast[C:2 * C, lanes])
                gsums.append(bcast[(2 + d) * C:(3 + d) * C, lanes])
                gtots.append(bcast[4 * C:5 * C, lanes])
        st = lambda xs: jnp.stack(xs, axis=0)
        q, k, v, kk, qk = st(qs), st(ks), st(vs), st(kks), st(qks)
        beta, gcol, gsum, gtot = st(betas), st(gcols), st(gsums), st(gtots)
        xg = jnp.where(strict, gcol[:, :, 0:C], 0.0)
        xh, xl = _split2(xg)
        tri2 = jnp.concatenate([tri2_ref[...]] * CB, axis=0)
        diff = _bdot(tri2, jnp.concatenate([xh, xl], axis=1))
        gamma = jnp.where(incl, jnp.exp(jnp.minimum(diff, 0.0)), 0.0)
        m = jnp.where(strict, beta[:, :, 0:C] * kk * gamma, 0.0)
        eg = jnp.exp(gsum)
        rhs = jnp.concatenate([v * beta, k * (beta * eg)], axis=2)
        sol = _bdot(_tri_inverse(m, eye, row, col).astype(BF16), rhs.astype(BF16))
        wq = jnp.concatenate([sol[:, :, 128:], q * eg], axis=1).astype(BF16)
        at = (qk * gamma).astype(BF16)
        kd = (k * jnp.exp(gtot - gsum)).astype(BF16)
        ls = jnp.exp(gtot[:, 0:8, :])
        for cc in range(CB):
            n = it * CB + cc
            for d in range(2):
                t = n if d == 0 else n_chunks - 1 - n
                dst = pl.ds(t * NHD + d * H_D, H_D)
                src = slice(cc * NHD + d * H_D, cc * NHD + (d + 1) * H_D)
                u_scr[dst] = sol[src, :, :128]
                wq_scr[dst] = wq[src]
                at_scr[dst] = at[src]
                kd_scr[dst] = kd[src]
                ls_scr[dst] = ls[src]
        return carry

    lax.fori_loop(0, n_chunks // CB, chunk_body, 0)

    if has_init:
        s_scr[...] = s0_ref[0]
    else:
        s_scr[...] = jnp.zeros_like(s_scr)

    def scan_body(i, carry):
        slot = pl.ds(i * NHD, NHD)
        s = s_scr[...]
        r = _bdot(wq_scr[slot], s.astype(BF16))
        v16 = (u_scr[slot] - r[:, 0:C]).astype(BF16)
        o = r[:, C:2 * C] + _bdot(at_scr[slot], v16)
        s_scr[...] = s * ls_scr[slot][:, 0:1, :] + _bdot_tn(kd_scr[slot], v16)
        for d in range(2):
            n = i if d == 0 else n_chunks - 1 - i
            r0 = pl.multiple_of(n * C, C)
            o_dst = of_scr if d == 0 else ob_scr
            for h in range(H_D):
                o_dst[pl.ds(r0, C), h * 128:(h + 1) * 128] = o[d * H_D + h]
        return carry

    lax.fori_loop(0, n_chunks, scan_body, 0)

    if want_state:
        sfin_ref[0] = s_scr[...]

    def out_body(n, carry):
        r0 = pl.multiple_of(n * C, C)
        for h in range(H_D):
            o = of_scr[pl.ds(r0, C), h * 128:(h + 1) * 128] + ob_scr[pl.ds(r0, C), h * 128:(h + 1) * 128]
            o = o * lax.rsqrt(jnp.mean(o * o, axis=-1, keepdims=True) + RMS_EPS) * ng_ref[...]
            zg = z_ref[pl.ds(r0, C), h * 128:(h + 1) * 128].astype(F32)
            o_ref[pl.ds(r0, C), h * 128:(h + 1) * 128] = (o * _silu(zg)).astype(BF16)
        return carry

    lax.fori_loop(0, n_chunks, out_body, 0)


def _delta(p, sm, consts, conv_qkv, arow, dtb, ng, s0, *, seq_len, n_seq, row_blk0, seg, want_state):
    has_init = s0 is not None
    n_chunks = seq_len // C
    nhd = 2 * H_D
    cq = COL_DQ // 512
    full = lambda shape: pl.BlockSpec(shape, lambda s: (0,) * len(shape))
    in_specs = [
        pl.BlockSpec((seq_len, 512), lambda s: (s + row_blk0, cq)),
        pl.BlockSpec((seq_len, 512), lambda s: (s + row_blk0, cq + 1)),
        pl.BlockSpec((seq_len, 512), lambda s: (s + row_blk0, cq + 2)),
        pl.BlockSpec((seq_len, 512), lambda s: (s + row_blk0, cq + 3)),
        pl.BlockSpec((seq_len, SM_W), lambda s: (s + row_blk0, 0)),
        full((3, 1536)), full((1, 128)), full((1, 128)), full((1, 128)),
        full((3 * C, 3 * C)), full((NHD, C, 2 * C)), full((2 * 128, NHD * 128)),
    ]
    args = [p, p, p, p, sm, conv_qkv, arow, dtb, ng, consts["cum3_d"], consts["tri2_d"], consts["e2_d"]]
    if has_init:
        in_specs.append(pl.BlockSpec((1, nhd, DK_D, DV_D), lambda s: (s, 0, 0, 0)))
        args.append(s0)
    out_specs = [pl.BlockSpec((seq_len, 512), lambda s: (s, 0))]
    out_shape = [jax.ShapeDtypeStruct((n_seq * seq_len, 512), BF16)]
    if want_state:
        out_specs.append(pl.BlockSpec((1, nhd, DK_D, DV_D), lambda s: (s, 0, 0, 0)))
        out_shape.append(jax.ShapeDtypeStruct((n_seq, nhd, DK_D, DV_D), F32))
    res = pl.pallas_call(
        functools.partial(_delta_kernel, seq_len=seq_len, seg=seg, has_init=has_init, want_state=want_state),
        grid=(n_seq,),
        in_specs=in_specs,
        out_specs=out_specs,
        out_shape=out_shape,
        scratch_shapes=[
            pltpu.VMEM((nhd * n_chunks, C, 128), F32),
            pltpu.VMEM((nhd * n_chunks, 2 * C, 128), BF16),
            pltpu.VMEM((nhd * n_chunks, C, C), BF16),
            pltpu.VMEM((nhd * n_chunks, C, 128), BF16),
            pltpu.VMEM((nhd * n_chunks, 8, 128), F32),
            pltpu.VMEM((nhd, DK_D, DV_D), F32),
            pltpu.VMEM((seq_len, 512), F32),
            pltpu.VMEM((seq_len, 512), F32),
        ],
        compiler_params=_cparams(("arbitrary",)),
        name="delta_lat" if has_init else "delta_ctx",
    )(*args)
    return res


CG = CHUNK_G
GB = 64
HP = H_G // 2


def _gla_kernel(*refs, seq_len, has_init, want_state):
    (q_ref, k_ref, v_ref, r_ref, sm_ref, w2_ref, b_ref, ng_ref, cum3_ref, sele_ref) = refs[:10]
    pos = 10
    s0_ref = None
    if has_init:
        s0_ref = refs[pos]
        pos += 1
    o_ref = refs[pos]
    pos += 1
    sfin_ref = None
    if want_state:
        sfin_ref = refs[pos]
        pos += 1
    bc_scr, tot_scr, oacc_scr, ointer_scr = refs[pos:]

    n_blocks = seq_len // GB
    n_chunks = seq_len // CG
    rowi = lax.broadcasted_iota(jnp.int32, (CG, H_G * DK_G), 0)
    half = CG // 2
    rowh = lax.broadcasted_iota(jnp.int32, (half, H_G * DK_G), 0)
    zero_half = jnp.zeros((GB // CG, half, H_G * DK_G), F32)
    qscale = DK_G ** -0.5

    def pairs(x, w):
        return jnp.stack([x[:, p * w:(p + 1) * w] for p in range(HP)], axis=0)

    sm16 = sm_ref[...].astype(BF16)
    for d in range(2):
        logits = _dot(sm16, w2_ref[d]) + b_ref[d]
        la = -_softplus(-logits) * (1.0 / GLA_TAU)
        for blk in range(n_blocks):
            cs = _dot(cum3_ref[...], _split3_rows(la[blk * GB:(blk + 1) * GB]))
            bc_scr[d, blk * GB:(blk + 1) * GB, :] = cs[d * GB:(d + 1) * GB]
            tot_scr[d, blk * GB:(blk + 1) * GB, :] = cs[2 * GB:3 * GB]

    def block_body(b, carry):
        r0 = pl.multiple_of(b * GB, GB)
        q = q_ref[pl.ds(r0, GB), :].astype(F32) * qscale
        k = k_ref[pl.ds(r0, GB), :].astype(F32)
        v = v_ref[pl.ds(r0, GB), :].astype(F32)
        nc = GB // CG
        q3, k3, v3 = (t.reshape(nc, CG, t.shape[-1]) for t in (q, k, v))
        o_blk = None
        for d in range(2):
            bc3 = bc_scr[d, pl.ds(r0, GB), :].reshape(nc, CG, H_G * DK_G)
            pieces = []
            for j in range(CG):
                if d == 0:
                    rs = slice(half if j >= half else 0, CG)
                    keep = (rowh >= j - half) if j >= half else (rowi >= j)
                else:
                    rs = slice(0, half if j < half else CG)
                    keep = (rowh <= j) if j < half else (rowi <= j)
                e = jnp.exp(jnp.minimum(bc3[:, rs] - bc3[:, j:j + 1], 0.0))
                a = jnp.where(keep, q3[:, rs] * e * k3[:, j:j + 1], 0.0)
                if rs.stop - rs.start < CG:
                    a = jnp.concatenate([zero_half, a] if rs.start else [a, zero_half], axis=1)
                pieces.append(a.astype(BF16))
            a_all = jnp.concatenate(pieces, axis=1)
            rr = _dot(a_all.reshape(nc * CG * CG, a_all.shape[-1]), sele_ref[...])
            rr = rr.reshape(nc, CG * CG, rr.shape[-1])
            top = [j for j in range(CG) if d == 1 or j < half]
            bot = [j for j in range(CG) if d == 0 or j >= half]
            acc_t = functools.reduce(jnp.add, [rr[:, j * CG:j * CG + half] * v3[:, j:j + 1] for j in top])
            acc_b = functools.reduce(jnp.add, [rr[:, j * CG + half:(j + 1) * CG] * v3[:, j:j + 1] for j in bot])
            acc = jnp.concatenate([acc_t, acc_b], axis=1)
            o_blk = acc if d == 0 else o_blk + acc
        oacc_scr[pl.ds(r0, GB), :] = o_blk.reshape(GB, o_blk.shape[-1])
        return carry

    lax.fori_loop(0, n_blocks, block_body, 0)

    if has_init:
        st0 = (s0_ref[0, 0:HP], s0_ref[0, HP:2 * HP])
    else:
        st0 = (jnp.zeros((HP, 2 * DV_G, 2 * DK_G), F32),) * 2
    prow = lax.broadcasted_iota(jnp.int32, (2 * DV_G, 2 * DK_G), 0) // DV_G
    pcol = lax.broadcasted_iota(jnp.int32, (2 * DV_G, 2 * DK_G), 1) // DK_G
    pmask = jnp.where(prow == pcol, 1.0, 0.0).astype(F32)

    lane_head = lax.broadcasted_iota(jnp.int32, (1, 2 * DK_G), 1) // DK_G
    head_mask = [jnp.where(lane_head == a, 1.0, 0.0).astype(F32) for a in range(2)]
    nc = GB // CG
    bnt = lambda a, b: lax.dot_general(a, b, (((2,), (2,)), ((0,), (0,))), preferred_element_type=F32)

    nk = GB - CG
    krow = [lax.broadcasted_iota(jnp.int32, (nk, H_G * DK_G), 0) + off for off in (0, CG)]

    def scan_body(i, carry):
        new, inter, r0s, qts, kts, vks = [], [], [], [], [], []
        for d in range(2):
            st = carry[d]
            blk = i if d == 0 else n_blocks - 1 - i
            r0 = pl.multiple_of(blk * GB, GB)
            bcum = bc_scr[d, pl.ds(r0, GB), :]
            tot = tot_scr[d, pl.ds(r0, GB), :]
            q = q_ref[pl.ds(r0, GB), :].astype(F32) * qscale
            k = k_ref[pl.ds(r0, GB), :].astype(F32)
            v16 = v_ref[pl.ds(r0, GB), :]
            qd = pairs(q * jnp.exp(bcum), 2 * DK_G).astype(BF16)
            kd = pairs(k * jnp.exp(tot - bcum), 2 * DK_G).astype(BF16)
            o = bnt(qd, st.astype(BF16))
            inter.append(jnp.concatenate([o[p] for p in range(HP)], axis=1))
            r0s.append(r0)
            upd = _bdot_tn(pairs(v16, 2 * DV_G), kd)
            new.append(st * jnp.exp(pairs(tot, 2 * DK_G)[:, 0:1, :]) + upd * pmask)
            kwin = slice(0, nk) if d == 0 else slice(CG, GB)
            for c in (range(1, nc) if d == 0 else range(nc - 1)):
                rows = slice(c * CG, (c + 1) * CG)
                ref_row = c * CG - 1 if d == 0 else (c + 1) * CG
                valid = (krow[0] < c * CG) if d == 0 else (krow[1] >= (c + 1) * CG)
                bref = bcum[ref_row:ref_row + 1, :]
                qt = q[rows] * jnp.exp(bcum[rows] - bref)
                kt = jnp.where(valid, k[kwin] * jnp.exp(jnp.minimum(bref - bcum[kwin], 0.0)), 0.0)
                qts.append(pairs(qt, 2 * DK_G))
                kts.append(pairs(kt, 2 * DK_G).astype(BF16))
                vks.append(pairs(v16[kwin], 2 * DV_G))
        cat = lambda xs: jnp.concatenate(xs, axis=0)
        qt = cat(qts)
        qq = jnp.concatenate([qt * head_mask[0], qt * head_mask[1]], axis=1).astype(BF16)
        attn = bnt(qq, cat(kts)).astype(BF16)
        ov = _bdot(attn, cat(vks))
        for d in range(2):
            out_rows = []
            for c in range(nc):
                rows = slice(c * CG, (c + 1) * CG)
                ci = c - 1 if d == 0 else c
                if ci < 0 or ci >= nc - 1:
                    out_rows.append(inter[d][rows])
                    continue
                base = (d * (nc - 1) + ci) * HP
                cross = jnp.concatenate([ov[base + p, a * CG:(a + 1) * CG, a * DV_G:(a + 1) * DV_G]
                                         for p in range(HP) for a in range(2)], axis=1)
                out_rows.append(inter[d][rows] + cross)
            ointer_scr[d, pl.ds(r0s[d], GB), :] = jnp.concatenate(out_rows, axis=0)
        return tuple(new)

    st_fin = lax.fori_loop(0, n_blocks, scan_body, st0)

    if want_state:
        sfin_ref[0, 0:HP] = st_fin[0]
        sfin_ref[0, HP:2 * HP] = st_fin[1]

    def out_body(b, carry):
        r0 = pl.multiple_of(b * GB, GB)
        for h in range(H_G):
            lanes = slice(h * DV_G, (h + 1) * DV_G)
            o = oacc_scr[pl.ds(r0, GB), lanes] + ointer_scr[0, pl.ds(r0, GB), lanes] + ointer_scr[1, pl.ds(r0, GB), lanes]
            o = o * lax.rsqrt(jnp.mean(o * o, axis=-1, keepdims=True) + RMS_EPS) * ng_ref[...]
            rg = r_ref[pl.ds(r0, GB), h * DV_G:(h + 1) * DV_G].astype(F32)
            o_ref[pl.ds(r0, GB), h * DV_G:(h + 1) * DV_G] = (o * _silu(rg)).astype(BF16)
        return carry

    lax.fori_loop(0, n_blocks, out_body, 0)


def _gla(p, sm, consts, w2p, gb, ng, s0, *, seq_len, n_seq, row_blk0, want_state):
    has_init = s0 is not None
    full = lambda shape: pl.BlockSpec(shape, lambda s: (0,) * len(shape))
    hk, hv = H_G * DK_G, H_G * DV_G
    in_specs = [
        pl.BlockSpec((seq_len, hk), lambda s: (s + row_blk0, COL_GQ // hk)),
        pl.BlockSpec((seq_len, hk), lambda s: (s + row_blk0, COL_GQ // hk + 1)),
        pl.BlockSpec((seq_len, hv), lambda s: (s + row_blk0, (COL_GQ + 2 * hk) // hv)),
        pl.BlockSpec((seq_len, hv), lambda s: (s + row_blk0, (COL_GQ + 2 * hk) // hv + 1)),
        pl.BlockSpec((seq_len, 128), lambda s: (s + row_blk0, 0)),
        full((2, 128, hk)), full((2, 1, hk)), full((1, DV_G)),
        full((3 * GB, 3 * GB)), full((CG * CG, hv)),
    ]
    args = [p, p, p, p, sm, w2p, gb, ng, consts["cum3_d"], consts["sele"]]
    if has_init:
        in_specs.append(pl.BlockSpec((1, 2 * HP, 2 * DV_G, 2 * DK_G), lambda s: (s, 0, 0, 0)))
        args.append(s0)
    out_specs = [pl.BlockSpec((seq_len, hv), lambda s: (s, 0))]
    out_shape = [jax.ShapeDtypeStruct((n_seq * seq_len, hv), BF16)]
    if want_state:
        out_specs.append(pl.BlockSpec((1, 2 * HP, 2 * DV_G, 2 * DK_G), lambda s: (s, 0, 0, 0)))
        out_shape.append(jax.ShapeDtypeStruct((n_seq, 2 * HP, 2 * DV_G, 2 * DK_G), F32))
    return pl.pallas_call(
        functools.partial(_gla_kernel, seq_len=seq_len, has_init=has_init, want_state=want_state),
        grid=(n_seq,),
        in_specs=in_specs,
        out_specs=out_specs,
        out_shape=out_shape,
        scratch_shapes=[
            pltpu.VMEM((2, seq_len, hk), F32),
            pltpu.VMEM((2, seq_len, hk), F32),
            pltpu.VMEM((seq_len, hv), F32),
            pltpu.VMEM((2, seq_len, hv), F32),
        ],
        compiler_params=_cparams(("arbitrary",)),
        name="gla_lat" if has_init else "gla_ctx",
    )(*args)


def _constants():
    i = np.arange(C)
    lo = (i[:, None] >= i[None, :]).astype(np.float32)
    up = (i[:, None] <= i[None, :]).astype(np.float32)
    ones = np.ones((C, C), np.float32)
    cum_d = np.concatenate([lo, up, ones], axis=0)
    hk, hv = H_G * DK_G, H_G * DV_G
    sele = (np.arange(hk)[:, None] // DK_G == np.arange(hv)[None, :] // DV_G).astype(np.float32)
    lane_sel = (np.arange(128)[:, None] == np.arange(NHD * 128)[None, :] // 128).astype(np.float32)
    return {
        "e2_d": jnp.asarray(np.tile(lane_sel, (2, 1)), BF16),
        "cum3_d": jnp.asarray(np.tile(cum_d, (1, 3)), BF16),
        "tri2_d": jnp.asarray(np.stack([np.tile(lo, (1, 2))] * H_D + [np.tile(up, (1, 2))] * H_D), BF16),
        "sele": jnp.asarray(sele, BF16),
    }


def _permute_w_in(w_in_l):
    widths = (W_A, W_A, W_A, 512, 512, 512, 512, 8, 8, 256, 256, 512, 512, 32, 3 * D_MODEL)
    offs = np.concatenate([[0], np.cumsum(widths)])
    seg = lambda a, b: w_in_l[:, offs[a]:offs[b]]
    zeros = lambda n: jnp.zeros((D_MODEL, n), w_in_l.dtype)
    narrow = [seg(7, 8), zeros(SM_LR - 8), seg(13, 14), zeros(SM_A - SM_LR - 2 * GLA_RANK),
              seg(8, 9), zeros(SM_W - SM_A - 8)]
    return jnp.concatenate([seg(0, 3), seg(3, 7), seg(9, 13), seg(14, 15)] + narrow, axis=1).astype(BF16)


def _gla_pack_state(s):
    n = s.shape[0]
    st = jnp.swapaxes(s, -1, -2).reshape(n, 2, HP, 2, DV_G, DK_G)
    packed = jnp.einsum("ndpavk,ab->ndpavbk", st, jnp.eye(2, dtype=s.dtype))
    return packed.reshape(n, 2 * HP, 2 * DV_G, 2 * DK_G)


def _gla_unpack_state(sp):
    n = sp.shape[0]
    s6 = sp.reshape(n, 2, HP, 2, DV_G, 2, DK_G)
    diag = jnp.stack([s6[:, :, :, a, :, a, :] for a in range(2)], axis=3)
    return jnp.swapaxes(diag.reshape(n, 2, H_G, DV_G, DK_G), -1, -2)


def _lane_row(vals8, lane0):
    return jnp.zeros((1, 128), F32).at[0, lane0:lane0 + 8].set(vals8.reshape(8).astype(F32))


def kernel(x_prompt, x_sample, state_delta, state_gla, c, c_ctx, w_ada, b_ada, ln_g, ln_b, ffn_w1, ffn_w2, w_in,
           conv_a, conv_qkv, delta_a_log, delta_dt_bias, delta_norm_g, gla_w2, gla_b, gla_norm_g,
           w_br_a, w_br_d, w_br_g, w_o):
    n_ctx, ctx_len, _ = x_prompt.shape
    n_lat, lat_len, _ = x_sample.shape
    n_ctx_tok = n_ctx * ctx_len
    n_lat_tok = n_lat * lat_len
    assert ctx_len == MERGE_TM and lat_len % MERGE_TM == 0 and n_ctx_tok % lat_len == 0 and n_lat <= 8

    consts = _constants()
    x = jnp.concatenate([x_prompt.reshape(n_ctx_tok, D_MODEL), x_sample.reshape(n_lat_tok, D_MODEL)], axis=0)
    cond16 = jnp.zeros((16, D_MODEL), F32).at[:n_lat].set(c).at[8].set(c_ctx)
    ada = _ada_table(cond16, w_ada, b_ada)

    sds, sgs = [], []
    for l in range(DEPTH):
        ada_l = ada[l]
        kw = dict(n_ctx_tok=n_ctx_tok, lat_len=lat_len)
        x = _ffn(x, ada_l, ffn_w1[l, 0].astype(BF16), ffn_w2[l, 0].astype(BF16), ln_g[l, 0], ln_b[l, 0], 0, **kw)
        p, sm = _inproj(x, ada_l, _permute_w_in(w_in[l]), **kw)

        arow = _lane_row(delta_a_log[l], SM_A % 128)
        dtb = _lane_row(delta_dt_bias[l], SM_A % 128)
        ngd = delta_norm_g[l].reshape(1, DV_D)
        od_ctx, sd = _delta(p, sm, consts, conv_qkv[l], arow, dtb, ngd, None, seq_len=ctx_len, n_seq=n_ctx,
                            row_blk0=0, seg=ctx_len, want_state=True)
        s0d = state_delta[:, l].astype(F32).reshape(n_lat, 2 * H_D, DK_D, DV_D)
        (od_lat,) = _delta(p, sm, consts, conv_qkv[l], arow, dtb, ngd, s0d, seq_len=lat_len, n_seq=n_lat,
                           row_blk0=n_ctx_tok // lat_len, seg=GRID_W, want_state=False)
        od = jnp.concatenate([od_ctx, od_lat], axis=0)
        sds.append(sd.reshape(n_ctx, 2, H_D, DK_D, DV_D))

        w2p = jnp.zeros((2, 128, H_G * DK_G), F32)
        for d in range(2):
            w2p = w2p.at[d, SM_LR + d * GLA_RANK:SM_LR + (d + 1) * GLA_RANK].set(gla_w2[l, d])
        w2p = w2p.astype(BF16)
        gb = gla_b[l].reshape(2, 1, H_G * DK_G)
        ngg = gla_norm_g[l].reshape(1, DV_G)
        og_ctx, sg = _gla(p, sm, consts, w2p, gb, ngg, None, seq_len=ctx_len, n_seq=n_ctx, row_blk0=0,
                          want_state=True)
        s0g = _gla_pack_state(state_gla[:, l].astype(F32))
        (og_lat,) = _gla(p, sm, consts, w2p, gb, ngg, s0g, seq_len=lat_len, n_seq=n_lat,
                         row_blk0=n_ctx_tok // lat_len, want_state=False)
        og = jnp.concatenate([og_ctx, og_lat], axis=0)
        sgs.append(_gla_unpack_state(sg))

        x = _merge(p, od, og, x, ada_l, conv_a[l], w_br_a[l].astype(BF16), w_br_d[l].astype(BF16),
                   w_br_g[l].astype(BF16), w_o[l].astype(BF16), ln_g[l, 1], ln_b[l, 1], **kw)
        x = _ffn(x, ada_l, ffn_w1[l, 1].astype(BF16), ffn_w2[l, 1].astype(BF16), ln_g[l, 2], ln_b[l, 2], 2, **kw)

    y_prompt = x[:n_ctx_tok].reshape(n_ctx, ctx_len, D_MODEL)
    y_sample = x[n_ctx_tok:].reshape(n_lat, lat_len, D_MODEL)
    new_state_delta = jnp.stack(sds, axis=1).astype(x_prompt.dtype)
    new_state_gla = jnp.stack(sgs, axis=1).astype(x_prompt.dtype)
    return (y_prompt, y_sample, new_state_delta, new_state_gla)
```

```python
import functools

import jax
import jax.numpy as jnp
import numpy as np
from jax import lax
from jax.experimental import pallas as pl
from jax.experimental.pallas import tpu as pltpu

F32 = jnp.float32
BF16 = jnp.bfloat16

D_MODEL = 1024
DEPTH = 2
GRID_W = 64
D_FF = 2816
W_A = 512
H_D, DK_D, DV_D, CHUNK_D = 4, 128, 128, 64
H_G, DK_G, DV_G, CHUNK_G = 4, 64, 128, 16
GLA_RANK = 16
GLA_TAU = 16.0
N_ADA = 9
ALPHA = float((2 * DEPTH) ** 0.25)
LN_EPS = 1e-5
RMS_EPS = 1e-6

COL_A = 0
COL_DQ = 1536
COL_GQ = 3584
COL_MG = 5120
COL_SM = 8192
SM_W = 256
D_PROJ_PAD = COL_SM + SM_W
SM_BETA, SM_LR, SM_A = 0, 16, 128

VMEM_LIMIT = 56 * 1024 * 1024


def _cparams(sem):
    return pltpu.CompilerParams(dimension_semantics=sem, vmem_limit_bytes=VMEM_LIMIT)


def _dot(a, b):
    return jnp.dot(a, b, preferred_element_type=F32)


def _dot_nt(a, b):
    return lax.dot_general(a, b, (((1,), (1,)), ((), ())), preferred_element_type=F32)


def _dot_tn(a, b):
    return lax.dot_general(a, b, (((0,), (0,)), ((), ())), preferred_element_type=F32)


def _sigmoid(x):
    return 1.0 / (1.0 + jnp.exp(-x))


def _silu(x):
    return x * _sigmoid(x)


def _softplus(x):
    return jnp.maximum(x, 0.0) + jnp.log1p(jnp.exp(-jnp.abs(x)))


def _split2(x):
    hi = x.astype(BF16)
    lo = (x - hi.astype(F32)).astype(BF16)
    return hi, lo


def _split3_rows(x):
    hi = x.astype(BF16)
    r = x - hi.astype(F32)
    mid = r.astype(BF16)
    lo = (r - mid.astype(F32)).astype(BF16)
    return jnp.concatenate([hi, mid, lo], axis=0)


def _mm3(a, b):
    ah, al = _split2(a)
    bh, bl = _split2(b)
    n = a.shape[0]
    p = _dot(jnp.concatenate([ah, al], axis=0), bh)
    return p[:n] + p[n:] + _dot(ah, bl)


def _mm1(a, b):
    return _dot(a.astype(BF16), b.astype(BF16))


def _layer_norm(y, g, b):
    mu = jnp.mean(y, axis=-1, keepdims=True)
    yc = y - mu
    var = jnp.mean(yc * yc, axis=-1, keepdims=True)
    return yc * lax.rsqrt(var + LN_EPS) * g + b


def _ada_kernel(cond_ref, w_ref, b_ref, o_ref):
    s = _silu(cond_ref[...]).astype(BF16)
    o_ref[0] = _dot(s, w_ref[0].astype(BF16)) + b_ref[0]


def _ada_table(cond16, w_ada, b_ada):
    n_l = w_ada.shape[0]
    tn = 1024
    out = pl.pallas_call(
        _ada_kernel,
        grid=(n_l, N_ADA * D_MODEL // tn),
        in_specs=[
            pl.BlockSpec((16, D_MODEL), lambda l, j: (0, 0)),
            pl.BlockSpec((1, D_MODEL, tn), lambda l, j: (l, 0, j)),
            pl.BlockSpec((1, 1, tn), lambda l, j: (l, 0, j)),
        ],
        out_specs=pl.BlockSpec((1, 16, tn), lambda l, j: (l, 0, j)),
        out_shape=jax.ShapeDtypeStruct((n_l, 16, N_ADA * D_MODEL), F32),
        compiler_params=_cparams(("arbitrary", "arbitrary")),
        name="ada",
    )(cond16, w_ada, b_ada.reshape(n_l, 1, N_ADA * D_MODEL))
    return out.reshape(n_l, 16, N_ADA, D_MODEL)


def _cond_row(tok0, n_ctx_tok, lat_len):
    return jnp.where(tok0 < n_ctx_tok, 8, (tok0 - n_ctx_tok) // lat_len)


MXU_TILE = 256
FF_SPLIT = (D_FF // MXU_TILE // 2) * MXU_TILE
FF_CHUNKS = ((0, FF_SPLIT), (FF_SPLIT, D_FF))


W1_ROWS = 128
W2_ROWS = 256


def _stage_weight(w_hbm, dst, stage, sem, rows):
    n_chunks = w_hbm.shape[0] // rows

    def copy(c):
        return pltpu.make_async_copy(w_hbm.at[pl.ds(c * rows, rows), :], stage.at[c % 2], sem.at[c % 2])

    copy(0).start()
    for c in range(n_chunks):
        if c + 1 < n_chunks:
            copy(c + 1).start()
        copy(c).wait()
        dst[c * rows:(c + 1) * rows, :] = stage[c % 2].astype(BF16)


def _ffn_kernel(xc_ref, xp_ref, adac_ref, adap_ref, w1_hbm, w2_hbm, lng_ref, lnb_ref, o_ref,
                y_scr, w1_ref, w2_ref, st1, st2, sem1, sem2, *, l, j):
    jj = j // 2

    @pl.when(pl.program_id(0) == 0)
    def _():
        y_scr[...] = jnp.zeros_like(y_scr)
        _stage_weight(w1_hbm.at[l, jj], w1_ref, st1, sem1, W1_ROWS)
        _stage_weight(w2_hbm.at[l, jj], w2_ref, st2, sem2, W2_ROWS)

    h = (xc_ref[...] * (1.0 + adac_ref[3 * j + 1:3 * j + 2, :]) + adac_ref[3 * j:3 * j + 1, :]).astype(BF16)
    y_new = None
    for c0, c1 in FF_CHUNKS:
        g = _dot(h, w1_ref[:, c0:c1])
        u = _dot(h, w1_ref[:, D_FF + c0:D_FF + c1])
        part = _dot((_silu(g) * u).astype(BF16), w2_ref[c0:c1, :])
        y_new = part if y_new is None else y_new + part

    y = ALPHA * xp_ref[...] + 0.5 * adap_ref[3 * j + 2:3 * j + 3, :] * y_scr[...]
    o_ref[...] = _layer_norm(y, lng_ref[...], lnb_ref[...])
    y_scr[...] = y_new


def _ffn(x, ada_l, ffn_w1, ffn_w2, lng, lnb, l, j, n_ctx_tok, lat_len, tm=512):
    t = x.shape[0]
    n = t // tm
    cur = lambda i: jnp.minimum(i, n - 1)
    prev = lambda i: jnp.maximum(i - 1, 0)
    cond = lambda i: _cond_row(i * tm, n_ctx_tok, lat_len)
    resident = lambda shape: pl.BlockSpec(shape, lambda i: (0,) * len(shape), pipeline_mode=pl.Buffered(1))
    return pl.pallas_call(
        functools.partial(_ffn_kernel, l=l, j=j),
        grid=(n + 1,),
        in_specs=[
            pl.BlockSpec((tm, D_MODEL), lambda i: (cur(i), 0)),
            pl.BlockSpec((tm, D_MODEL), lambda i: (prev(i), 0)),
            pl.BlockSpec((None, N_ADA, D_MODEL), lambda i: (cond(cur(i)), 0, 0)),
            pl.BlockSpec((None, N_ADA, D_MODEL), lambda i: (cond(prev(i)), 0, 0)),
            pl.BlockSpec(memory_space=pl.ANY),
            pl.BlockSpec(memory_space=pl.ANY),
            resident((1, D_MODEL)),
            resident((1, D_MODEL)),
        ],
        out_specs=pl.BlockSpec((tm, D_MODEL), lambda i: (prev(i), 0)),
        out_shape=jax.ShapeDtypeStruct((t, D_MODEL), F32),
        scratch_shapes=[
            pltpu.VMEM((tm, D_MODEL), F32),
            pltpu.VMEM((D_MODEL, 2 * D_FF), BF16),
            pltpu.VMEM((D_FF, D_MODEL), BF16),
            pltpu.VMEM((2, W1_ROWS, 2 * D_FF), F32),
            pltpu.VMEM((2, W2_ROWS, D_MODEL), F32),
            pltpu.SemaphoreType.DMA((2,)),
            pltpu.SemaphoreType.DMA((2,)),
        ],
        compiler_params=_cparams(("arbitrary",)),
        name="ffn",
    )(x, x, ada_l, ada_l, ffn_w1, ffn_w2, lng.reshape(1, D_MODEL), lnb.reshape(1, D_MODEL))


INPROJ_NC = 8 * MXU_TILE


def _inproj_kernel(x_ref, ada_ref, w_ref, p_ref, sm_ref):
    h = (x_ref[...] * (1.0 + ada_ref[4:5, :]) + ada_ref[3:4, :]).astype(BF16)
    for c0 in range(0, COL_SM, INPROJ_NC):
        p_ref[:, c0:c0 + INPROJ_NC] = _dot(h, w_ref[:, c0:c0 + INPROJ_NC]).astype(BF16)
    sm_ref[...] = _dot(h, w_ref[:, COL_SM:D_PROJ_PAD])


def _inproj(x, ada_l, w_in_p, n_ctx_tok, lat_len, tm=512):
    t = x.shape[0]
    cond = lambda i: _cond_row(i * tm, n_ctx_tok, lat_len)
    return pl.pallas_call(
        _inproj_kernel,
        grid=(t // tm,),
        in_specs=[
            pl.BlockSpec((tm, D_MODEL), lambda i: (i, 0)),
            pl.BlockSpec((None, N_ADA, D_MODEL), lambda i: (cond(i), 0, 0)),
            pl.BlockSpec((D_MODEL, D_PROJ_PAD), lambda i: (0, 0), pipeline_mode=pl.Buffered(1)),
        ],
        out_specs=[pl.BlockSpec((tm, COL_SM), lambda i: (i, 0)),
                   pl.BlockSpec((tm, D_PROJ_PAD - COL_SM), lambda i: (i, 0))],
        out_shape=[jax.ShapeDtypeStruct((t, COL_SM), BF16),
                   jax.ShapeDtypeStruct((t, D_PROJ_PAD - COL_SM), F32)],
        compiler_params=_cparams(("arbitrary",)),
        name="inproj",
    )(x, ada_l, w_in_p)


MERGE_TM = 256


def _merge_kernel(a_ref, m0_ref, m1_ref, m2_ref, odc_ref, odl_ref, ogc_ref, ogl_ref, x_ref, ada_ref, cw_ref,
                  wa_ref, wd_ref, wg_ref, wo_ref, lng_ref, lnb_ref, o_ref, *, n_ctx_tiles):
    i = pl.program_id(0)
    is_ctx = i < n_ctx_tiles
    o_d = jnp.where(is_ctx, odc_ref[...], odl_ref[...])
    o_g = jnp.where(is_ctx, ogc_ref[...], ogl_ref[...])
    seg = jnp.where(i < n_ctx_tiles, MERGE_TM, GRID_W)
    row = lax.broadcasted_iota(jnp.int32, (MERGE_TM, W_A), 0)
    pos = jnp.bitwise_and(row, seg - 1)
    a_x = a_ref[:, 0:W_A].astype(F32)
    a_b = a_ref[:, W_A:2 * W_A].astype(F32)
    a_c = a_ref[:, 2 * W_A:3 * W_A].astype(F32)
    z = a_c * a_x
    z_prev = jnp.where(pos == 0, 0.0, pltpu.roll(z, 1, 0))
    z_next = jnp.where(pos == seg - 1, 0.0, pltpu.roll(z, MERGE_TM - 1, 0))
    y_a = a_b * (cw_ref[0:1, :] * z_prev + cw_ref[1:2, :] * z + cw_ref[2:3, :] * z_next)
    br_a = _dot(y_a.astype(BF16), wa_ref[...])
    br_d = _dot(o_d, wd_ref[...])
    br_g = _dot(o_g, wg_ref[...])
    gate = lambda m_ref: _sigmoid(m_ref[...].astype(F32))
    merged = gate(m0_ref) * br_a + gate(m1_ref) * br_d + gate(m2_ref) * br_g
    y = _dot(merged.astype(BF16), wo_ref[...])
    y = ALPHA * x_ref[...] + ada_ref[5:6, :] * y
    o_ref[...] = _layer_norm(y, lng_ref[...], lnb_ref[...])


def _merge(p, od_ctx, od_lat, og_ctx, og_lat, x, ada_l, conv_a, wa, wd, wg, wo, lng, lnb, n_ctx_tok, lat_len):
    t = x.shape[0]
    tm = MERGE_TM
    n_ctx_tiles = n_ctx_tok // tm
    cond = lambda i: _cond_row(i * tm, n_ctx_tok, lat_len)
    full = lambda shape: pl.BlockSpec(shape, lambda i: (0,) * len(shape))
    ctx_blk = lambda i: (jnp.minimum(i, n_ctx_tiles - 1), 0)
    lat_blk = lambda i: (jnp.maximum(i - n_ctx_tiles, 0), 0)
    mg0 = COL_MG // D_MODEL
    return pl.pallas_call(
        functools.partial(_merge_kernel, n_ctx_tiles=n_ctx_tiles),
        grid=(t // tm,),
        in_specs=[
            pl.BlockSpec((tm, 3 * W_A), lambda i: (i, 0)),
            pl.BlockSpec((tm, D_MODEL), lambda i: (i, mg0)),
            pl.BlockSpec((tm, D_MODEL), lambda i: (i, mg0 + 1)),
            pl.BlockSpec((tm, D_MODEL), lambda i: (i, mg0 + 2)),
            pl.BlockSpec((tm, 512), ctx_blk), pl.BlockSpec((tm, 512), lat_blk),
            pl.BlockSpec((tm, 512), ctx_blk), pl.BlockSpec((tm, 512), lat_blk),
            pl.BlockSpec((tm, D_MODEL), lambda i: (i, 0)),
            pl.BlockSpec((None, N_ADA, D_MODEL), lambda i: (cond(i), 0, 0)),
            full((3, W_A)),
            full((W_A, D_MODEL)), full((512, D_MODEL)), full((512, D_MODEL)), full((D_MODEL, D_MODEL)),
            full((1, D_MODEL)), full((1, D_MODEL)),
        ],
        out_specs=pl.BlockSpec((tm, D_MODEL), lambda i: (i, 0)),
        out_shape=jax.ShapeDtypeStruct((t, D_MODEL), F32),
        compiler_params=_cparams(("arbitrary",)),
        name="merge",
    )(p, p, p, p, od_ctx, od_lat, og_ctx, og_lat, x, ada_l, conv_a, wa, wd, wg, wo,
      lng.reshape(1, D_MODEL), lnb.reshape(1, D_MODEL))


C = CHUNK_D


def _bdot(a, b):
    return lax.dot_general(a, b, (((2,), (1,)), ((0,), (0,))), preferred_element_type=F32)


def _bdot_tn(a, b):
    return lax.dot_general(a, b, (((1,), (1,)), ((0,), (0,))), preferred_element_type=F32)


def _tri_inverse(m, eye, row, col):
    def same_block(shift):
        return jnp.right_shift(row, shift) == jnp.right_shift(col, shift)

    m8 = jnp.where(same_block(3), m, 0.0)
    m16 = m8.astype(BF16)
    x = eye - m8
    sq = _bdot(m16, m16).astype(BF16)
    x = x + _bdot(x.astype(BF16), sq)
    sq = _bdot(sq, sq).astype(BF16)
    x = x + _bdot(x.astype(BF16), sq)
    for shift in (4, 5, 6):
        e = jnp.where(same_block(shift) & jnp.logical_not(same_block(shift - 1)), m, 0.0)
        x16 = x.astype(BF16)
        x = x - _bdot(_bdot(x16, e.astype(BF16)).astype(BF16), x16)
    return x


NHD = 2 * H_D
CB = 4
HALO = 16


def _delta_kernel(*refs, seq_len, seg, has_init, want_state):
    (q_ref, k_ref, v_ref, z_ref, sm_ref, cw_ref, arow_ref, dtb_ref, ng_ref, cum3_ref, tri2_ref, e2_ref) = refs[:12]
    pos = 12
    s0_ref = None
    if has_init:
        s0_ref = refs[pos]
        pos += 1
    o_ref = refs[pos]
    pos += 1
    sfin_ref = None
    if want_state:
        sfin_ref = refs[pos]
        pos += 1
    u_scr, wq_scr, at_scr, kd_scr, ls_scr, s_scr, of_scr, ob_scr = refs[pos:]

    n_chunks = seq_len // C
    nb = CB * NHD
    row = lax.broadcasted_iota(jnp.int32, (nb, C, C), 1)
    col = lax.broadcasted_iota(jnp.int32, (nb, C, C), 2)
    fwd = jnp.bitwise_and(lax.broadcasted_iota(jnp.int32, (nb, C, C), 0), NHD - 1) < H_D
    dist = jnp.where(fwd, row - col, col - row)
    strict = dist > 0
    incl = dist >= 0
    eye = jnp.where(row == col, 1.0, 0.0).astype(F32)
    row128 = lax.broadcasted_iota(jnp.int32, (C, 128), 0)

    def conv_block(ref, c0, wc0, r0, n):
        x = ref[pl.ds(r0, C), c0:c0 + 128].astype(F32)
        xp = pltpu.roll(x, 1, 0)
        xn = pltpu.roll(x, C - 1, 0)
        if seg == C:
            xp = jnp.where(row128 == 0, 0.0, xp)
            xn = jnp.where(row128 == C - 1, 0.0, xn)
        else:
            prev = ref[pl.ds(pl.multiple_of(jnp.maximum(r0 - HALO, 0), HALO), HALO), c0:c0 + 128].astype(F32)
            nxt = ref[pl.ds(pl.multiple_of(jnp.minimum(r0 + C, seq_len - HALO), HALO), HALO), c0:c0 + 128].astype(F32)
            pm = jnp.where(n > 0, 1.0, 0.0)
            nm = jnp.where(n < n_chunks - 1, 1.0, 0.0)
            xp = jnp.where(row128 == 0, prev[HALO - 1:HALO, :] * pm, xp)
            xn = jnp.where(row128 == C - 1, nxt[0:1, :] * nm, xn)
        y = cw_ref[0:1, wc0:wc0 + 128] * xp + cw_ref[1:2, wc0:wc0 + 128] * x + cw_ref[2:3, wc0:wc0 + 128] * xn
        return _silu(y)

    def l2n(x):
        return x * lax.rsqrt(jnp.sum(x * x, axis=-1, keepdims=True) + RMS_EPS)

    def chunk_body(it, carry):
        qs, ks, vs, kks, qks, betas, gcols, gsums, gtots = [], [], [], [], [], [], [], [], []
        for cc in range(CB):
            n = it * CB + cc
            r0 = pl.multiple_of(n * C, C)
            beta_full = _sigmoid(sm_ref[pl.ds(r0, C), 0:128])
            g_full = -jnp.exp(arow_ref[...]) * _softplus(sm_ref[pl.ds(r0, C), 128:256] + dtb_ref[...])
            cs = _dot(cum3_ref[...], _split3_rows(g_full))
            cols = jnp.concatenate([g_full, beta_full, cs], axis=0)
            chi, clo = _split2(cols)
            bcast = _dot(jnp.concatenate([chi, clo], axis=1), e2_ref[...])
            qh, kh, vh, kkh, qkh = [], [], [], [], []
            for h in range(H_D):
                q = l2n(conv_block(q_ref, h * 128, h * 128, r0, n)) * (DK_D ** -0.5)
                k = l2n(conv_block(k_ref, h * 128, 512 + h * 128, r0, n))
                v = conv_block(v_ref, h * 128, 1024 + h * 128, r0, n)
                k16 = k.astype(BF16)
                qh.append(q)
                kh.append(k)
                vh.append(v)
                kkh.append(_dot_nt(k16, k16))
                qkh.append(_dot_nt(q.astype(BF16), k16))
            for dst, src in ((qs, qh), (ks, kh), (vs, vh), (kks, kkh), (qks, qkh)):
                dst.extend(src + src)
            for b in range(NHD):
                lanes = slice(b * 128, (b + 1) * 128)
                d = b // H_D
                gcols.append(bcast[0:C, lanes])
                betas.append(bcast[C:2 * C, lanes])
                gsums.append(bcast[(2 + d) * C:(3 + d) * C, lanes])
                gtots.append(bcast[4 * C:5 * C, lanes])
        st = lambda xs: jnp.stack(xs, axis=0)
        q, k, v, kk, qk = st(qs), st(ks), st(vs), st(kks), st(qks)
        beta, gcol, gsum, gtot = st(betas), st(gcols), st(gsums), st(gtots)
        xg = jnp.where(strict, gcol[:, :, 0:C], 0.0)
        xh, xl = _split2(xg)
        tri2 = jnp.concatenate([tri2_ref[...]] * CB, axis=0)
        diff = _bdot(tri2, jnp.concatenate([xh, xl], axis=1))
        gamma = jnp.where(incl, jnp.exp(jnp.minimum(diff, 0.0)), 0.0)
        m = jnp.where(strict, beta[:, :, 0:C] * kk * gamma, 0.0)
        eg = jnp.exp(gsum)
        rhs = jnp.concatenate([v * beta, k * (beta * eg)], axis=2)
        sol = _bdot(_tri_inverse(m, eye, row, col).astype(BF16), rhs.astype(BF16))
        wq = jnp.concatenate([sol[:, :, 128:], q * eg], axis=1).astype(BF16)
        at = (qk * gamma).astype(BF16)
        kd = (k * jnp.exp(gtot - gsum)).astype(BF16)
        ls = jnp.exp(gtot[:, 0:8, :])
        for cc in range(CB):
            n = it * CB + cc
            for d in range(2):
                t = n if d == 0 else n_chunks - 1 - n
                dst = pl.ds(t * NHD + d * H_D, H_D)
                src = slice(cc * NHD + d * H_D, cc * NHD + (d + 1) * H_D)
                u_scr[dst] = sol[src, :, :128]
                wq_scr[dst] = wq[src]
                at_scr[dst] = at[src]
                kd_scr[dst] = kd[src]
                ls_scr[dst] = ls[src]
        return carry

    lax.fori_loop(0, n_chunks // CB, chunk_body, 0)

    if has_init:
        s_scr[...] = s0_ref[0]
    else:
        s_scr[...] = jnp.zeros_like(s_scr)

    def scan_body(i, carry):
        slot = pl.ds(i * NHD, NHD)
        s = s_scr[...]
        r = _bdot(wq_scr[slot], s.astype(BF16))
        v16 = (u_scr[slot] - r[:, 0:C]).astype(BF16)
        o = r[:, C:2 * C] + _bdot(at_scr[slot], v16)
        s_scr[...] = s * ls_scr[slot][:, 0:1, :] + _bdot_tn(kd_scr[slot], v16)
        for d in range(2):
            n = i if d == 0 else n_chunks - 1 - i
            r0 = pl.multiple_of(n * C, C)
            o_dst = of_scr if d == 0 else ob_scr
            for h in range(H_D):
                o_dst[pl.ds(r0, C), h * 128:(h + 1) * 128] = o[d * H_D + h]
        return carry

    lax.fori_loop(0, n_chunks, scan_body, 0)

    if want_state:
        sfin_ref[0] = s_scr[...]

    def out_body(n, carry):
        r0 = pl.multiple_of(n * C, C)
        for h in range(H_D):
            o = of_scr[pl.ds(r0, C), h * 128:(h + 1) * 128] + ob_scr[pl.ds(r0, C), h * 128:(h + 1) * 128]
            o = o * lax.rsqrt(jnp.mean(o * o, axis=-1, keepdims=True) + RMS_EPS) * ng_ref[...]
            zg = z_ref[pl.ds(r0, C), h * 128:(h + 1) * 128].astype(F32)
            o_ref[pl.ds(r0, C), h * 128:(h + 1) * 128] = (o * _silu(zg)).astype(BF16)
        return carry

    lax.fori_loop(0, n_chunks, out_body, 0)


def _delta(p, sm, consts, conv_qkv, arow, dtb, ng, s0, *, seq_len, n_seq, row_blk0, seg, want_state):
    has_init = s0 is not None
    n_chunks = seq_len // C
    nhd = 2 * H_D
    cq = COL_DQ // 512
    full = lambda shape: pl.BlockSpec(shape, lambda s: (0,) * len(shape))
    in_specs = [
        pl.BlockSpec((seq_len, 512), lambda s: (s + row_blk0, cq)),
        pl.BlockSpec((seq_len, 512), lambda s: (s + row_blk0, cq + 1)),
        pl.BlockSpec((seq_len, 512), lambda s: (s + row_blk0, cq + 2)),
        pl.BlockSpec((seq_len, 512), lambda s: (s + row_blk0, cq + 3)),
        pl.BlockSpec((seq_len, SM_W), lambda s: (s + row_blk0, 0)),
        full((3, 1536)), full((1, 128)), full((1, 128)), full((1, 128)),
        full((3 * C, 3 * C)), full((NHD, C, 2 * C)), full((2 * 128, NHD * 128)),
    ]
    args = [p, p, p, p, sm, conv_qkv, arow, dtb, ng, consts["cum3_d"], consts["tri2_d"], consts["e2_d"]]
    if has_init:
        in_specs.append(pl.BlockSpec((1, nhd, DK_D, DV_D), lambda s: (s, 0, 0, 0)))
        args.append(s0)
    out_specs = [pl.BlockSpec((seq_len, 512), lambda s: (s, 0))]
    out_shape = [jax.ShapeDtypeStruct((n_seq * seq_len, 512), BF16)]
    if want_state:
        out_specs.append(pl.BlockSpec((1, nhd, DK_D, DV_D), lambda s: (s, 0, 0, 0)))
        out_shape.append(jax.ShapeDtypeStruct((n_seq, nhd, DK_D, DV_D), F32))
    res = pl.pallas_call(
        functools.partial(_delta_kernel, seq_len=seq_len, seg=seg, has_init=has_init, want_state=want_state),
        grid=(n_seq,),
        in_specs=in_specs,
        out_specs=out_specs,
        out_shape=out_shape,
        scratch_shapes=[
            pltpu.VMEM((nhd * n_chunks, C, 128), F32),
            pltpu.VMEM((nhd * n_chunks, 2 * C, 128), BF16),
            pltpu.VMEM((nhd * n_chunks, C, C), BF16),
            pltpu.VMEM((nhd * n_chunks, C, 128), BF16),
            pltpu.VMEM((nhd * n_chunks, 8, 128), F32),
            pltpu.VMEM((nhd, DK_D, DV_D), F32),
            pltpu.VMEM((seq_len, 512), F32),
            pltpu.VMEM((seq_len, 512), F32),
        ],
        compiler_params=_cparams(("arbitrary",)),
        name="delta_lat" if has_init else "delta_ctx",
    )(*args)
    return res


CG = CHUNK_G
GB = 64
HP = H_G // 2


def _gla_kernel(*refs, seq_len, has_init, want_state):
    (q_ref, k_ref, v_ref, r_ref, sm_ref, w2_ref, b_ref, ng_ref, cum3_ref, sele_ref) = refs[:10]
    pos = 10
    s0_ref = None
    if has_init:
        s0_ref = refs[pos]
        pos += 1
    o_ref = refs[pos]
    pos += 1
    sfin_ref = None
    if want_state:
        sfin_ref = refs[pos]
        pos += 1
    bc_scr, tot_scr, oacc_scr, ointer_scr = refs[pos:]

    n_blocks = seq_len // GB
    n_chunks = seq_len // CG
    rowi = lax.broadcasted_iota(jnp.int32, (CG, H_G * DK_G), 0)
    half = CG // 2
    rowh = lax.broadcasted_iota(jnp.int32, (half, H_G * DK_G), 0)
    zero_half = jnp.zeros((GB // CG, half, H_G * DK_G), F32)
    qscale = DK_G ** -0.5

    def pairs(x, w):
        return jnp.stack([x[:, p * w:(p + 1) * w] for p in range(HP)], axis=0)

    sm16 = sm_ref[...].astype(BF16)
    for d in range(2):
        logits = _dot(sm16, w2_ref[d]) + b_ref[d]
        la = -_softplus(-logits) * (1.0 / GLA_TAU)
        for blk in range(n_blocks):
            cs = _dot(cum3_ref[...], _split3_rows(la[blk * GB:(blk + 1) * GB]))
            bc_scr[d, blk * GB:(blk + 1) * GB, :] = cs[d * GB:(d + 1) * GB]
            tot_scr[d, blk * GB:(blk + 1) * GB, :] = cs[2 * GB:3 * GB]

    def block_body(b, carry):
        r0 = pl.multiple_of(b * GB, GB)
        q = q_ref[pl.ds(r0, GB), :].astype(F32) * qscale
        k = k_ref[pl.ds(r0, GB), :].astype(F32)
        v = v_ref[pl.ds(r0, GB), :].astype(F32)
        nc = GB // CG
        q3, k3, v3 = (t.reshape(nc, CG, t.shape[-1]) for t in (q, k, v))
        o_blk = None
        for d in range(2):
            bc3 = bc_scr[d, pl.ds(r0, GB), :].reshape(nc, CG, H_G * DK_G)
            pieces = []
            for j in range(CG):
                if d == 0:
                    rs = slice(half if j >= half else 0, CG)
                    keep = (rowh >= j - half) if j >= half else (rowi >= j)
                else:
                    rs = slice(0, half if j < half else CG)
                    keep = (rowh <= j) if j < half else (rowi <= j)
                e = jnp.exp(jnp.minimum(bc3[:, rs] - bc3[:, j:j + 1], 0.0))
                a = jnp.where(keep, q3[:, rs] * e * k3[:, j:j + 1], 0.0)
                if rs.stop - rs.start < CG:
                    a = jnp.concatenate([zero_half, a] if rs.start else [a, zero_half], axis=1)
                pieces.append(a.astype(BF16))
            a_all = jnp.concatenate(pieces, axis=1)
            rr = _dot(a_all.reshape(nc * CG * CG, a_all.shape[-1]), sele_ref[...])
            rr = rr.reshape(nc, CG * CG, rr.shape[-1])
            top = [j for j in range(CG) if d == 1 or j < half]
            bot = [j for j in range(CG) if d == 0 or j >= half]
            acc_t = functools.reduce(jnp.add, [rr[:, j * CG:j * CG + half] * v3[:, j:j + 1] for j in top])
            acc_b = functools.reduce(jnp.add, [rr[:, j * CG + half:(j + 1) * CG] * v3[:, j:j + 1] for j in bot])
            acc = jnp.concatenate([acc_t, acc_b], axis=1)
            o_blk = acc if d == 0 else o_blk + acc
        oacc_scr[pl.ds(r0, GB), :] = o_blk.reshape(GB, o_blk.shape[-1])
        return carry

    lax.fori_loop(0, n_blocks, block_body, 0)

    if has_init:
        st0 = (s0_ref[0, 0:HP], s0_ref[0, HP:2 * HP])
    else:
        st0 = (jnp.zeros((HP, 2 * DV_G, 2 * DK_G), F32),) * 2
    prow = lax.broadcasted_iota(jnp.int32, (2 * DV_G, 2 * DK_G), 0) // DV_G
    pcol = lax.broadcasted_iota(jnp.int32, (2 * DV_G, 2 * DK_G), 1) // DK_G
    pmask = jnp.where(prow == pcol, 1.0, 0.0).astype(F32)

    lane_head = lax.broadcasted_iota(jnp.int32, (1, 2 * DK_G), 1) // DK_G
    head_mask = [jnp.where(lane_head == a, 1.0, 0.0).astype(F32) for a in range(2)]
    nc = GB // CG
    bnt = lambda a, b: lax.dot_general(a, b, (((2,), (2,)), ((0,), (0,))), preferred_element_type=F32)

    nk = GB - CG
    krow = [lax.broadcasted_iota(jnp.int32, (nk, H_G * DK_G), 0) + off for off in (0, CG)]

    def scan_body(i, carry):
        new, inter, r0s, qts, kts, vks = [], [], [], [], [], []
        for d in range(2):
            st = carry[d]
            blk = i if d == 0 else n_blocks - 1 - i
            r0 = pl.multiple_of(blk * GB, GB)
            bcum = bc_scr[d, pl.ds(r0, GB), :]
            tot = tot_scr[d, pl.ds(r0, GB), :]
            q = q_ref[pl.ds(r0, GB), :].astype(F32) * qscale
            k = k_ref[pl.ds(r0, GB), :].astype(F32)
            v16 = v_ref[pl.ds(r0, GB), :]
            qd = pairs(q * jnp.exp(bcum), 2 * DK_G).astype(BF16)
            kd = pairs(k * jnp.exp(tot - bcum), 2 * DK_G).astype(BF16)
            o = bnt(qd, st.astype(BF16))
            inter.append(jnp.concatenate([o[p] for p in range(HP)], axis=1))
            r0s.append(r0)
            upd = _bdot_tn(pairs(v16, 2 * DV_G), kd)
            new.append(st * jnp.exp(pairs(tot, 2 * DK_G)[:, 0:1, :]) + upd * pmask)
            kwin = slice(0, nk) if d == 0 else slice(CG, GB)
            for c in (range(1, nc) if d == 0 else range(nc - 1)):
                rows = slice(c * CG, (c + 1) * CG)
                ref_row = c * CG - 1 if d == 0 else (c + 1) * CG
                valid = (krow[0] < c * CG) if d == 0 else (krow[1] >= (c + 1) * CG)
                bref = bcum[ref_row:ref_row + 1, :]
                qt = q[rows] * jnp.exp(bcum[rows] - bref)
                kt = jnp.where(valid, k[kwin] * jnp.exp(jnp.minimum(bref - bcum[kwin], 0.0)), 0.0)
                qts.append(pairs(qt, 2 * DK_G))
                kts.append(pairs(kt, 2 * DK_G).astype(BF16))
                vks.append(pairs(v16[kwin], 2 * DV_G))
        cat = lambda xs: jnp.concatenate(xs, axis=0)
        qt = cat(qts)
        qq = jnp.concatenate([qt * head_mask[0], qt * head_mask[1]], axis=1).astype(BF16)
        attn = bnt(qq, cat(kts)).astype(BF16)
        ov = _bdot(attn, cat(vks))
        for d in range(2):
            out_rows = []
            for c in range(nc):
                rows = slice(c * CG, (c + 1) * CG)
                ci = c - 1 if d == 0 else c
                if ci < 0 or ci >= nc - 1:
                    out_rows.append(inter[d][rows])
                    continue
                base = (d * (nc - 1) + ci) * HP
                cross = jnp.concatenate([ov[base + p, a * CG:(a + 1) * CG, a * DV_G:(a + 1) * DV_G]
                                         for p in range(HP) for a in range(2)], axis=1)
                out_rows.append(inter[d][rows] + cross)
            ointer_scr[d, pl.ds(r0s[d], GB), :] = jnp.concatenate(out_rows, axis=0)
        return tuple(new)

    st_fin = lax.fori_loop(0, n_blocks, scan_body, st0)

    if want_state:
        sfin_ref[0, 0:HP] = st_fin[0]
        sfin_ref[0, HP:2 * HP] = st_fin[1]

    def out_body(b, carry):
        r0 = pl.multiple_of(b * GB, GB)
        for h in range(H_G):
            lanes = slice(h * DV_G, (h + 1) * DV_G)
            o = oacc_scr[pl.ds(r0, GB), lanes] + ointer_scr[0, pl.ds(r0, GB), lanes] + ointer_scr[1, pl.ds(r0, GB), lanes]
            o = o * lax.rsqrt(jnp.mean(o * o, axis=-1, keepdims=True) + RMS_EPS) * ng_ref[...]
            rg = r_ref[pl.ds(r0, GB), h * DV_G:(h + 1) * DV_G].astype(F32)
            o_ref[pl.ds(r0, GB), h * DV_G:(h + 1) * DV_G] = (o * _silu(rg)).astype(BF16)
        return carry

    lax.fori_loop(0, n_blocks, out_body, 0)


def _gla(p, sm, consts, w2p, gb, ng, s0, *, seq_len, n_seq, row_blk0, want_state):
    has_init = s0 is not None
    full = lambda shape: pl.BlockSpec(shape, lambda s: (0,) * len(shape))
    hk, hv = H_G * DK_G, H_G * DV_G
    in_specs = [
        pl.BlockSpec((seq_len, hk), lambda s: (s + row_blk0, COL_GQ // hk)),
        pl.BlockSpec((seq_len, hk), lambda s: (s + row_blk0, COL_GQ // hk + 1)),
        pl.BlockSpec((seq_len, hv), lambda s: (s + row_blk0, (COL_GQ + 2 * hk) // hv)),
        pl.BlockSpec((seq_len, hv), lambda s: (s + row_blk0, (COL_GQ + 2 * hk) // hv + 1)),
        pl.BlockSpec((seq_len, 128), lambda s: (s + row_blk0, 0)),
        full((2, 128, hk)), full((2, 1, hk)), full((1, DV_G)),
        full((3 * GB, 3 * GB)), full((CG * CG, hv)),
    ]
    args = [p, p, p, p, sm, w2p, gb, ng, consts["cum3_d"], consts["sele"]]
    if has_init:
        in_specs.append(pl.BlockSpec((1, 2 * HP, 2 * DV_G, 2 * DK_G), lambda s: (s, 0, 0, 0)))
        args.append(s0)
    out_specs = [pl.BlockSpec((seq_len, hv), lambda s: (s, 0))]
    out_shape = [jax.ShapeDtypeStruct((n_seq * seq_len, hv), BF16)]
    if want_state:
        out_specs.append(pl.BlockSpec((1, 2 * HP, 2 * DV_G, 2 * DK_G), lambda s: (s, 0, 0, 0)))
        out_shape.append(jax.ShapeDtypeStruct((n_seq, 2 * HP, 2 * DV_G, 2 * DK_G), F32))
    return pl.pallas_call(
        functools.partial(_gla_kernel, seq_len=seq_len, has_init=has_init, want_state=want_state),
        grid=(n_seq,),
        in_specs=in_specs,
        out_specs=out_specs,
        out_shape=out_shape,
        scratch_shapes=[
            pltpu.VMEM((2, seq_len, hk), F32),
            pltpu.VMEM((2, seq_len, hk), F32),
            pltpu.VMEM((seq_len, hv), F32),
            pltpu.VMEM((2, seq_len, hv), F32),
        ],
        compiler_params=_cparams(("arbitrary",)),
        name="gla_lat" if has_init else "gla_ctx",
    )(*args)


def _constants():
    i = np.arange(C)
    lo = (i[:, None] >= i[None, :]).astype(np.float32)
    up = (i[:, None] <= i[None, :]).astype(np.float32)
    ones = np.ones((C, C), np.float32)
    cum_d = np.concatenate([lo, up, ones], axis=0)
    hk, hv = H_G * DK_G, H_G * DV_G
    sele = (np.arange(hk)[:, None] // DK_G == np.arange(hv)[None, :] // DV_G).astype(np.float32)
    lane_sel = (np.arange(128)[:, None] == np.arange(NHD * 128)[None, :] // 128).astype(np.float32)
    return {
        "e2_d": jnp.asarray(np.tile(lane_sel, (2, 1)), BF16),
        "cum3_d": jnp.asarray(np.tile(cum_d, (1, 3)), BF16),
        "tri2_d": jnp.asarray(np.stack([np.tile(lo, (1, 2))] * H_D + [np.tile(up, (1, 2))] * H_D), BF16),
        "sele": jnp.asarray(sele, BF16),
    }


def _permute_w_in(w_in_l):
    widths = (W_A, W_A, W_A, 512, 512, 512, 512, 8, 8, 256, 256, 512, 512, 32, 3 * D_MODEL)
    offs = np.concatenate([[0], np.cumsum(widths)])
    seg = lambda a, b: w_in_l[:, offs[a]:offs[b]]
    zeros = lambda n: jnp.zeros((D_MODEL, n), w_in_l.dtype)
    narrow = [seg(7, 8), zeros(SM_LR - 8), seg(13, 14), zeros(SM_A - SM_LR - 2 * GLA_RANK),
              seg(8, 9), zeros(SM_W - SM_A - 8)]
    return jnp.concatenate([seg(0, 3), seg(3, 7), seg(9, 13), seg(14, 15)] + narrow, axis=1).astype(BF16)


def _gla_pack_state(s):
    n = s.shape[0]
    st = jnp.swapaxes(s, -1, -2).reshape(n, 2, HP, 2, DV_G, DK_G)
    packed = jnp.einsum("ndpavk,ab->ndpavbk", st, jnp.eye(2, dtype=s.dtype))
    return packed.reshape(n, 2 * HP, 2 * DV_G, 2 * DK_G)


def _gla_unpack_state(sp):
    n = sp.shape[0]
    s6 = sp.reshape(n, 2, HP, 2, DV_G, 2, DK_G)
    diag = jnp.stack([s6[:, :, :, a, :, a, :] for a in range(2)], axis=3)
    return jnp.swapaxes(diag.reshape(n, 2, H_G, DV_G, DK_G), -1, -2)


def _lane_row(vals8, lane0):
    return jnp.zeros((1, 128), F32).at[0, lane0:lane0 + 8].set(vals8.reshape(8).astype(F32))


def kernel(x_prompt, x_sample, state_delta, state_gla, c, c_ctx, w_ada, b_ada, ln_g, ln_b, ffn_w1, ffn_w2, w_in,
           conv_a, conv_qkv, delta_a_log, delta_dt_bias, delta_norm_g, gla_w2, gla_b, gla_norm_g,
           w_br_a, w_br_d, w_br_g, w_o):
    n_ctx, ctx_len, _ = x_prompt.shape
    n_lat, lat_len, _ = x_sample.shape
    n_ctx_tok = n_ctx * ctx_len
    n_lat_tok = n_lat * lat_len
    assert ctx_len == MERGE_TM and lat_len % MERGE_TM == 0 and n_ctx_tok % lat_len == 0 and n_lat <= 8

    consts = _constants()
    x = jnp.concatenate([x_prompt.reshape(n_ctx_tok, D_MODEL), x_sample.reshape(n_lat_tok, D_MODEL)], axis=0)
    cond16 = jnp.zeros((16, D_MODEL), F32).at[:n_lat].set(c).at[8].set(c_ctx)
    ada = _ada_table(cond16, w_ada, b_ada)

    sds, sgs = [], []
    for l in range(DEPTH):
        ada_l = ada[l]
        kw = dict(n_ctx_tok=n_ctx_tok, lat_len=lat_len)
        x = _ffn(x, ada_l, ffn_w1, ffn_w2, ln_g[l, 0], ln_b[l, 0], l, 0, **kw)
        p, sm = _inproj(x, ada_l, _permute_w_in(w_in[l]), **kw)

        arow = _lane_row(delta_a_log[l], SM_A % 128)
        dtb = _lane_row(delta_dt_bias[l], SM_A % 128)
        ngd = delta_norm_g[l].reshape(1, DV_D)
        od_ctx, sd = _delta(p, sm, consts, conv_qkv[l], arow, dtb, ngd, None, seq_len=ctx_len, n_seq=n_ctx,
                            row_blk0=0, seg=ctx_len, want_state=True)
        s0d = state_delta[:, l].astype(F32).reshape(n_lat, 2 * H_D, DK_D, DV_D)
        (od_lat,) = _delta(p, sm, consts, conv_qkv[l], arow, dtb, ngd, s0d, seq_len=lat_len, n_seq=n_lat,
                           row_blk0=n_ctx_tok // lat_len, seg=GRID_W, want_state=False)
        sds.append(sd.reshape(n_ctx, 2, H_D, DK_D, DV_D))

        w2p = jnp.zeros((2, 128, H_G * DK_G), F32)
        for d in range(2):
            w2p = w2p.at[d, SM_LR + d * GLA_RANK:SM_LR + (d + 1) * GLA_RANK].set(gla_w2[l, d])
        w2p = w2p.astype(BF16)
        gb = gla_b[l].reshape(2, 1, H_G * DK_G)
        ngg = gla_norm_g[l].reshape(1, DV_G)
        og_ctx, sg = _gla(p, sm, consts, w2p, gb, ngg, None, seq_len=ctx_len, n_seq=n_ctx, row_blk0=0,
                          want_state=True)
        s0g = _gla_pack_state(state_gla[:, l].astype(F32))
        (og_lat,) = _gla(p, sm, consts, w2p, gb, ngg, s0g, seq_len=lat_len, n_seq=n_lat,
                         row_blk0=n_ctx_tok // lat_len, want_state=False)
        sgs.append(_gla_unpack_state(sg))

        x = _merge(p, od_ctx, od_lat, og_ctx, og_lat, x, ada_l, conv_a[l], w_br_a[l].astype(BF16), w_br_d[l].astype(BF16),
                   w_br_g[l].astype(BF16), w_o[l].astype(BF16), ln_g[l, 1], ln_b[l, 1], **kw)
        x = _ffn(x, ada_l, ffn_w1, ffn_w2, ln_g[l, 2], ln_b[l, 2], l, 2, **kw)

    y_prompt = x[:n_ctx_tok].reshape(n_ctx, ctx_len, D_MODEL)
    y_sample = x[n_ctx_tok:].reshape(n_lat, lat_len, D_MODEL)
    new_state_delta = jnp.stack(sds, axis=1).astype(x_prompt.dtype)
    new_state_gla = jnp.stack(sgs, axis=1).astype(x_prompt.dtype)
    return (y_prompt, y_sample, new_state_delta, new_state_gla)
```

```python
import functools

import jax
import jax.numpy as jnp
import numpy as np
from jax import lax
from jax.experimental import pallas as pl
from jax.experimental.pallas import tpu as pltpu

F32 = jnp.float32
BF16 = jnp.bfloat16

D_MODEL = 1024
DEPTH = 2
GRID_W = 64
D_FF = 2816
W_A = 512
H_D, DK_D, DV_D, CHUNK_D = 4, 128, 128, 64
H_G, DK_G, DV_G, CHUNK_G = 4, 64, 128, 16
GLA_RANK = 16
GLA_TAU = 16.0
N_ADA = 9
ALPHA = float((2 * DEPTH) ** 0.25)
LN_EPS = 1e-5
RMS_EPS = 1e-6

COL_A = 0
COL_DQ = 1536
COL_GQ = 3584
COL_MG = 5120
COL_SM = 8192
SM_W = 256
D_PROJ_PAD = COL_SM + SM_W
SM_BETA, SM_LR, SM_A = 0, 16, 128

VMEM_LIMIT = 56 * 1024 * 1024


def _cparams(sem):
    return pltpu.CompilerParams(dimension_semantics=sem, vmem_limit_bytes=VMEM_LIMIT)


def _dot(a, b):
    return jnp.dot(a, b, preferred_element_type=F32)


def _dot_nt(a, b):
    return lax.dot_general(a, b, (((1,), (1,)), ((), ())), preferred_element_type=F32)


def _dot_tn(a, b):
    return lax.dot_general(a, b, (((0,), (0,)), ((), ())), preferred_element_type=F32)


def _sigmoid(x):
    return 1.0 / (1.0 + jnp.exp(-x))


def _silu(x):
    return x * _sigmoid(x)


def _softplus(x):
    return jnp.maximum(x, 0.0) + jnp.log1p(jnp.exp(-jnp.abs(x)))


def _split2(x):
    hi = x.astype(BF16)
    lo = (x - hi.astype(F32)).astype(BF16)
    return hi, lo


def _split3_rows(x):
    hi = x.astype(BF16)
    r = x - hi.astype(F32)
    mid = r.astype(BF16)
    lo = (r - mid.astype(F32)).astype(BF16)
    return jnp.concatenate([hi, mid, lo], axis=0)


def _mm3(a, b):
    ah, al = _split2(a)
    bh, bl = _split2(b)
    n = a.shape[0]
    p = _dot(jnp.concatenate([ah, al], axis=0), bh)
    return p[:n] + p[n:] + _dot(ah, bl)


def _mm1(a, b):
    return _dot(a.astype(BF16), b.astype(BF16))


def _layer_norm(y, g, b):
    mu = jnp.mean(y, axis=-1, keepdims=True)
    yc = y - mu
    var = jnp.mean(yc * yc, axis=-1, keepdims=True)
    return yc * lax.rsqrt(var + LN_EPS) * g + b


def _ada_kernel(cond_ref, w_ref, b_ref, o_ref):
    s = _silu(cond_ref[...]).astype(BF16)
    o_ref[0] = _dot(s, w_ref[0].astype(BF16)) + b_ref[0]


def _ada_table(cond16, w_ada, b_ada):
    n_l = w_ada.shape[0]
    tn = 1024
    out = pl.pallas_call(
        _ada_kernel,
        grid=(n_l, N_ADA * D_MODEL // tn),
        in_specs=[
            pl.BlockSpec((16, D_MODEL), lambda l, j: (0, 0)),
            pl.BlockSpec((1, D_MODEL, tn), lambda l, j: (l, 0, j)),
            pl.BlockSpec((1, 1, tn), lambda l, j: (l, 0, j)),
        ],
        out_specs=pl.BlockSpec((1, 16, tn), lambda l, j: (l, 0, j)),
        out_shape=jax.ShapeDtypeStruct((n_l, 16, N_ADA * D_MODEL), F32),
        compiler_params=_cparams(("arbitrary", "arbitrary")),
        name="ada",
    )(cond16, w_ada, b_ada.reshape(n_l, 1, N_ADA * D_MODEL))
    return out.reshape(n_l, 16, N_ADA, D_MODEL)


def _cond_row(tok0, n_ctx_tok, lat_len):
    return jnp.where(tok0 < n_ctx_tok, 8, (tok0 - n_ctx_tok) // lat_len)


MXU_TILE = 256
FF_SPLIT = (D_FF // MXU_TILE // 2) * MXU_TILE
FF_CHUNKS = ((0, FF_SPLIT), (FF_SPLIT, D_FF))


W1_ROWS = 128
W2_ROWS = 256


def _stage_weight(w_hbm, dst, stage, sem, rows):
    n_chunks = w_hbm.shape[0] // rows

    def copy(c):
        return pltpu.make_async_copy(w_hbm.at[pl.ds(c * rows, rows), :], stage.at[c % 2], sem.at[c % 2])

    copy(0).start()
    for c in range(n_chunks):
        if c + 1 < n_chunks:
            copy(c + 1).start()
        copy(c).wait()
        dst[c * rows:(c + 1) * rows, :] = stage[c % 2].astype(BF16)


def _ffn_kernel(*refs, l, j, n, n_ctx_tiles, split_in, split_out):
    n_x = 4 if split_in else 2
    x_refs = refs[:n_x]
    adac_ref, adap_ref, w1_hbm, w2_hbm, lng_ref, lnb_ref = refs[n_x:n_x + 6]
    n_o = 2 if split_out else 1
    o_refs = refs[n_x + 6:n_x + 6 + n_o]
    y_scr, w1_ref, w2_ref, st1, st2, sem1, sem2 = refs[n_x + 6 + n_o:]
    jj = j // 2
    i = pl.program_id(0)
    prev_is_ctx = jnp.maximum(i - 1, 0) < n_ctx_tiles

    @pl.when(i == 0)
    def _():
        y_scr[...] = jnp.zeros_like(y_scr)
        _stage_weight(w1_hbm.at[l, jj], w1_ref, st1, sem1, W1_ROWS)
        _stage_weight(w2_hbm.at[l, jj], w2_ref, st2, sem2, W2_ROWS)

    if split_in:
        x_cur = jnp.where(jnp.minimum(i, n - 1) < n_ctx_tiles, x_refs[0][...], x_refs[1][...])
        x_prev = jnp.where(prev_is_ctx, x_refs[2][...], x_refs[3][...])
    else:
        x_cur, x_prev = x_refs[0][...], x_refs[1][...]

    h = (x_cur * (1.0 + adac_ref[3 * j + 1:3 * j + 2, :]) + adac_ref[3 * j:3 * j + 1, :]).astype(BF16)
    y_new = None
    for c0, c1 in FF_CHUNKS:
        g = _dot(h, w1_ref[:, c0:c1])
        u = _dot(h, w1_ref[:, D_FF + c0:D_FF + c1])
        part = _dot((_silu(g) * u).astype(BF16), w2_ref[c0:c1, :])
        y_new = part if y_new is None else y_new + part

    y = ALPHA * x_prev + 0.5 * adap_ref[3 * j + 2:3 * j + 3, :] * y_scr[...]
    out = _layer_norm(y, lng_ref[...], lnb_ref[...])
    if split_out:
        @pl.when(prev_is_ctx)
        def _():
            o_refs[0][...] = out

        @pl.when(jnp.logical_not(prev_is_ctx))
        def _():
            o_refs[1][...] = out
    else:
        o_refs[0][...] = out
    y_scr[...] = y_new


def _ffn(x, ada_l, ffn_w1, ffn_w2, lng, lnb, l, j, n_ctx_tok, lat_len, tm=512, split_out=False):
    split_in = isinstance(x, tuple)
    n_ctx_tiles = n_ctx_tok // tm
    t = n_ctx_tok + x[1].shape[0] if split_in else x.shape[0]
    n = t // tm
    cur = lambda i: jnp.minimum(i, n - 1)
    prev = lambda i: jnp.maximum(i - 1, 0)
    ctx_blk = lambda f: (lambda i: (jnp.minimum(f(i), n_ctx_tiles - 1), 0))
    lat_blk = lambda f: (lambda i: (jnp.maximum(f(i) - n_ctx_tiles, 0), 0))
    cond = lambda i: _cond_row(i * tm, n_ctx_tok, lat_len)
    resident = lambda shape: pl.BlockSpec(shape, lambda i: (0,) * len(shape), pipeline_mode=pl.Buffered(1))
    tile = lambda index_map: pl.BlockSpec((tm, D_MODEL), index_map)
    if split_in:
        x_specs = [tile(ctx_blk(cur)), tile(lat_blk(cur)), tile(ctx_blk(prev)), tile(lat_blk(prev))]
        x_args = [x[0], x[1], x[0], x[1]]
    else:
        x_specs = [tile(lambda i: (cur(i), 0)), tile(lambda i: (prev(i), 0))]
        x_args = [x, x]
    if split_out:
        out_specs = [tile(ctx_blk(prev)), tile(lat_blk(prev))]
        out_shape = [jax.ShapeDtypeStruct((n_ctx_tok, D_MODEL), F32), jax.ShapeDtypeStruct((t - n_ctx_tok, D_MODEL), F32)]
    else:
        out_specs = tile(lambda i: (prev(i), 0))
        out_shape = jax.ShapeDtypeStruct((t, D_MODEL), F32)
    return pl.pallas_call(
        functools.partial(_ffn_kernel, l=l, j=j, n=n, n_ctx_tiles=n_ctx_tiles, split_in=split_in, split_out=split_out),
        grid=(n + 1,),
        in_specs=x_specs + [
            pl.BlockSpec((None, N_ADA, D_MODEL), lambda i: (cond(cur(i)), 0, 0)),
            pl.BlockSpec((None, N_ADA, D_MODEL), lambda i: (cond(prev(i)), 0, 0)),
            pl.BlockSpec(memory_space=pl.ANY),
            pl.BlockSpec(memory_space=pl.ANY),
            resident((1, D_MODEL)),
            resident((1, D_MODEL)),
        ],
        out_specs=out_specs,
        out_shape=out_shape,
        scratch_shapes=[
            pltpu.VMEM((tm, D_MODEL), F32),
            pltpu.VMEM((D_MODEL, 2 * D_FF), BF16),
            pltpu.VMEM((D_FF, D_MODEL), BF16),
            pltpu.VMEM((2, W1_ROWS, 2 * D_FF), F32),
            pltpu.VMEM((2, W2_ROWS, D_MODEL), F32),
            pltpu.SemaphoreType.DMA((2,)),
            pltpu.SemaphoreType.DMA((2,)),
        ],
        compiler_params=_cparams(("arbitrary",)),
        name="ffn",
    )(*x_args, ada_l, ada_l, ffn_w1, ffn_w2, lng.reshape(1, D_MODEL), lnb.reshape(1, D_MODEL))


INPROJ_NC = 8 * MXU_TILE


def _inproj_kernel(x_ref, ada_ref, w_ref, p_ref, sm_ref):
    h = (x_ref[...] * (1.0 + ada_ref[4:5, :]) + ada_ref[3:4, :]).astype(BF16)
    for c0 in range(0, COL_SM, INPROJ_NC):
        p_ref[:, c0:c0 + INPROJ_NC] = _dot(h, w_ref[:, c0:c0 + INPROJ_NC]).astype(BF16)
    sm_ref[...] = _dot(h, w_ref[:, COL_SM:D_PROJ_PAD])


def _inproj(x, ada_l, w_in_p, n_ctx_tok, lat_len, tm=512):
    t = x.shape[0]
    cond = lambda i: _cond_row(i * tm, n_ctx_tok, lat_len)
    return pl.pallas_call(
        _inproj_kernel,
        grid=(t // tm,),
        in_specs=[
            pl.BlockSpec((tm, D_MODEL), lambda i: (i, 0)),
            pl.BlockSpec((None, N_ADA, D_MODEL), lambda i: (cond(i), 0, 0)),
            pl.BlockSpec((D_MODEL, D_PROJ_PAD), lambda i: (0, 0), pipeline_mode=pl.Buffered(1)),
        ],
        out_specs=[pl.BlockSpec((tm, COL_SM), lambda i: (i, 0)),
                   pl.BlockSpec((tm, D_PROJ_PAD - COL_SM), lambda i: (i, 0))],
        out_shape=[jax.ShapeDtypeStruct((t, COL_SM), BF16),
                   jax.ShapeDtypeStruct((t, D_PROJ_PAD - COL_SM), F32)],
        compiler_params=_cparams(("arbitrary",)),
        name="inproj",
    )(x, ada_l, w_in_p)


MERGE_TM = 512
CTX_SEG = 256


def _merge_kernel(a_ref, m0_ref, m1_ref, m2_ref, odc_ref, odl_ref, ogc_ref, ogl_ref, x_ref, ada_ref, cw_ref,
                  wa_ref, wd_ref, wg_ref, wo_ref, lng_ref, lnb_ref, o_ref, *, n_ctx_tiles):
    i = pl.program_id(0)
    is_ctx = i < n_ctx_tiles
    o_d = jnp.where(is_ctx, odc_ref[...], odl_ref[...])
    o_g = jnp.where(is_ctx, ogc_ref[...], ogl_ref[...])
    seg = jnp.where(i < n_ctx_tiles, CTX_SEG, GRID_W)
    row = lax.broadcasted_iota(jnp.int32, (MERGE_TM, W_A), 0)
    pos = jnp.bitwise_and(row, seg - 1)
    a_x = a_ref[:, 0:W_A].astype(F32)
    a_b = a_ref[:, W_A:2 * W_A].astype(F32)
    a_c = a_ref[:, 2 * W_A:3 * W_A].astype(F32)
    z = a_c * a_x
    z_prev = jnp.where(pos == 0, 0.0, pltpu.roll(z, 1, 0))
    z_next = jnp.where(pos == seg - 1, 0.0, pltpu.roll(z, MERGE_TM - 1, 0))
    y_a = a_b * (cw_ref[0:1, :] * z_prev + cw_ref[1:2, :] * z + cw_ref[2:3, :] * z_next)
    br_a = _dot(y_a.astype(BF16), wa_ref[...])
    br_d = _dot(o_d, wd_ref[...])
    br_g = _dot(o_g, wg_ref[...])
    gate = lambda m_ref: _sigmoid(m_ref[...].astype(F32))
    merged = gate(m0_ref) * br_a + gate(m1_ref) * br_d + gate(m2_ref) * br_g
    y = _dot(merged.astype(BF16), wo_ref[...])
    y = ALPHA * x_ref[...] + ada_ref[5:6, :] * y
    o_ref[...] = _layer_norm(y, lng_ref[...], lnb_ref[...])


def _merge(p, od_ctx, od_lat, og_ctx, og_lat, x, ada_l, conv_a, wa, wd, wg, wo, lng, lnb, n_ctx_tok, lat_len):
    t = x.shape[0]
    tm = MERGE_TM
    n_ctx_tiles = n_ctx_tok // tm
    cond = lambda i: _cond_row(i * tm, n_ctx_tok, lat_len)
    full = lambda shape: pl.BlockSpec(shape, lambda i: (0,) * len(shape))
    ctx_blk = lambda i: (jnp.minimum(i, n_ctx_tiles - 1), 0)
    lat_blk = lambda i: (jnp.maximum(i - n_ctx_tiles, 0), 0)
    mg0 = COL_MG // D_MODEL
    return pl.pallas_call(
        functools.partial(_merge_kernel, n_ctx_tiles=n_ctx_tiles),
        grid=(t // tm,),
        in_specs=[
            pl.BlockSpec((tm, 3 * W_A), lambda i: (i, 0)),
            pl.BlockSpec((tm, D_MODEL), lambda i: (i, mg0)),
            pl.BlockSpec((tm, D_MODEL), lambda i: (i, mg0 + 1)),
            pl.BlockSpec((tm, D_MODEL), lambda i: (i, mg0 + 2)),
            pl.BlockSpec((tm, 512), ctx_blk), pl.BlockSpec((tm, 512), lat_blk),
            pl.BlockSpec((tm, 512), ctx_blk), pl.BlockSpec((tm, 512), lat_blk),
            pl.BlockSpec((tm, D_MODEL), lambda i: (i, 0)),
            pl.BlockSpec((None, N_ADA, D_MODEL), lambda i: (cond(i), 0, 0)),
            full((3, W_A)),
            full((W_A, D_MODEL)), full((512, D_MODEL)), full((512, D_MODEL)), full((D_MODEL, D_MODEL)),
            full((1, D_MODEL)), full((1, D_MODEL)),
        ],
        out_specs=pl.BlockSpec((tm, D_MODEL), lambda i: (i, 0)),
        out_shape=jax.ShapeDtypeStruct((t, D_MODEL), F32),
        compiler_params=_cparams(("arbitrary",)),
        name="merge",
    )(p, p, p, p, od_ctx, od_lat, og_ctx, og_lat, x, ada_l, conv_a, wa, wd, wg, wo,
      lng.reshape(1, D_MODEL), lnb.reshape(1, D_MODEL))


C = CHUNK_D


def _bdot(a, b):
    return lax.dot_general(a, b, (((2,), (1,)), ((0,), (0,))), preferred_element_type=F32)


def _bdot_tn(a, b):
    return lax.dot_general(a, b, (((1,), (1,)), ((0,), (0,))), preferred_element_type=F32)


def _tri_inverse(m, eye, row, col):
    def same_block(shift):
        return jnp.right_shift(row, shift) == jnp.right_shift(col, shift)

    m8 = jnp.where(same_block(3), m, 0.0)
    m16 = m8.astype(BF16)
    x = eye - m8
    sq = _bdot(m16, m16).astype(BF16)
    x = x + _bdot(x.astype(BF16), sq)
    sq = _bdot(sq, sq).astype(BF16)
    x = x + _bdot(x.astype(BF16), sq)
    for shift in (4, 5, 6):
        e = jnp.where(same_block(shift) & jnp.logical_not(same_block(shift - 1)), m, 0.0)
        x16 = x.astype(BF16)
        x = x - _bdot(_bdot(x16, e.astype(BF16)).astype(BF16), x16)
    return x


NHD = 2 * H_D
CB = 4
HALO = 16


def _delta_kernel(*refs, seq_len, seg, has_init, want_state):
    (q_ref, k_ref, v_ref, z_ref, sm_ref, cw_ref, arow_ref, dtb_ref, ng_ref, cum3_ref, tri2_ref, e2_ref) = refs[:12]
    pos = 12
    s0_ref = None
    if has_init:
        s0_ref = refs[pos]
        pos += 1
    o_ref = refs[pos]
    pos += 1
    sfin_ref = None
    if want_state:
        sfin_ref = refs[pos]
        pos += 1
    u_scr, wq_scr, at_scr, kd_scr, ls_scr, s_scr, of_scr, ob_scr = refs[pos:]

    n_chunks = seq_len // C
    nb = CB * NHD
    row = lax.broadcasted_iota(jnp.int32, (nb, C, C), 1)
    col = lax.broadcasted_iota(jnp.int32, (nb, C, C), 2)
    fwd = jnp.bitwise_and(lax.broadcasted_iota(jnp.int32, (nb, C, C), 0), NHD - 1) < H_D
    dist = jnp.where(fwd, row - col, col - row)
    strict = dist > 0
    incl = dist >= 0
    eye = jnp.where(row == col, 1.0, 0.0).astype(F32)
    row128 = lax.broadcasted_iota(jnp.int32, (C, 128), 0)

    def conv_block(ref, c0, wc0, r0, n):
        x = ref[pl.ds(r0, C), c0:c0 + 128].astype(F32)
        xp = pltpu.roll(x, 1, 0)
        xn = pltpu.roll(x, C - 1, 0)
        if seg == C:
            xp = jnp.where(row128 == 0, 0.0, xp)
            xn = jnp.where(row128 == C - 1, 0.0, xn)
        else:
            prev = ref[pl.ds(pl.multiple_of(jnp.maximum(r0 - HALO, 0), HALO), HALO), c0:c0 + 128].astype(F32)
            nxt = ref[pl.ds(pl.multiple_of(jnp.minimum(r0 + C, seq_len - HALO), HALO), HALO), c0:c0 + 128].astype(F32)
            pm = jnp.where(n > 0, 1.0, 0.0)
            nm = jnp.where(n < n_chunks - 1, 1.0, 0.0)
            xp = jnp.where(row128 == 0, prev[HALO - 1:HALO, :] * pm, xp)
            xn = jnp.where(row128 == C - 1, nxt[0:1, :] * nm, xn)
        y = cw_ref[0:1, wc0:wc0 + 128] * xp + cw_ref[1:2, wc0:wc0 + 128] * x + cw_ref[2:3, wc0:wc0 + 128] * xn
        return _silu(y)

    def l2n(x):
        return x * lax.rsqrt(jnp.sum(x * x, axis=-1, keepdims=True) + RMS_EPS)

    def chunk_body(it, carry):
        qs, ks, vs, kks, qks, betas, gcols, gsums, gtots = [], [], [], [], [], [], [], [], []
        for cc in range(CB):
            n = it * CB + cc
            r0 = pl.multiple_of(n * C, C)
            beta_full = _sigmoid(sm_ref[pl.ds(r0, C), 0:128])
            g_full = -jnp.exp(arow_ref[...]) * _softplus(sm_ref[pl.ds(r0, C), 128:256] + dtb_ref[...])
            cs = _dot(cum3_ref[...], _split3_rows(g_full))
            cols = jnp.concatenate([g_full, beta_full, cs], axis=0)
            chi, clo = _split2(cols)
            bcast = _dot(jnp.concatenate([chi, clo], axis=1), e2_ref[...])
            qh, kh, vh, kkh, qkh = [], [], [], [], []
            for h in range(H_D):
                q = l2n(conv_block(q_ref, h * 128, h * 128, r0, n)) * (DK_D ** -0.5)
                k = l2n(conv_block(k_ref, h * 128, 512 + h * 128, r0, n))
                v = conv_block(v_ref, h * 128, 1024 + h * 128, r0, n)
                k16 = k.astype(BF16)
                qh.append(q)
                kh.append(k)
                vh.append(v)
                kkh.append(_dot_nt(k16, k16))
                qkh.append(_dot_nt(q.astype(BF16), k16))
            for dst, src in ((qs, qh), (ks, kh), (vs, vh), (kks, kkh), (qks, qkh)):
                dst.extend(src + src)
            for b in range(NHD):
                lanes = slice(b * 128, (b + 1) * 128)
                d = b // H_D
                gcols.append(bcast[0:C, lanes])
                betas.append(bcast[C:2 * C, lanes])
                gsums.append(bcast[(2 + d) * C:(3 + d) * C, lanes])
                gtots.append(bcast[4 * C:5 * C, lanes])
        st = lambda xs: jnp.stack(xs, axis=0)
        q, k, v, kk, qk = st(qs), st(ks), st(vs), st(kks), st(qks)
        beta, gcol, gsum, gtot = st(betas), st(gcols), st(gsums), st(gtots)
        xg = jnp.where(strict, gcol[:, :, 0:C], 0.0)
        xh, xl = _split2(xg)
        tri2 = jnp.concatenate([tri2_ref[...]] * CB, axis=0)
        diff = _bdot(tri2, jnp.concatenate([xh, xl], axis=1))
        gamma = jnp.where(incl, jnp.exp(jnp.minimum(diff, 0.0)), 0.0)
        m = jnp.where(strict, beta[:, :, 0:C] * kk * gamma, 0.0)
        eg = jnp.exp(gsum)
        rhs = jnp.concatenate([v * beta, k * (beta * eg)], axis=2)
        sol = _bdot(_tri_inverse(m, eye, row, col).astype(BF16), rhs.astype(BF16))
        wq = jnp.concatenate([sol[:, :, 128:], q * eg], axis=1).astype(BF16)
        at = (qk * gamma).astype(BF16)
        kd = (k * jnp.exp(gtot - gsum)).astype(BF16)
        ls = jnp.exp(gtot[:, 0:8, :])
        for cc in range(CB):
            n = it * CB + cc
            for d in range(2):
                t = n if d == 0 else n_chunks - 1 - n
                dst = pl.ds(t * NHD + d * H_D, H_D)
                src = slice(cc * NHD + d * H_D, cc * NHD + (d + 1) * H_D)
                u_scr[dst] = sol[src, :, :128]
                wq_scr[dst] = wq[src]
                at_scr[dst] = at[src]
                kd_scr[dst] = kd[src]
                ls_scr[dst] = ls[src]
        return carry

    lax.fori_loop(0, n_chunks // CB, chunk_body, 0)

    if has_init:
        s_scr[...] = s0_ref[0]
    else:
        s_scr[...] = jnp.zeros_like(s_scr)

    def scan_body(i, carry):
        slot = pl.ds(i * NHD, NHD)
        s = s_scr[...]
        r = _bdot(wq_scr[slot], s.astype(BF16))
        v16 = (u_scr[slot] - r[:, 0:C]).astype(BF16)
        o = r[:, C:2 * C] + _bdot(at_scr[slot], v16)
        s_scr[...] = s * ls_scr[slot][:, 0:1, :] + _bdot_tn(kd_scr[slot], v16)
        for d in range(2):
            n = i if d == 0 else n_chunks - 1 - i
            r0 = pl.multiple_of(n * C, C)
            o_dst = of_scr if d == 0 else ob_scr
            for h in range(H_D):
                o_dst[pl.ds(r0, C), h * 128:(h + 1) * 128] = o[d * H_D + h]
        return carry

    lax.fori_loop(0, n_chunks, scan_body, 0)

    if want_state:
        sfin_ref[0] = s_scr[...]

    def out_body(n, carry):
        r0 = pl.multiple_of(n * C, C)
        for h in range(H_D):
            o = of_scr[pl.ds(r0, C), h * 128:(h + 1) * 128] + ob_scr[pl.ds(r0, C), h * 128:(h + 1) * 128]
            o = o * lax.rsqrt(jnp.mean(o * o, axis=-1, keepdims=True) + RMS_EPS) * ng_ref[...]
            zg = z_ref[pl.ds(r0, C), h * 128:(h + 1) * 128].astype(F32)
            o_ref[pl.ds(r0, C), h * 128:(h + 1) * 128] = (o * _silu(zg)).astype(BF16)
        return carry

    lax.fori_loop(0, n_chunks, out_body, 0)


def _delta(p, sm, consts, conv_qkv, arow, dtb, ng, s0, *, seq_len, n_seq, row_blk0, seg, want_state):
    has_init = s0 is not None
    n_chunks = seq_len // C
    nhd = 2 * H_D
    cq = COL_DQ // 512
    full = lambda shape: pl.BlockSpec(shape, lambda s: (0,) * len(shape))
    in_specs = [
        pl.BlockSpec((seq_len, 512), lambda s: (s + row_blk0, cq)),
        pl.BlockSpec((seq_len, 512), lambda s: (s + row_blk0, cq + 1)),
        pl.BlockSpec((seq_len, 512), lambda s: (s + row_blk0, cq + 2)),
        pl.BlockSpec((seq_len, 512), lambda s: (s + row_blk0, cq + 3)),
        pl.BlockSpec((seq_len, SM_W), lambda s: (s + row_blk0, 0)),
        full((3, 1536)), full((1, 128)), full((1, 128)), full((1, 128)),
        full((3 * C, 3 * C)), full((NHD, C, 2 * C)), full((2 * 128, NHD * 128)),
    ]
    args = [p, p, p, p, sm, conv_qkv, arow, dtb, ng, consts["cum3_d"], consts["tri2_d"], consts["e2_d"]]
    if has_init:
        in_specs.append(pl.BlockSpec((1, nhd, DK_D, DV_D), lambda s: (s, 0, 0, 0)))
        args.append(s0)
    out_specs = [pl.BlockSpec((seq_len, 512), lambda s: (s, 0))]
    out_shape = [jax.ShapeDtypeStruct((n_seq * seq_len, 512), BF16)]
    if want_state:
        out_specs.append(pl.BlockSpec((1, nhd, DK_D, DV_D), lambda s: (s, 0, 0, 0)))
        out_shape.append(jax.ShapeDtypeStruct((n_seq, nhd, DK_D, DV_D), F32))
    res = pl.pallas_call(
        functools.partial(_delta_kernel, seq_len=seq_len, seg=seg, has_init=has_init, want_state=want_state),
        grid=(n_seq,),
        in_specs=in_specs,
        out_specs=out_specs,
        out_shape=out_shape,
        scratch_shapes=[
            pltpu.VMEM((nhd * n_chunks, C, 128), F32),
            pltpu.VMEM((nhd * n_chunks, 2 * C, 128), BF16),
            pltpu.VMEM((nhd * n_chunks, C, C), BF16),
            pltpu.VMEM((nhd * n_chunks, C, 128), BF16),
            pltpu.VMEM((nhd * n_chunks, 8, 128), F32),
            pltpu.VMEM((nhd, DK_D, DV_D), F32),
            pltpu.VMEM((seq_len, 512), F32),
            pltpu.VMEM((seq_len, 512), F32),
        ],
        compiler_params=_cparams(("arbitrary",)),
        name="delta_lat" if has_init else "delta_ctx",
    )(*args)
    return res


CG = CHUNK_G
GB = 64
HP = H_G // 2


def _gla_kernel(*refs, seq_len, has_init, want_state):
    (q_ref, k_ref, v_ref, r_ref, sm_ref, w2_ref, b_ref, ng_ref, cum3_ref, sele_ref) = refs[:10]
    pos = 10
    s0_ref = None
    if has_init:
        s0_ref = refs[pos]
        pos += 1
    o_ref = refs[pos]
    pos += 1
    sfin_ref = None
    if want_state:
        sfin_ref = refs[pos]
        pos += 1
    bc_scr, tot_scr, oacc_scr, ointer_scr = refs[pos:]

    n_blocks = seq_len // GB
    n_chunks = seq_len // CG
    rowi = lax.broadcasted_iota(jnp.int32, (CG, H_G * DK_G), 0)
    half = CG // 2
    rowh = lax.broadcasted_iota(jnp.int32, (half, H_G * DK_G), 0)
    zero_half = jnp.zeros((GB // CG, half, H_G * DK_G), F32)
    qscale = DK_G ** -0.5

    def pairs(x, w):
        return jnp.stack([x[:, p * w:(p + 1) * w] for p in range(HP)], axis=0)

    sm16 = sm_ref[...].astype(BF16)
    for d in range(2):
        logits = _dot(sm16, w2_ref[d]) + b_ref[d]
        la = -_softplus(-logits) * (1.0 / GLA_TAU)
        for blk in range(n_blocks):
            cs = _dot(cum3_ref[...], _split3_rows(la[blk * GB:(blk + 1) * GB]))
            bc_scr[d, blk * GB:(blk + 1) * GB, :] = cs[d * GB:(d + 1) * GB]
            tot_scr[d, blk * GB:(blk + 1) * GB, :] = cs[2 * GB:3 * GB]

    def block_body(b, carry):
        r0 = pl.multiple_of(b * GB, GB)
        q = q_ref[pl.ds(r0, GB), :].astype(F32) * qscale
        k = k_ref[pl.ds(r0, GB), :].astype(F32)
        v = v_ref[pl.ds(r0, GB), :].astype(F32)
        nc = GB // CG
        q3, k3, v3 = (t.reshape(nc, CG, t.shape[-1]) for t in (q, k, v))
        o_blk = None
        for d in range(2):
            bc3 = bc_scr[d, pl.ds(r0, GB), :].reshape(nc, CG, H_G * DK_G)
            pieces = []
            for j in range(CG):
                if d == 0:
                    rs = slice(half if j >= half else 0, CG)
                    keep = (rowh >= j - half) if j >= half else (rowi >= j)
                else:
                    rs = slice(0, half if j < half else CG)
                    keep = (rowh <= j) if j < half else (rowi <= j)
                e = jnp.exp(jnp.minimum(bc3[:, rs] - bc3[:, j:j + 1], 0.0))
                a = jnp.where(keep, q3[:, rs] * e * k3[:, j:j + 1], 0.0)
                if rs.stop - rs.start < CG:
                    a = jnp.concatenate([zero_half, a] if rs.start else [a, zero_half], axis=1)
                pieces.append(a.astype(BF16))
            a_all = jnp.concatenate(pieces, axis=1)
            rr = _dot(a_all.reshape(nc * CG * CG, a_all.shape[-1]), sele_ref[...])
            rr = rr.reshape(nc, CG * CG, rr.shape[-1])
            top = [j for j in range(CG) if d == 1 or j < half]
            bot = [j for j in range(CG) if d == 0 or j >= half]
            acc_t = functools.reduce(jnp.add, [rr[:, j * CG:j * CG + half] * v3[:, j:j + 1] for j in top])
            acc_b = functools.reduce(jnp.add, [rr[:, j * CG + half:(j + 1) * CG] * v3[:, j:j + 1] for j in bot])
            acc = jnp.concatenate([acc_t, acc_b], axis=1)
            o_blk = acc if d == 0 else o_blk + acc
        oacc_scr[pl.ds(r0, GB), :] = o_blk.reshape(GB, o_blk.shape[-1])
        return carry

    lax.fori_loop(0, n_blocks, block_body, 0)

    if has_init:
        st0 = (s0_ref[0, 0:HP], s0_ref[0, HP:2 * HP])
    else:
        st0 = (jnp.zeros((HP, 2 * DV_G, 2 * DK_G), F32),) * 2
    prow = lax.broadcasted_iota(jnp.int32, (2 * DV_G, 2 * DK_G), 0) // DV_G
    pcol = lax.broadcasted_iota(jnp.int32, (2 * DV_G, 2 * DK_G), 1) // DK_G
    pmask = jnp.where(prow == pcol, 1.0, 0.0).astype(F32)

    lane_head = lax.broadcasted_iota(jnp.int32, (1, 2 * DK_G), 1) // DK_G
    head_mask = [jnp.where(lane_head == a, 1.0, 0.0).astype(F32) for a in range(2)]
    nc = GB // CG
    bnt = lambda a, b: lax.dot_general(a, b, (((2,), (2,)), ((0,), (0,))), preferred_element_type=F32)

    nk = GB - CG
    krow = [lax.broadcasted_iota(jnp.int32, (nk, H_G * DK_G), 0) + off for off in (0, CG)]

    def scan_body(i, carry):
        new, inter, r0s, qts, kts, vks = [], [], [], [], [], []
        for d in range(2):
            st = carry[d]
            blk = i if d == 0 else n_blocks - 1 - i
            r0 = pl.multiple_of(blk * GB, GB)
            bcum = bc_scr[d, pl.ds(r0, GB), :]
            tot = tot_scr[d, pl.ds(r0, GB), :]
            q = q_ref[pl.ds(r0, GB), :].astype(F32) * qscale
            k = k_ref[pl.ds(r0, GB), :].astype(F32)
            v16 = v_ref[pl.ds(r0, GB), :]
            qd = pairs(q * jnp.exp(bcum), 2 * DK_G).astype(BF16)
            kd = pairs(k * jnp.exp(tot - bcum), 2 * DK_G).astype(BF16)
            o = bnt(qd, st.astype(BF16))
            inter.append(jnp.concatenate([o[p] for p in range(HP)], axis=1))
            r0s.append(r0)
            upd = _bdot_tn(pairs(v16, 2 * DV_G), kd)
            new.append(st * jnp.exp(pairs(tot, 2 * DK_G)[:, 0:1, :]) + upd * pmask)
            kwin = slice(0, nk) if d == 0 else slice(CG, GB)
            for c in (range(1, nc) if d == 0 else range(nc - 1)):
                rows = slice(c * CG, (c + 1) * CG)
                ref_row = c * CG - 1 if d == 0 else (c + 1) * CG
                valid = (krow[0] < c * CG) if d == 0 else (krow[1] >= (c + 1) * CG)
                bref = bcum[ref_row:ref_row + 1, :]
                qt = q[rows] * jnp.exp(bcum[rows] - bref)
                kt = jnp.where(valid, k[kwin] * jnp.exp(jnp.minimum(bref - bcum[kwin], 0.0)), 0.0)
                qts.append(pairs(qt, 2 * DK_G))
                kts.append(pairs(kt, 2 * DK_G).astype(BF16))
                vks.append(pairs(v16[kwin], 2 * DV_G))
        cat = lambda xs: jnp.concatenate(xs, axis=0)
        qt = cat(qts)
        qq = jnp.concatenate([qt * head_mask[0], qt * head_mask[1]], axis=1).astype(BF16)
        attn = bnt(qq, cat(kts)).astype(BF16)
        ov = _bdot(attn, cat(vks))
        for d in range(2):
            out_rows = []
            for c in range(nc):
                rows = slice(c * CG, (c + 1) * CG)
                ci = c - 1 if d == 0 else c
                if ci < 0 or ci >= nc - 1:
                    out_rows.append(inter[d][rows])
                    continue
                base = (d * (nc - 1) + ci) * HP
                cross = jnp.concatenate([ov[base + p, a * CG:(a + 1) * CG, a * DV_G:(a + 1) * DV_G]
                                         for p in range(HP) for a in range(2)], axis=1)
                out_rows.append(inter[d][rows] + cross)
            ointer_scr[d, pl.ds(r0s[d], GB), :] = jnp.concatenate(out_rows, axis=0)
        return tuple(new)

    st_fin = lax.fori_loop(0, n_blocks, scan_body, st0)

    if want_state:
        sfin_ref[0, 0:HP] = st_fin[0]
        sfin_ref[0, HP:2 * HP] = st_fin[1]

    def out_body(b, carry):
        r0 = pl.multiple_of(b * GB, GB)
        for h in range(H_G):
            lanes = slice(h * DV_G, (h + 1) * DV_G)
            o = oacc_scr[pl.ds(r0, GB), lanes] + ointer_scr[0, pl.ds(r0, GB), lanes] + ointer_scr[1, pl.ds(r0, GB), lanes]
            o = o * lax.rsqrt(jnp.mean(o * o, axis=-1, keepdims=True) + RMS_EPS) * ng_ref[...]
            rg = r_ref[pl.ds(r0, GB), h * DV_G:(h + 1) * DV_G].astype(F32)
            o_ref[pl.ds(r0, GB), h * DV_G:(h + 1) * DV_G] = (o * _silu(rg)).astype(BF16)
        return carry

    lax.fori_loop(0, n_blocks, out_body, 0)


def _gla(p, sm, consts, w2p, gb, ng, s0, *, seq_len, n_seq, row_blk0, want_state):
    has_init = s0 is not None
    full = lambda shape: pl.BlockSpec(shape, lambda s: (0,) * len(shape))
    hk, hv = H_G * DK_G, H_G * DV_G
    in_specs = [
        pl.BlockSpec((seq_len, hk), lambda s: (s + row_blk0, COL_GQ // hk)),
        pl.BlockSpec((seq_len, hk), lambda s: (s + row_blk0, COL_GQ // hk + 1)),
        pl.BlockSpec((seq_len, hv), lambda s: (s + row_blk0, (COL_GQ + 2 * hk) // hv)),
        pl.BlockSpec((seq_len, hv), lambda s: (s + row_blk0, (COL_GQ + 2 * hk) // hv + 1)),
        pl.BlockSpec((seq_len, 128), lambda s: (s + row_blk0, 0)),
        full((2, 128, hk)), full((2, 1, hk)), full((1, DV_G)),
        full((3 * GB, 3 * GB)), full((CG * CG, hv)),
    ]
    args = [p, p, p, p, sm, w2p, gb, ng, consts["cum3_d"], consts["sele"]]
    if has_init:
        in_specs.append(pl.BlockSpec((1, 2 * HP, 2 * DV_G, 2 * DK_G), lambda s: (s, 0, 0, 0)))
        args.append(s0)
    out_specs = [pl.BlockSpec((seq_len, hv), lambda s: (s, 0))]
    out_shape = [jax.ShapeDtypeStruct((n_seq * seq_len, hv), BF16)]
    if want_state:
        out_specs.append(pl.BlockSpec((1, 2 * HP, 2 * DV_G, 2 * DK_G), lambda s: (s, 0, 0, 0)))
        out_shape.append(jax.ShapeDtypeStruct((n_seq, 2 * HP, 2 * DV_G, 2 * DK_G), F32))
    return pl.pallas_call(
        functools.partial(_gla_kernel, seq_len=seq_len, has_init=has_init, want_state=want_state),
        grid=(n_seq,),
        in_specs=in_specs,
        out_specs=out_specs,
        out_shape=out_shape,
        scratch_shapes=[
            pltpu.VMEM((2, seq_len, hk), F32),
            pltpu.VMEM((2, seq_len, hk), F32),
            pltpu.VMEM((seq_len, hv), F32),
            pltpu.VMEM((2, seq_len, hv), F32),
        ],
        compiler_params=_cparams(("arbitrary",)),
        name="gla_lat" if has_init else "gla_ctx",
    )(*args)


def _constants():
    i = np.arange(C)
    lo = (i[:, None] >= i[None, :]).astype(np.float32)
    up = (i[:, None] <= i[None, :]).astype(np.float32)
    ones = np.ones((C, C), np.float32)
    cum_d = np.concatenate([lo, up, ones], axis=0)
    hk, hv = H_G * DK_G, H_G * DV_G
    sele = (np.arange(hk)[:, None] // DK_G == np.arange(hv)[None, :] // DV_G).astype(np.float32)
    lane_sel = (np.arange(128)[:, None] == np.arange(NHD * 128)[None, :] // 128).astype(np.float32)
    return {
        "e2_d": jnp.asarray(np.tile(lane_sel, (2, 1)), BF16),
        "cum3_d": jnp.asarray(np.tile(cum_d, (1, 3)), BF16),
        "tri2_d": jnp.asarray(np.stack([np.tile(lo, (1, 2))] * H_D + [np.tile(up, (1, 2))] * H_D), BF16),
        "sele": jnp.asarray(sele, BF16),
    }


def _permute_w_in(w_in_l):
    widths = (W_A, W_A, W_A, 512, 512, 512, 512, 8, 8, 256, 256, 512, 512, 32, 3 * D_MODEL)
    offs = np.concatenate([[0], np.cumsum(widths)])
    seg = lambda a, b: w_in_l[:, offs[a]:offs[b]]
    zeros = lambda n: jnp.zeros((D_MODEL, n), w_in_l.dtype)
    narrow = [seg(7, 8), zeros(SM_LR - 8), seg(13, 14), zeros(SM_A - SM_LR - 2 * GLA_RANK),
              seg(8, 9), zeros(SM_W - SM_A - 8)]
    return jnp.concatenate([seg(0, 3), seg(3, 7), seg(9, 13), seg(14, 15)] + narrow, axis=1).astype(BF16)


def _gla_pack_state(s):
    n = s.shape[0]
    st = jnp.swapaxes(s, -1, -2).reshape(n, 2, HP, 2, DV_G, DK_G)
    packed = jnp.einsum("ndpavk,ab->ndpavbk", st, jnp.eye(2, dtype=s.dtype))
    return packed.reshape(n, 2 * HP, 2 * DV_G, 2 * DK_G)


def _gla_unpack_state(sp):
    n = sp.shape[0]
    s6 = sp.reshape(n, 2, HP, 2, DV_G, 2, DK_G)
    diag = jnp.stack([s6[:, :, :, a, :, a, :] for a in range(2)], axis=3)
    return jnp.swapaxes(diag.reshape(n, 2, H_G, DV_G, DK_G), -1, -2)


def _lane_row(vals8, lane0):
    return jnp.zeros((1, 128), F32).at[0, lane0:lane0 + 8].set(vals8.reshape(8).astype(F32))


def kernel(x_prompt, x_sample, state_delta, state_gla, c, c_ctx, w_ada, b_ada, ln_g, ln_b, ffn_w1, ffn_w2, w_in,
           conv_a, conv_qkv, delta_a_log, delta_dt_bias, delta_norm_g, gla_w2, gla_b, gla_norm_g,
           w_br_a, w_br_d, w_br_g, w_o):
    n_ctx, ctx_len, _ = x_prompt.shape
    n_lat, lat_len, _ = x_sample.shape
    n_ctx_tok = n_ctx * ctx_len
    n_lat_tok = n_lat * lat_len
    assert ctx_len == CTX_SEG and MERGE_TM % CTX_SEG == 0 and n_ctx_tok % MERGE_TM == 0
    assert lat_len % MERGE_TM == 0 and n_ctx_tok % lat_len == 0 and n_lat <= 8

    consts = _constants()
    x = (x_prompt.reshape(n_ctx_tok, D_MODEL), x_sample.reshape(n_lat_tok, D_MODEL))
    cond16 = jnp.zeros((16, D_MODEL), F32).at[:n_lat].set(c).at[8].set(c_ctx)
    ada = _ada_table(cond16, w_ada, b_ada)

    sds, sgs = [], []
    for l in range(DEPTH):
        ada_l = ada[l]
        kw = dict(n_ctx_tok=n_ctx_tok, lat_len=lat_len)
        x = _ffn(x, ada_l, ffn_w1, ffn_w2, ln_g[l, 0], ln_b[l, 0], l, 0, **kw)
        p, sm = _inproj(x, ada_l, _permute_w_in(w_in[l]), **kw)

        arow = _lane_row(delta_a_log[l], SM_A % 128)
        dtb = _lane_row(delta_dt_bias[l], SM_A % 128)
        ngd = delta_norm_g[l].reshape(1, DV_D)
        od_ctx, sd = _delta(p, sm, consts, conv_qkv[l], arow, dtb, ngd, None, seq_len=ctx_len, n_seq=n_ctx,
                            row_blk0=0, seg=ctx_len, want_state=True)
        s0d = state_delta[:, l].astype(F32).reshape(n_lat, 2 * H_D, DK_D, DV_D)
        (od_lat,) = _delta(p, sm, consts, conv_qkv[l], arow, dtb, ngd, s0d, seq_len=lat_len, n_seq=n_lat,
                           row_blk0=n_ctx_tok // lat_len, seg=GRID_W, want_state=False)
        sds.append(sd.reshape(n_ctx, 2, H_D, DK_D, DV_D))

        w2p = jnp.zeros((2, 128, H_G * DK_G), F32)
        for d in range(2):
            w2p = w2p.at[d, SM_LR + d * GLA_RANK:SM_LR + (d + 1) * GLA_RANK].set(gla_w2[l, d])
        w2p = w2p.astype(BF16)
        gb = gla_b[l].reshape(2, 1, H_G * DK_G)
        ngg = gla_norm_g[l].reshape(1, DV_G)
        og_ctx, sg = _gla(p, sm, consts, w2p, gb, ngg, None, seq_len=ctx_len, n_seq=n_ctx, row_blk0=0,
                          want_state=True)
        s0g = _gla_pack_state(state_gla[:, l].astype(F32))
        (og_lat,) = _gla(p, sm, consts, w2p, gb, ngg, s0g, seq_len=lat_len, n_seq=n_lat,
                         row_blk0=n_ctx_tok // lat_len, want_state=False)
        sgs.append(_gla_unpack_state(sg))

        x = _merge(p, od_ctx, od_lat, og_ctx, og_lat, x, ada_l, conv_a[l], w_br_a[l].astype(BF16), w_br_d[l].astype(BF16),
                   w_br_g[l].astype(BF16), w_o[l].astype(BF16), ln_g[l, 1], ln_b[l, 1], **kw)
        x = _ffn(x, ada_l, ffn_w1, ffn_w2, ln_g[l, 2], ln_b[l, 2], l, 2, split_out=(l == DEPTH - 1), **kw)

    y_prompt = x[0].reshape(n_ctx, ctx_len, D_MODEL)
    y_sample = x[1].reshape(n_lat, lat_len, D_MODEL)
    new_state_delta = jnp.stack(sds, axis=1).astype(x_prompt.dtype)
    new_state_gla = jnp.stack(sgs, axis=1).astype(x_prompt.dtype)
    return (y_prompt, y_sample, new_state_delta, new_state_gla)
```

```python
import functools

import jax
import jax.numpy as jnp
import numpy as np
from jax import lax
from jax.experimental import pallas as pl
from jax.experimental.pallas import tpu as pltpu

F32 = jnp.float32
BF16 = jnp.bfloat16

D_MODEL = 1024
DEPTH = 2
GRID_W = 64
D_FF = 2816
W_A = 512
H_D, DK_D, DV_D, CHUNK_D = 4, 128, 128, 64
H_G, DK_G, DV_G, CHUNK_G = 4, 64, 128, 16
GLA_RANK = 16
GLA_TAU = 16.0
N_ADA = 9
ALPHA = float((2 * DEPTH) ** 0.25)
LN_EPS = 1e-5
RMS_EPS = 1e-6
LOG2_E = 1.4426950408889634

COL_A = 0
COL_DQ = 1536
COL_GQ = 3584
COL_MG = 5120
COL_SM = 8192
SM_W = 256
D_PROJ_PAD = COL_SM + SM_W
SM_BETA, SM_LR, SM_A = 0, 16, 128

VMEM_LIMIT = 56 * 1024 * 1024


def _cparams(sem):
    return pltpu.CompilerParams(dimension_semantics=sem, vmem_limit_bytes=VMEM_LIMIT)


def _dot(a, b):
    return jnp.dot(a, b, preferred_element_type=F32)


def _dot_nt(a, b):
    return lax.dot_general(a, b, (((1,), (1,)), ((), ())), preferred_element_type=F32)


def _dot_tn(a, b):
    return lax.dot_general(a, b, (((0,), (0,)), ((), ())), preferred_element_type=F32)


def _sigmoid(x):
    return 1.0 / (1.0 + jnp.exp(-x))


def _silu(x):
    return x * _sigmoid(x)


def _softplus(x):
    return jnp.maximum(x, 0.0) + jnp.log1p(jnp.exp(-jnp.abs(x)))


def _split2(x):
    hi = x.astype(BF16)
    lo = (x - hi.astype(F32)).astype(BF16)
    return hi, lo


def _split3_rows(x):
    hi = x.astype(BF16)
    r = x - hi.astype(F32)
    mid = r.astype(BF16)
    lo = (r - mid.astype(F32)).astype(BF16)
    return jnp.concatenate([hi, mid, lo], axis=0)


def _mm3(a, b):
    ah, al = _split2(a)
    bh, bl = _split2(b)
    n = a.shape[0]
    p = _dot(jnp.concatenate([ah, al], axis=0), bh)
    return p[:n] + p[n:] + _dot(ah, bl)


def _mm1(a, b):
    return _dot(a.astype(BF16), b.astype(BF16))


def _layer_norm(y, g, b):
    mu = jnp.mean(y, axis=-1, keepdims=True)
    yc = y - mu
    var = jnp.mean(yc * yc, axis=-1, keepdims=True)
    return yc * lax.rsqrt(var + LN_EPS) * g + b


def _ada_kernel(cond_ref, w_ref, b_ref, o_ref):
    s = _silu(cond_ref[...]).astype(BF16)
    o_ref[0] = _dot(s, w_ref[0].astype(BF16)) + b_ref[0]


def _ada_table(cond16, w_ada, b_ada):
    n_l = w_ada.shape[0]
    tn = 1024
    out = pl.pallas_call(
        _ada_kernel,
        grid=(n_l, N_ADA * D_MODEL // tn),
        in_specs=[
            pl.BlockSpec((16, D_MODEL), lambda l, j: (0, 0)),
            pl.BlockSpec((1, D_MODEL, tn), lambda l, j: (l, 0, j)),
            pl.BlockSpec((1, 1, tn), lambda l, j: (l, 0, j)),
        ],
        out_specs=pl.BlockSpec((1, 16, tn), lambda l, j: (l, 0, j)),
        out_shape=jax.ShapeDtypeStruct((n_l, 16, N_ADA * D_MODEL), F32),
        compiler_params=_cparams(("arbitrary", "arbitrary")),
        name="ada",
    )(cond16, w_ada, b_ada.reshape(n_l, 1, N_ADA * D_MODEL))
    return out.reshape(n_l, 16, N_ADA, D_MODEL)


def _cond_row(tok0, n_ctx_tok, lat_len):
    return jnp.where(tok0 < n_ctx_tok, 8, (tok0 - n_ctx_tok) // lat_len)


MXU_TILE = 256
FF_SPLIT = (D_FF // MXU_TILE // 2) * MXU_TILE
FF_CHUNKS = ((0, FF_SPLIT), (FF_SPLIT, D_FF))


W1_ROWS = 128
W2_ROWS = 256


def _stage_weight(w_hbm, dst, stage, sem, rows):
    n_chunks = w_hbm.shape[0] // rows

    def copy(c):
        return pltpu.make_async_copy(w_hbm.at[pl.ds(c * rows, rows), :], stage.at[c % 2], sem.at[c % 2])

    copy(0).start()
    for c in range(n_chunks):
        if c + 1 < n_chunks:
            copy(c + 1).start()
        copy(c).wait()
        dst[c * rows:(c + 1) * rows, :] = stage[c % 2].astype(BF16)


def _ffn_kernel(*refs, l, j, n, n_ctx_tiles, split_in, split_out):
    n_x = 4 if split_in else 2
    x_refs = refs[:n_x]
    adac_ref, adap_ref, w1_hbm, w2_hbm, lng_ref, lnb_ref = refs[n_x:n_x + 6]
    n_o = 2 if split_out else 1
    o_refs = refs[n_x + 6:n_x + 6 + n_o]
    y_scr, w1_ref, w2_ref, st1, st2, sem1, sem2 = refs[n_x + 6 + n_o:]
    jj = j // 2
    i = pl.program_id(0)
    prev_is_ctx = jnp.maximum(i - 1, 0) < n_ctx_tiles

    @pl.when(i == 0)
    def _():
        y_scr[...] = jnp.zeros_like(y_scr)
        _stage_weight(w1_hbm.at[l, jj], w1_ref, st1, sem1, W1_ROWS)
        _stage_weight(w2_hbm.at[l, jj], w2_ref, st2, sem2, W2_ROWS)

    if split_in:
        x_cur = jnp.where(jnp.minimum(i, n - 1) < n_ctx_tiles, x_refs[0][...], x_refs[1][...])
        x_prev = jnp.where(prev_is_ctx, x_refs[2][...], x_refs[3][...])
    else:
        x_cur, x_prev = x_refs[0][...], x_refs[1][...]

    h = (x_cur * (1.0 + adac_ref[3 * j + 1:3 * j + 2, :]) + adac_ref[3 * j:3 * j + 1, :]).astype(BF16)
    y_new = None
    for c0, c1 in FF_CHUNKS:
        g = _dot(h, w1_ref[:, c0:c1])
        u = _dot(h, w1_ref[:, D_FF + c0:D_FF + c1])
        part = _dot((_silu(g) * u).astype(BF16), w2_ref[c0:c1, :])
        y_new = part if y_new is None else y_new + part

    y = ALPHA * x_prev + 0.5 * adap_ref[3 * j + 2:3 * j + 3, :] * y_scr[...]
    out = _layer_norm(y, lng_ref[...], lnb_ref[...])
    if split_out:
        @pl.when(prev_is_ctx)
        def _():
            o_refs[0][...] = out

        @pl.when(jnp.logical_not(prev_is_ctx))
        def _():
            o_refs[1][...] = out
    else:
        o_refs[0][...] = out
    y_scr[...] = y_new


def _ffn(x, ada_l, ffn_w1, ffn_w2, lng, lnb, l, j, n_ctx_tok, lat_len, tm=512, split_out=False):
    split_in = isinstance(x, tuple)
    n_ctx_tiles = n_ctx_tok // tm
    t = n_ctx_tok + x[1].shape[0] if split_in else x.shape[0]
    n = t // tm
    cur = lambda i: jnp.minimum(i, n - 1)
    prev = lambda i: jnp.maximum(i - 1, 0)
    ctx_blk = lambda f: (lambda i: (jnp.minimum(f(i), n_ctx_tiles - 1), 0))
    lat_blk = lambda f: (lambda i: (jnp.maximum(f(i) - n_ctx_tiles, 0), 0))
    cond = lambda i: _cond_row(i * tm, n_ctx_tok, lat_len)
    resident = lambda shape: pl.BlockSpec(shape, lambda i: (0,) * len(shape), pipeline_mode=pl.Buffered(1))
    tile = lambda index_map: pl.BlockSpec((tm, D_MODEL), index_map)
    if split_in:
        x_specs = [tile(ctx_blk(cur)), tile(lat_blk(cur)), tile(ctx_blk(prev)), tile(lat_blk(prev))]
        x_args = [x[0], x[1], x[0], x[1]]
    else:
        x_specs = [tile(lambda i: (cur(i), 0)), tile(lambda i: (prev(i), 0))]
        x_args = [x, x]
    if split_out:
        out_specs = [tile(ctx_blk(prev)), tile(lat_blk(prev))]
        out_shape = [jax.ShapeDtypeStruct((n_ctx_tok, D_MODEL), F32), jax.ShapeDtypeStruct((t - n_ctx_tok, D_MODEL), F32)]
    else:
        out_specs = tile(lambda i: (prev(i), 0))
        out_shape = jax.ShapeDtypeStruct((t, D_MODEL), F32)
    return pl.pallas_call(
        functools.partial(_ffn_kernel, l=l, j=j, n=n, n_ctx_tiles=n_ctx_tiles, split_in=split_in, split_out=split_out),
        grid=(n + 1,),
        in_specs=x_specs + [
            pl.BlockSpec((None, N_ADA, D_MODEL), lambda i: (cond(cur(i)), 0, 0)),
            pl.BlockSpec((None, N_ADA, D_MODEL), lambda i: (cond(prev(i)), 0, 0)),
            pl.BlockSpec(memory_space=pl.ANY),
            pl.BlockSpec(memory_space=pl.ANY),
            resident((1, D_MODEL)),
            resident((1, D_MODEL)),
        ],
        out_specs=out_specs,
        out_shape=out_shape,
        scratch_shapes=[
            pltpu.VMEM((tm, D_MODEL), F32),
            pltpu.VMEM((D_MODEL, 2 * D_FF), BF16),
            pltpu.VMEM((D_FF, D_MODEL), BF16),
            pltpu.VMEM((2, W1_ROWS, 2 * D_FF), F32),
            pltpu.VMEM((2, W2_ROWS, D_MODEL), F32),
            pltpu.SemaphoreType.DMA((2,)),
            pltpu.SemaphoreType.DMA((2,)),
        ],
        compiler_params=_cparams(("arbitrary",)),
        name="ffn",
    )(*x_args, ada_l, ada_l, ffn_w1, ffn_w2, lng.reshape(1, D_MODEL), lnb.reshape(1, D_MODEL))


INPROJ_NC = 8 * MXU_TILE


def _inproj_kernel(x_ref, ada_ref, w_ref, p_ref, sm_ref):
    h = (x_ref[...] * (1.0 + ada_ref[4:5, :]) + ada_ref[3:4, :]).astype(BF16)
    for c0 in range(0, COL_SM, INPROJ_NC):
        p_ref[:, c0:c0 + INPROJ_NC] = _dot(h, w_ref[:, c0:c0 + INPROJ_NC]).astype(BF16)
    sm_ref[...] = _dot(h, w_ref[:, COL_SM:D_PROJ_PAD])


def _inproj(x, ada_l, w_in_p, n_ctx_tok, lat_len, tm=512):
    t = x.shape[0]
    cond = lambda i: _cond_row(i * tm, n_ctx_tok, lat_len)
    return pl.pallas_call(
        _inproj_kernel,
        grid=(t // tm,),
        in_specs=[
            pl.BlockSpec((tm, D_MODEL), lambda i: (i, 0)),
            pl.BlockSpec((None, N_ADA, D_MODEL), lambda i: (cond(i), 0, 0)),
            pl.BlockSpec((D_MODEL, D_PROJ_PAD), lambda i: (0, 0), pipeline_mode=pl.Buffered(1)),
        ],
        out_specs=[pl.BlockSpec((tm, COL_SM), lambda i: (i, 0)),
                   pl.BlockSpec((tm, D_PROJ_PAD - COL_SM), lambda i: (i, 0))],
        out_shape=[jax.ShapeDtypeStruct((t, COL_SM), BF16),
                   jax.ShapeDtypeStruct((t, D_PROJ_PAD - COL_SM), F32)],
        compiler_params=_cparams(("arbitrary",)),
        name="inproj",
    )(x, ada_l, w_in_p)


MERGE_TM = 512
CTX_SEG = 256


def _merge_kernel(a_ref, m0_ref, m1_ref, m2_ref, odc_ref, odl_ref, ogc_ref, ogl_ref, x_ref, ada_ref, cw_ref,
                  wa_ref, wd_ref, wg_ref, wo_ref, lng_ref, lnb_ref, o_ref, *, n_ctx_tiles):
    i = pl.program_id(0)
    is_ctx = i < n_ctx_tiles
    o_d = jnp.where(is_ctx, odc_ref[...], odl_ref[...])
    o_g = jnp.where(is_ctx, ogc_ref[...], ogl_ref[...])
    seg = jnp.where(i < n_ctx_tiles, CTX_SEG, GRID_W)
    row = lax.broadcasted_iota(jnp.int32, (MERGE_TM, W_A), 0)
    pos = jnp.bitwise_and(row, seg - 1)
    a_x = a_ref[:, 0:W_A].astype(F32)
    a_b = a_ref[:, W_A:2 * W_A].astype(F32)
    a_c = a_ref[:, 2 * W_A:3 * W_A].astype(F32)
    z = a_c * a_x
    z_prev = jnp.where(pos == 0, 0.0, pltpu.roll(z, 1, 0))
    z_next = jnp.where(pos == seg - 1, 0.0, pltpu.roll(z, MERGE_TM - 1, 0))
    y_a = a_b * (cw_ref[0:1, :] * z_prev + cw_ref[1:2, :] * z + cw_ref[2:3, :] * z_next)
    br_a = _dot(y_a.astype(BF16), wa_ref[...])
    br_d = _dot(o_d, wd_ref[...])
    br_g = _dot(o_g, wg_ref[...])
    gate = lambda m_ref: _sigmoid(m_ref[...].astype(F32))
    merged = gate(m0_ref) * br_a + gate(m1_ref) * br_d + gate(m2_ref) * br_g
    y = _dot(merged.astype(BF16), wo_ref[...])
    y = ALPHA * x_ref[...] + ada_ref[5:6, :] * y
    o_ref[...] = _layer_norm(y, lng_ref[...], lnb_ref[...])


def _merge(p, od_ctx, od_lat, og_ctx, og_lat, x, ada_l, conv_a, wa, wd, wg, wo, lng, lnb, n_ctx_tok, lat_len):
    t = x.shape[0]
    tm = MERGE_TM
    n_ctx_tiles = n_ctx_tok // tm
    cond = lambda i: _cond_row(i * tm, n_ctx_tok, lat_len)
    full = lambda shape: pl.BlockSpec(shape, lambda i: (0,) * len(shape))
    ctx_blk = lambda i: (jnp.minimum(i, n_ctx_tiles - 1), 0)
    lat_blk = lambda i: (jnp.maximum(i - n_ctx_tiles, 0), 0)
    mg0 = COL_MG // D_MODEL
    return pl.pallas_call(
        functools.partial(_merge_kernel, n_ctx_tiles=n_ctx_tiles),
        grid=(t // tm,),
        in_specs=[
            pl.BlockSpec((tm, 3 * W_A), lambda i: (i, 0)),
            pl.BlockSpec((tm, D_MODEL), lambda i: (i, mg0)),
            pl.BlockSpec((tm, D_MODEL), lambda i: (i, mg0 + 1)),
            pl.BlockSpec((tm, D_MODEL), lambda i: (i, mg0 + 2)),
            pl.BlockSpec((tm, 512), ctx_blk), pl.BlockSpec((tm, 512), lat_blk),
            pl.BlockSpec((tm, 512), ctx_blk), pl.BlockSpec((tm, 512), lat_blk),
            pl.BlockSpec((tm, D_MODEL), lambda i: (i, 0)),
            pl.BlockSpec((None, N_ADA, D_MODEL), lambda i: (cond(i), 0, 0)),
            full((3, W_A)),
            full((W_A, D_MODEL)), full((512, D_MODEL)), full((512, D_MODEL)), full((D_MODEL, D_MODEL)),
            full((1, D_MODEL)), full((1, D_MODEL)),
        ],
        out_specs=pl.BlockSpec((tm, D_MODEL), lambda i: (i, 0)),
        out_shape=jax.ShapeDtypeStruct((t, D_MODEL), F32),
        compiler_params=_cparams(("arbitrary",)),
        name="merge",
    )(p, p, p, p, od_ctx, od_lat, og_ctx, og_lat, x, ada_l, conv_a, wa, wd, wg, wo,
      lng.reshape(1, D_MODEL), lnb.reshape(1, D_MODEL))


C = CHUNK_D


def _bdot(a, b):
    return lax.dot_general(a, b, (((2,), (1,)), ((0,), (0,))), preferred_element_type=F32)


def _bdot_tn(a, b):
    return lax.dot_general(a, b, (((1,), (1,)), ((0,), (0,))), preferred_element_type=F32)


def _tri_inverse(m, eye, row, col):
    def same_block(shift):
        return jnp.right_shift(row, shift) == jnp.right_shift(col, shift)

    m8 = jnp.where(same_block(3), m, 0.0)
    m16 = m8.astype(BF16)
    x = eye - m8
    sq = _bdot(m16, m16).astype(BF16)
    x = x + _bdot(x.astype(BF16), sq)
    sq = _bdot(sq, sq).astype(BF16)
    x = x + _bdot(x.astype(BF16), sq)
    for shift in (4, 5, 6):
        e = jnp.where(same_block(shift) & jnp.logical_not(same_block(shift - 1)), m, 0.0)
        x16 = x.astype(BF16)
        x = x - _bdot(_bdot(x16, e.astype(BF16)).astype(BF16), x16)
    return x


NHD = 2 * H_D
CB = 4
HALO = 16


def _delta_kernel(*refs, seq_len, seg, has_init, want_state):
    (q_ref, k_ref, v_ref, z_ref, sm_ref, cw_ref, arow_ref, dtb_ref, ng_ref, cum3_ref, tri2_ref, e2_ref) = refs[:12]
    pos = 12
    s0_ref = None
    if has_init:
        s0_ref = refs[pos]
        pos += 1
    o_ref = refs[pos]
    pos += 1
    sfin_ref = None
    if want_state:
        sfin_ref = refs[pos]
        pos += 1
    u_scr, wq_scr, at_scr, kd_scr, ls_scr, s_scr, of_scr, ob_scr = refs[pos:]

    n_chunks = seq_len // C
    nb = CB * NHD
    row = lax.broadcasted_iota(jnp.int32, (nb, C, C), 1)
    col = lax.broadcasted_iota(jnp.int32, (nb, C, C), 2)
    fwd = jnp.bitwise_and(lax.broadcasted_iota(jnp.int32, (nb, C, C), 0), NHD - 1) < H_D
    dist = jnp.where(fwd, row - col, col - row)
    strict = dist > 0
    incl = dist >= 0
    eye = jnp.where(row == col, 1.0, 0.0).astype(F32)
    row128 = lax.broadcasted_iota(jnp.int32, (C, 128), 0)

    def conv_block(ref, c0, wc0, r0, n):
        x = ref[pl.ds(r0, C), c0:c0 + 128].astype(F32)
        xp = pltpu.roll(x, 1, 0)
        xn = pltpu.roll(x, C - 1, 0)
        if seg == C:
            xp = jnp.where(row128 == 0, 0.0, xp)
            xn = jnp.where(row128 == C - 1, 0.0, xn)
        else:
            prev = ref[pl.ds(pl.multiple_of(jnp.maximum(r0 - HALO, 0), HALO), HALO), c0:c0 + 128].astype(F32)
            nxt = ref[pl.ds(pl.multiple_of(jnp.minimum(r0 + C, seq_len - HALO), HALO), HALO), c0:c0 + 128].astype(F32)
            pm = jnp.where(n > 0, 1.0, 0.0)
            nm = jnp.where(n < n_chunks - 1, 1.0, 0.0)
            xp = jnp.where(row128 == 0, prev[HALO - 1:HALO, :] * pm, xp)
            xn = jnp.where(row128 == C - 1, nxt[0:1, :] * nm, xn)
        y = cw_ref[0:1, wc0:wc0 + 128] * xp + cw_ref[1:2, wc0:wc0 + 128] * x + cw_ref[2:3, wc0:wc0 + 128] * xn
        return _silu(y)

    def l2n(x):
        return x * lax.rsqrt(jnp.sum(x * x, axis=-1, keepdims=True) + RMS_EPS)

    def chunk_body(it, carry):
        qs, ks, vs, kks, qks, betas, gcols, gsums, gtots = [], [], [], [], [], [], [], [], []
        for cc in range(CB):
            n = it * CB + cc
            r0 = pl.multiple_of(n * C, C)
            beta_full = _sigmoid(sm_ref[pl.ds(r0, C), 0:128])
            g_full = -jnp.exp(arow_ref[...]) * _softplus(sm_ref[pl.ds(r0, C), 128:256] + dtb_ref[...])
            cs = _dot(cum3_ref[...], _split3_rows(g_full))
            cols = jnp.concatenate([g_full, beta_full, cs], axis=0)
            chi, clo = _split2(cols)
            bcast = _dot(jnp.concatenate([chi, clo], axis=1), e2_ref[...])
            qh, kh, vh, kkh, qkh = [], [], [], [], []
            for h in range(H_D):
                q = l2n(conv_block(q_ref, h * 128, h * 128, r0, n)) * (DK_D ** -0.5)
                k = l2n(conv_block(k_ref, h * 128, 512 + h * 128, r0, n))
                v = conv_block(v_ref, h * 128, 1024 + h * 128, r0, n)
                k16 = k.astype(BF16)
                qh.append(q)
                kh.append(k)
                vh.append(v)
                kkh.append(_dot_nt(k16, k16))
                qkh.append(_dot_nt(q.astype(BF16), k16))
            for dst, src in ((qs, qh), (ks, kh), (vs, vh), (kks, kkh), (qks, qkh)):
                dst.extend(src + src)
            for b in range(NHD):
                lanes = slice(b * 128, (b + 1) * 128)
                d = b // H_D
                gcols.append(bcast[0:C, lanes])
                betas.append(bcast[C:2 * C, lanes])
                gsums.append(bcast[(2 + d) * C:(3 + d) * C, lanes])
                gtots.append(bcast[4 * C:5 * C, lanes])
        st = lambda xs: jnp.stack(xs, axis=0)
        q, k, v, kk, qk = st(qs), st(ks), st(vs), st(kks), st(qks)
        beta, gcol, gsum, gtot = st(betas), st(gcols), st(gsums), st(gtots)
        xg = jnp.where(strict, gcol[:, :, 0:C], 0.0)
        xh, xl = _split2(xg)
        tri2 = jnp.concatenate([tri2_ref[...]] * CB, axis=0)
        diff = _bdot(tri2, jnp.concatenate([xh, xl], axis=1))
        gamma = jnp.where(incl, jnp.exp(jnp.minimum(diff, 0.0)), 0.0)
        m = jnp.where(strict, beta[:, :, 0:C] * kk * gamma, 0.0)
        eg = jnp.exp(gsum)
        rhs = jnp.concatenate([v * beta, k * (beta * eg)], axis=2)
        sol = _bdot(_tri_inverse(m, eye, row, col).astype(BF16), rhs.astype(BF16))
        wq = jnp.concatenate([sol[:, :, 128:], q * eg], axis=1).astype(BF16)
        at = (qk * gamma).astype(BF16)
        kd = (k * jnp.exp(gtot - gsum)).astype(BF16)
        ls = jnp.exp(gtot[:, 0:8, :])
        for cc in range(CB):
            n = it * CB + cc
            for d in range(2):
                t = n if d == 0 else n_chunks - 1 - n
                dst = pl.ds(t * NHD + d * H_D, H_D)
                src = slice(cc * NHD + d * H_D, cc * NHD + (d + 1) * H_D)
                u_scr[dst] = sol[src, :, :128]
                wq_scr[dst] = wq[src]
                at_scr[dst] = at[src]
                kd_scr[dst] = kd[src]
                ls_scr[dst] = ls[src]
        return carry

    lax.fori_loop(0, n_chunks // CB, chunk_body, 0)

    if has_init:
        s_scr[...] = s0_ref[0]
    else:
        s_scr[...] = jnp.zeros_like(s_scr)

    def scan_body(i, carry):
        slot = pl.ds(i * NHD, NHD)
        s = s_scr[...]
        r = _bdot(wq_scr[slot], s.astype(BF16))
        v16 = (u_scr[slot] - r[:, 0:C]).astype(BF16)
        o = r[:, C:2 * C] + _bdot(at_scr[slot], v16)
        s_scr[...] = s * ls_scr[slot][:, 0:1, :] + _bdot_tn(kd_scr[slot], v16)
        for d in range(2):
            n = i if d == 0 else n_chunks - 1 - i
            r0 = pl.multiple_of(n * C, C)
            o_dst = of_scr if d == 0 else ob_scr
            for h in range(H_D):
                o_dst[pl.ds(r0, C), h * 128:(h + 1) * 128] = o[d * H_D + h]
        return carry

    lax.fori_loop(0, n_chunks, scan_body, 0)

    if want_state:
        sfin_ref[0] = s_scr[...]

    def out_body(n, carry):
        r0 = pl.multiple_of(n * C, C)
        for h in range(H_D):
            o = of_scr[pl.ds(r0, C), h * 128:(h + 1) * 128] + ob_scr[pl.ds(r0, C), h * 128:(h + 1) * 128]
            o = o * lax.rsqrt(jnp.mean(o * o, axis=-1, keepdims=True) + RMS_EPS) * ng_ref[...]
            zg = z_ref[pl.ds(r0, C), h * 128:(h + 1) * 128].astype(F32)
            o_ref[pl.ds(r0, C), h * 128:(h + 1) * 128] = (o * _silu(zg)).astype(BF16)
        return carry

    lax.fori_loop(0, n_chunks, out_body, 0)


def _delta(p, sm, consts, conv_qkv, arow, dtb, ng, s0, *, seq_len, n_seq, row_blk0, seg, want_state):
    has_init = s0 is not None
    n_chunks = seq_len // C
    nhd = 2 * H_D
    cq = COL_DQ // 512
    full = lambda shape: pl.BlockSpec(shape, lambda s: (0,) * len(shape))
    in_specs = [
        pl.BlockSpec((seq_len, 512), lambda s: (s + row_blk0, cq)),
        pl.BlockSpec((seq_len, 512), lambda s: (s + row_blk0, cq + 1)),
        pl.BlockSpec((seq_len, 512), lambda s: (s + row_blk0, cq + 2)),
        pl.BlockSpec((seq_len, 512), lambda s: (s + row_blk0, cq + 3)),
        pl.BlockSpec((seq_len, SM_W), lambda s: (s + row_blk0, 0)),
        full((3, 1536)), full((1, 128)), full((1, 128)), full((1, 128)),
        full((3 * C, 3 * C)), full((NHD, C, 2 * C)), full((2 * 128, NHD * 128)),
    ]
    args = [p, p, p, p, sm, conv_qkv, arow, dtb, ng, consts["cum3_d"], consts["tri2_d"], consts["e2_d"]]
    if has_init:
        in_specs.append(pl.BlockSpec((1, nhd, DK_D, DV_D), lambda s: (s, 0, 0, 0)))
        args.append(s0)
    out_specs = [pl.BlockSpec((seq_len, 512), lambda s: (s, 0))]
    out_shape = [jax.ShapeDtypeStruct((n_seq * seq_len, 512), BF16)]
    if want_state:
        out_specs.append(pl.BlockSpec((1, nhd, DK_D, DV_D), lambda s: (s, 0, 0, 0)))
        out_shape.append(jax.ShapeDtypeStruct((n_seq, nhd, DK_D, DV_D), F32))
    res = pl.pallas_call(
        functools.partial(_delta_kernel, seq_len=seq_len, seg=seg, has_init=has_init, want_state=want_state),
        grid=(n_seq,),
        in_specs=in_specs,
        out_specs=out_specs,
        out_shape=out_shape,
        scratch_shapes=[
            pltpu.VMEM((nhd * n_chunks, C, 128), F32),
            pltpu.VMEM((nhd * n_chunks, 2 * C, 128), BF16),
            pltpu.VMEM((nhd * n_chunks, C, C), BF16),
            pltpu.VMEM((nhd * n_chunks, C, 128), BF16),
            pltpu.VMEM((nhd * n_chunks, 8, 128), F32),
            pltpu.VMEM((nhd, DK_D, DV_D), F32),
            pltpu.VMEM((seq_len, 512), F32),
            pltpu.VMEM((seq_len, 512), F32),
        ],
        compiler_params=_cparams(("arbitrary",)),
        name="delta_lat" if has_init else "delta_ctx",
    )(*args)
    return res


CG = CHUNK_G
GB = 64
HP = H_G // 2


def _gla_kernel(*refs, seq_len, has_init, want_state):
    (q_ref, k_ref, v_ref, r_ref, sm_ref, w2_ref, b_ref, ng_ref, cum3_ref, sele_ref) = refs[:10]
    pos = 10
    s0_ref = None
    if has_init:
        s0_ref = refs[pos]
        pos += 1
    o_ref = refs[pos]
    pos += 1
    sfin_ref = None
    if want_state:
        sfin_ref = refs[pos]
        pos += 1
    bc_scr, tot_scr, oacc_scr, ointer_scr = refs[pos:]

    n_blocks = seq_len // GB
    n_chunks = seq_len // CG
    rowi = lax.broadcasted_iota(jnp.int32, (CG, H_G * DK_G), 0)
    half = CG // 2
    rowh = lax.broadcasted_iota(jnp.int32, (half, H_G * DK_G), 0)
    zero_half = jnp.zeros((GB // CG, half, H_G * DK_G), F32)
    qscale = DK_G ** -0.5

    def pairs(x, w):
        return jnp.stack([x[:, p * w:(p + 1) * w] for p in range(HP)], axis=0)

    sm16 = sm_ref[...].astype(BF16)
    for d in range(2):
        logits = _dot(sm16, w2_ref[d]) + b_ref[d]
        la = -_softplus(-logits) * (LOG2_E / GLA_TAU)
        for blk in range(n_blocks):
            cs = _dot(cum3_ref[...], _split3_rows(la[blk * GB:(blk + 1) * GB]))
            bc_scr[d, blk * GB:(blk + 1) * GB, :] = cs[d * GB:(d + 1) * GB]
            tot_scr[d, blk * GB:(blk + 1) * GB, :] = cs[2 * GB:3 * GB]

    def block_body(b, carry):
        r0 = pl.multiple_of(b * GB, GB)
        q = q_ref[pl.ds(r0, GB), :].astype(F32) * qscale
        k = k_ref[pl.ds(r0, GB), :].astype(F32)
        v = v_ref[pl.ds(r0, GB), :].astype(F32)
        nc = GB // CG
        q3, k3, v3 = (t.reshape(nc, CG, t.shape[-1]) for t in (q, k, v))
        o_blk = None
        for d in range(2):
            bc3 = bc_scr[d, pl.ds(r0, GB), :].reshape(nc, CG, H_G * DK_G)
            pieces = []
            for j in range(CG):
                if d == 0:
                    rs = slice(half if j >= half else 0, CG)
                    keep = (rowh >= j - half) if j >= half else (rowi >= j)
                else:
                    rs = slice(0, half if j < half else CG)
                    keep = (rowh <= j) if j < half else (rowi <= j)
                e = jnp.exp2(jnp.minimum(bc3[:, rs] - bc3[:, j:j + 1], 0.0))
                a = jnp.where(keep, q3[:, rs] * e * k3[:, j:j + 1], 0.0)
                if rs.stop - rs.start < CG:
                    a = jnp.concatenate([zero_half, a] if rs.start else [a, zero_half], axis=1)
                pieces.append(a.astype(BF16))
            a_all = jnp.concatenate(pieces, axis=1)
            rr = _dot(a_all.reshape(nc * CG * CG, a_all.shape[-1]), sele_ref[...])
            rr = rr.reshape(nc, CG * CG, rr.shape[-1])
            top = [j for j in range(CG) if d == 1 or j < half]
            bot = [j for j in range(CG) if d == 0 or j >= half]
            acc_t = functools.reduce(jnp.add, [rr[:, j * CG:j * CG + half] * v3[:, j:j + 1] for j in top])
            acc_b = functools.reduce(jnp.add, [rr[:, j * CG + half:(j + 1) * CG] * v3[:, j:j + 1] for j in bot])
            acc = jnp.concatenate([acc_t, acc_b], axis=1)
            o_blk = acc if d == 0 else o_blk + acc
        oacc_scr[pl.ds(r0, GB), :] = o_blk.reshape(GB, o_blk.shape[-1])
        return carry

    lax.fori_loop(0, n_blocks, block_body, 0)

    if has_init:
        st0 = (s0_ref[0, 0:HP], s0_ref[0, HP:2 * HP])
    else:
        st0 = (jnp.zeros((HP, 2 * DV_G, 2 * DK_G), F32),) * 2
    prow = lax.broadcasted_iota(jnp.int32, (2 * DV_G, 2 * DK_G), 0) // DV_G
    pcol = lax.broadcasted_iota(jnp.int32, (2 * DV_G, 2 * DK_G), 1) // DK_G
    pmask = jnp.where(prow == pcol, 1.0, 0.0).astype(F32)

    lane_head = lax.broadcasted_iota(jnp.int32, (1, 2 * DK_G), 1) // DK_G
    head_mask = [jnp.where(lane_head == a, 1.0, 0.0).astype(F32) for a in range(2)]
    nc = GB // CG
    bnt = lambda a, b: lax.dot_general(a, b, (((2,), (2,)), ((0,), (0,))), preferred_element_type=F32)

    nk = GB - CG
    krow = [lax.broadcasted_iota(jnp.int32, (nk, H_G * DK_G), 0) + off for off in (0, CG)]

    def scan_body(i, carry):
        new, inter, r0s, qts, kts, vks = [], [], [], [], [], []
        for d in range(2):
            st = carry[d]
            blk = i if d == 0 else n_blocks - 1 - i
            r0 = pl.multiple_of(blk * GB, GB)
            bcum = bc_scr[d, pl.ds(r0, GB), :]
            tot = tot_scr[d, pl.ds(r0, GB), :]
            q = q_ref[pl.ds(r0, GB), :].astype(F32) * qscale
            k = k_ref[pl.ds(r0, GB), :].astype(F32)
            v16 = v_ref[pl.ds(r0, GB), :]
            qd = pairs(q * jnp.exp2(bcum), 2 * DK_G).astype(BF16)
            kd = pairs(k * jnp.exp2(tot - bcum), 2 * DK_G).astype(BF16)
            o = bnt(qd, st.astype(BF16))
            inter.append(jnp.concatenate([o[p] for p in range(HP)], axis=1))
            r0s.append(r0)
            upd = _bdot_tn(pairs(v16, 2 * DV_G), kd)
            new.append(st * jnp.exp2(pairs(tot, 2 * DK_G)[:, 0:1, :]) + upd * pmask)
            kwin = slice(0, nk) if d == 0 else slice(CG, GB)
            for c in (range(1, nc) if d == 0 else range(nc - 1)):
                rows = slice(c * CG, (c + 1) * CG)
                ref_row = c * CG - 1 if d == 0 else (c + 1) * CG
                valid = (krow[0] < c * CG) if d == 0 else (krow[1] >= (c + 1) * CG)
                bref = bcum[ref_row:ref_row + 1, :]
                qt = q[rows] * jnp.exp2(bcum[rows] - bref)
                kt = jnp.where(valid, k[kwin] * jnp.exp2(jnp.minimum(bref - bcum[kwin], 0.0)), 0.0)
                qts.append(pairs(qt, 2 * DK_G))
                kts.append(pairs(kt, 2 * DK_G).astype(BF16))
                vks.append(pairs(v16[kwin], 2 * DV_G))
        cat = lambda xs: jnp.concatenate(xs, axis=0)
        qt = cat(qts)
        qq = jnp.concatenate([qt * head_mask[0], qt * head_mask[1]], axis=1).astype(BF16)
        attn = bnt(qq, cat(kts)).astype(BF16)
        ov = _bdot(attn, cat(vks))
        for d in range(2):
            out_rows = []
            for c in range(nc):
                rows = slice(c * CG, (c + 1) * CG)
                ci = c - 1 if d == 0 else c
                if ci < 0 or ci >= nc - 1:
                    out_rows.append(inter[d][rows])
                    continue
                base = (d * (nc - 1) + ci) * HP
                cross = jnp.concatenate([ov[base + p, a * CG:(a + 1) * CG, a * DV_G:(a + 1) * DV_G]
                                         for p in range(HP) for a in range(2)], axis=1)
                out_rows.append(inter[d][rows] + cross)
            ointer_scr[d, pl.ds(r0s[d], GB), :] = jnp.concatenate(out_rows, axis=0)
        return tuple(new)

    st_fin = lax.fori_loop(0, n_blocks, scan_body, st0)

    if want_state:
        for d in range(2):
            for p in range(HP):
                t = st_fin[d][p].T
                for a in range(2):
                    sfin_ref[0, d * H_G + 2 * p + a] = t[a * DK_G:(a + 1) * DK_G, a * DV_G:(a + 1) * DV_G]

    def out_body(b, carry):
        r0 = pl.multiple_of(b * GB, GB)
        for h in range(H_G):
            lanes = slice(h * DV_G, (h + 1) * DV_G)
            o = oacc_scr[pl.ds(r0, GB), lanes] + ointer_scr[0, pl.ds(r0, GB), lanes] + ointer_scr[1, pl.ds(r0, GB), lanes]
            o = o * lax.rsqrt(jnp.mean(o * o, axis=-1, keepdims=True) + RMS_EPS) * ng_ref[...]
            rg = r_ref[pl.ds(r0, GB), h * DV_G:(h + 1) * DV_G].astype(F32)
            o_ref[pl.ds(r0, GB), h * DV_G:(h + 1) * DV_G] = (o * _silu(rg)).astype(BF16)
        return carry

    lax.fori_loop(0, n_blocks, out_body, 0)


def _gla(p, sm, consts, w2p, gb, ng, s0, *, seq_len, n_seq, row_blk0, want_state):
    has_init = s0 is not None
    full = lambda shape: pl.BlockSpec(shape, lambda s: (0,) * len(shape))
    hk, hv = H_G * DK_G, H_G * DV_G
    in_specs = [
        pl.BlockSpec((seq_len, hk), lambda s: (s + row_blk0, COL_GQ // hk)),
        pl.BlockSpec((seq_len, hk), lambda s: (s + row_blk0, COL_GQ // hk + 1)),
        pl.BlockSpec((seq_len, hv), lambda s: (s + row_blk0, (COL_GQ + 2 * hk) // hv)),
        pl.BlockSpec((seq_len, hv), lambda s: (s + row_blk0, (COL_GQ + 2 * hk) // hv + 1)),
        pl.BlockSpec((seq_len, 128), lambda s: (s + row_blk0, 0)),
        full((2, 128, hk)), full((2, 1, hk)), full((1, DV_G)),
        full((3 * GB, 3 * GB)), full((CG * CG, hv)),
    ]
    args = [p, p, p, p, sm, w2p, gb, ng, consts["cum3_d"], consts["sele"]]
    if has_init:
        in_specs.append(pl.BlockSpec((1, 2 * HP, 2 * DV_G, 2 * DK_G), lambda s: (s, 0, 0, 0)))
        args.append(s0)
    out_specs = [pl.BlockSpec((seq_len, hv), lambda s: (s, 0))]
    out_shape = [jax.ShapeDtypeStruct((n_seq * seq_len, hv), BF16)]
    if want_state:
        out_specs.append(pl.BlockSpec((1, 2 * H_G, DK_G, DV_G), lambda s: (s, 0, 0, 0)))
        out_shape.append(jax.ShapeDtypeStruct((n_seq, 2 * H_G, DK_G, DV_G), F32))
    return pl.pallas_call(
        functools.partial(_gla_kernel, seq_len=seq_len, has_init=has_init, want_state=want_state),
        grid=(n_seq,),
        in_specs=in_specs,
        out_specs=out_specs,
        out_shape=out_shape,
        scratch_shapes=[
            pltpu.VMEM((2, seq_len, hk), F32),
            pltpu.VMEM((2, seq_len, hk), F32),
            pltpu.VMEM((seq_len, hv), F32),
            pltpu.VMEM((2, seq_len, hv), F32),
        ],
        compiler_params=_cparams(("arbitrary",)),
        name="gla_lat" if has_init else "gla_ctx",
    )(*args)


def _constants():
    i = np.arange(C)
    lo = (i[:, None] >= i[None, :]).astype(np.float32)
    up = (i[:, None] <= i[None, :]).astype(np.float32)
    ones = np.ones((C, C), np.float32)
    cum_d = np.concatenate([lo, up, ones], axis=0)
    hk, hv = H_G * DK_G, H_G * DV_G
    sele = (np.arange(hk)[:, None] // DK_G == np.arange(hv)[None, :] // DV_G).astype(np.float32)
    lane_sel = (np.arange(128)[:, None] == np.arange(NHD * 128)[None, :] // 128).astype(np.float32)
    return {
        "e2_d": jnp.asarray(np.tile(lane_sel, (2, 1)), BF16),
        "cum3_d": jnp.asarray(np.tile(cum_d, (1, 3)), BF16),
        "tri2_d": jnp.asarray(np.stack([np.tile(lo, (1, 2))] * H_D + [np.tile(up, (1, 2))] * H_D), BF16),
        "sele": jnp.asarray(sele, BF16),
    }


def _permute_w_in(w_in_l):
    widths = (W_A, W_A, W_A, 512, 512, 512, 512, 8, 8, 256, 256, 512, 512, 32, 3 * D_MODEL)
    offs = np.concatenate([[0], np.cumsum(widths)])
    seg = lambda a, b: w_in_l[:, offs[a]:offs[b]]
    zeros = lambda n: jnp.zeros((D_MODEL, n), w_in_l.dtype)
    narrow = [seg(7, 8), zeros(SM_LR - 8), seg(13, 14), zeros(SM_A - SM_LR - 2 * GLA_RANK),
              seg(8, 9), zeros(SM_W - SM_A - 8)]
    return jnp.concatenate([seg(0, 3), seg(3, 7), seg(9, 13), seg(14, 15)] + narrow, axis=1).astype(BF16)


def _gla_pack_state(s):
    n = s.shape[0]
    st = jnp.swapaxes(s, -1, -2).reshape(n, 2, HP, 2, DV_G, DK_G)
    packed = jnp.einsum("ndpavk,ab->ndpavbk", st, jnp.eye(2, dtype=s.dtype))
    return packed.reshape(n, 2 * HP, 2 * DV_G, 2 * DK_G)


def _lane_row(vals8, lane0):
    return jnp.zeros((1, 128), F32).at[0, lane0:lane0 + 8].set(vals8.reshape(8).astype(F32))


def kernel(x_prompt, x_sample, state_delta, state_gla, c, c_ctx, w_ada, b_ada, ln_g, ln_b, ffn_w1, ffn_w2, w_in,
           conv_a, conv_qkv, delta_a_log, delta_dt_bias, delta_norm_g, gla_w2, gla_b, gla_norm_g,
           w_br_a, w_br_d, w_br_g, w_o):
    n_ctx, ctx_len, _ = x_prompt.shape
    n_lat, lat_len, _ = x_sample.shape
    n_ctx_tok = n_ctx * ctx_len
    n_lat_tok = n_lat * lat_len
    assert ctx_len == CTX_SEG and MERGE_TM % CTX_SEG == 0 and n_ctx_tok % MERGE_TM == 0
    assert lat_len % MERGE_TM == 0 and n_ctx_tok % lat_len == 0 and n_lat <= 8

    consts = _constants()
    x = (x_prompt.reshape(n_ctx_tok, D_MODEL), x_sample.reshape(n_lat_tok, D_MODEL))
    cond16 = jnp.zeros((16, D_MODEL), F32).at[:n_lat].set(c).at[8].set(c_ctx)
    ada = _ada_table(cond16, w_ada, b_ada)

    sds, sgs = [], []
    for l in range(DEPTH):
        ada_l = ada[l]
        kw = dict(n_ctx_tok=n_ctx_tok, lat_len=lat_len)
        x = _ffn(x, ada_l, ffn_w1, ffn_w2, ln_g[l, 0], ln_b[l, 0], l, 0, **kw)
        p, sm = _inproj(x, ada_l, _permute_w_in(w_in[l]), **kw)

        arow = _lane_row(delta_a_log[l], SM_A % 128)
        dtb = _lane_row(delta_dt_bias[l], SM_A % 128)
        ngd = delta_norm_g[l].reshape(1, DV_D)
        od_ctx, sd = _delta(p, sm, consts, conv_qkv[l], arow, dtb, ngd, None, seq_len=ctx_len, n_seq=n_ctx,
                            row_blk0=0, seg=ctx_len, want_state=True)
        s0d = state_delta[:, l].astype(F32).reshape(n_lat, 2 * H_D, DK_D, DV_D)
        (od_lat,) = _delta(p, sm, consts, conv_qkv[l], arow, dtb, ngd, s0d, seq_len=lat_len, n_seq=n_lat,
                           row_blk0=n_ctx_tok // lat_len, seg=GRID_W, want_state=False)
        sds.append(sd.reshape(n_ctx, 2, H_D, DK_D, DV_D))

        w2p = jnp.zeros((2, 128, H_G * DK_G), F32)
        for d in range(2):
            w2p = w2p.at[d, SM_LR + d * GLA_RANK:SM_LR + (d + 1) * GLA_RANK].set(gla_w2[l, d])
        w2p = w2p.astype(BF16)
        gb = gla_b[l].reshape(2, 1, H_G * DK_G)
        ngg = gla_norm_g[l].reshape(1, DV_G)
        og_ctx, sg = _gla(p, sm, consts, w2p, gb, ngg, None, seq_len=ctx_len, n_seq=n_ctx, row_blk0=0,
                          want_state=True)
        s0g = _gla_pack_state(state_gla[:, l].astype(F32))
        (og_lat,) = _gla(p, sm, consts, w2p, gb, ngg, s0g, seq_len=lat_len, n_seq=n_lat,
                         row_blk0=n_ctx_tok // lat_len, want_state=False)
        sgs.append(sg.reshape(n_ctx, 2, H_G, DK_G, DV_G))

        x = _merge(p, od_ctx, od_lat, og_ctx, og_lat, x, ada_l, conv_a[l], w_br_a[l].astype(BF16), w_br_d[l].astype(BF16),
                   w_br_g[l].astype(BF16), w_o[l].astype(BF16), ln_g[l, 1], ln_b[l, 1], **kw)
        x = _ffn(x, ada_l, ffn_w1, ffn_w2, ln_g[l, 2], ln_b[l, 2], l, 2, split_out=(l == DEPTH - 1), **kw)

    y_prompt = x[0].reshape(n_ctx, ctx_len, D_MODEL)
    y_sample = x[1].reshape(n_lat, lat_len, D_MODEL)
    new_state_delta = jnp.stack(sds, axis=1).astype(x_prompt.dtype)
    new_state_gla = jnp.stack(sgs, axis=1).astype(x_prompt.dtype)
    return (y_prompt, y_sample, new_state_delta, new_state_gla)
```

```python
import functools

import jax
import jax.numpy as jnp
import numpy as np
from jax import lax
from jax.experimental import pallas as pl
from jax.experimental.pallas import tpu as pltpu

F32 = jnp.float32
BF16 = jnp.bfloat16

D_MODEL = 1024
DEPTH = 2
GRID_W = 64
D_FF = 2816
W_A = 512
H_D, DK_D, DV_D, CHUNK_D = 4, 128, 128, 64
H_G, DK_G, DV_G, CHUNK_G = 4, 64, 128, 16
GLA_RANK = 16
GLA_TAU = 16.0
N_ADA = 9
ALPHA = float((2 * DEPTH) ** 0.25)
LN_EPS = 1e-5
RMS_EPS = 1e-6
LOG2_E = 1.4426950408889634

COL_A = 0
COL_DQ = 1536
COL_GQ = 3584
COL_MG = 5120
COL_SM = 8192
SM_W = 256
D_PROJ_PAD = COL_SM + SM_W
SM_BETA, SM_LR, SM_A = 0, 16, 128

VMEM_LIMIT = 56 * 1024 * 1024


def _cparams(sem):
    return pltpu.CompilerParams(dimension_semantics=sem, vmem_limit_bytes=VMEM_LIMIT)


def _dot(a, b):
    return jnp.dot(a, b, preferred_element_type=F32)


def _dot_nt(a, b):
    return lax.dot_general(a, b, (((1,), (1,)), ((), ())), preferred_element_type=F32)


def _sigmoid(x):
    return 1.0 / (1.0 + jnp.exp(-x))


def _silu(x):
    return x * _sigmoid(x)


def _softplus(x):
    return jnp.maximum(x, 0.0) + jnp.log1p(jnp.exp(-jnp.abs(x)))


def _split2(x):
    hi = x.astype(BF16)
    lo = (x - hi.astype(F32)).astype(BF16)
    return hi, lo


def _split3_rows(x):
    hi = x.astype(BF16)
    r = x - hi.astype(F32)
    mid = r.astype(BF16)
    lo = (r - mid.astype(F32)).astype(BF16)
    return jnp.concatenate([hi, mid, lo], axis=0)


def _layer_norm(y, g, b):
    mu = jnp.mean(y, axis=-1, keepdims=True)
    yc = y - mu
    var = jnp.mean(yc * yc, axis=-1, keepdims=True)
    return yc * lax.rsqrt(var + LN_EPS) * g + b


def _ada_kernel(cond_ref, w_ref, b_ref, o_ref):
    s = _silu(cond_ref[...]).astype(BF16)
    o_ref[0] = _dot(s, w_ref[0].astype(BF16)) + b_ref[0]


def _ada_table(cond16, w_ada, b_ada):
    n_l = w_ada.shape[0]
    tn = N_ADA * D_MODEL // 4
    out = pl.pallas_call(
        _ada_kernel,
        grid=(n_l, N_ADA * D_MODEL // tn),
        in_specs=[
            pl.BlockSpec((16, D_MODEL), lambda l, j: (0, 0)),
            pl.BlockSpec((1, D_MODEL, tn), lambda l, j: (l, 0, j)),
            pl.BlockSpec((1, 1, tn), lambda l, j: (l, 0, j)),
        ],
        out_specs=pl.BlockSpec((1, 16, tn), lambda l, j: (l, 0, j)),
        out_shape=jax.ShapeDtypeStruct((n_l, 16, N_ADA * D_MODEL), F32),
        compiler_params=_cparams(("arbitrary", "arbitrary")),
        name="ada",
    )(cond16, w_ada, b_ada.reshape(n_l, 1, N_ADA * D_MODEL))
    return out.reshape(n_l, 16, N_ADA, D_MODEL)


def _cond_row(tok0, n_ctx_tok, lat_len):
    return jnp.where(tok0 < n_ctx_tok, 8, (tok0 - n_ctx_tok) // lat_len)


MXU_TILE = 256
FF_SPLIT = (D_FF // MXU_TILE // 2) * MXU_TILE
FF_CHUNKS = ((0, FF_SPLIT), (FF_SPLIT, D_FF))


W1_ROWS = 128
W2_ROWS = 256


def _stage_weight(w_hbm, dst, stage, sem, rows):
    n_chunks = w_hbm.shape[0] // rows

    def copy(c):
        return pltpu.make_async_copy(w_hbm.at[pl.ds(c * rows, rows), :], stage.at[c % 2], sem.at[c % 2])

    copy(0).start()
    for c in range(n_chunks):
        if c + 1 < n_chunks:
            copy(c + 1).start()
        copy(c).wait()
        dst[c * rows:(c + 1) * rows, :] = stage[c % 2].astype(BF16)


def _ffn_kernel(*refs, l, j, n, n_ctx_tiles, split_in, split_out):
    n_x = 4 if split_in else 2
    x_refs = refs[:n_x]
    adac_ref, adap_ref, w1_hbm, w2_hbm, lng_ref, lnb_ref = refs[n_x:n_x + 6]
    n_o = 2 if split_out else 1
    o_refs = refs[n_x + 6:n_x + 6 + n_o]
    y_scr, w1_ref, w2_ref, st1, st2, sem1, sem2 = refs[n_x + 6 + n_o:]
    jj = j // 2
    i = pl.program_id(0)
    prev_is_ctx = jnp.maximum(i - 1, 0) < n_ctx_tiles

    @pl.when(i == 0)
    def _():
        y_scr[...] = jnp.zeros_like(y_scr)
        _stage_weight(w1_hbm.at[l, jj], w1_ref, st1, sem1, W1_ROWS)
        _stage_weight(w2_hbm.at[l, jj], w2_ref, st2, sem2, W2_ROWS)

    if split_in:
        x_cur = jnp.where(jnp.minimum(i, n - 1) < n_ctx_tiles, x_refs[0][...], x_refs[1][...])
        x_prev = jnp.where(prev_is_ctx, x_refs[2][...], x_refs[3][...])
    else:
        x_cur, x_prev = x_refs[0][...], x_refs[1][...]

    h = (x_cur * (1.0 + adac_ref[3 * j + 1:3 * j + 2, :]) + adac_ref[3 * j:3 * j + 1, :]).astype(BF16)
    y_new = None
    for c0, c1 in FF_CHUNKS:
        g = _dot(h, w1_ref[:, c0:c1])
        u = _dot(h, w1_ref[:, D_FF + c0:D_FF + c1])
        part = _dot((_silu(g) * u).astype(BF16), w2_ref[c0:c1, :])
        y_new = part if y_new is None else y_new + part

    y = ALPHA * x_prev + 0.5 * adap_ref[3 * j + 2:3 * j + 3, :] * y_scr[...]
    out = _layer_norm(y, lng_ref[...], lnb_ref[...])
    if split_out:
        @pl.when(prev_is_ctx)
        def _():
            o_refs[0][...] = out

        @pl.when(jnp.logical_not(prev_is_ctx))
        def _():
            o_refs[1][...] = out
    else:
        o_refs[0][...] = out
    y_scr[...] = y_new


def _ffn(x, ada_l, ffn_w1, ffn_w2, lng, lnb, l, j, n_ctx_tok, lat_len, tm=512, split_out=False):
    split_in = isinstance(x, tuple)
    n_ctx_tiles = n_ctx_tok // tm
    t = n_ctx_tok + x[1].shape[0] if split_in else x.shape[0]
    n = t // tm
    cur = lambda i: jnp.minimum(i, n - 1)
    prev = lambda i: jnp.maximum(i - 1, 0)
    ctx_blk = lambda f: (lambda i: (jnp.minimum(f(i), n_ctx_tiles - 1), 0))
    lat_blk = lambda f: (lambda i: (jnp.maximum(f(i) - n_ctx_tiles, 0), 0))
    cond = lambda i: _cond_row(i * tm, n_ctx_tok, lat_len)
    resident = lambda shape: pl.BlockSpec(shape, lambda i: (0,) * len(shape), pipeline_mode=pl.Buffered(1))
    tile = lambda index_map: pl.BlockSpec((tm, D_MODEL), index_map)
    if split_in:
        x_specs = [tile(ctx_blk(cur)), tile(lat_blk(cur)), tile(ctx_blk(prev)), tile(lat_blk(prev))]
        x_args = [x[0], x[1], x[0], x[1]]
    else:
        x_specs = [tile(lambda i: (cur(i), 0)), tile(lambda i: (prev(i), 0))]
        x_args = [x, x]
    if split_out:
        out_specs = [tile(ctx_blk(prev)), tile(lat_blk(prev))]
        out_shape = [jax.ShapeDtypeStruct((n_ctx_tok, D_MODEL), F32), jax.ShapeDtypeStruct((t - n_ctx_tok, D_MODEL), F32)]
    else:
        out_specs = tile(lambda i: (prev(i), 0))
        out_shape = jax.ShapeDtypeStruct((t, D_MODEL), F32)
    return pl.pallas_call(
        functools.partial(_ffn_kernel, l=l, j=j, n=n, n_ctx_tiles=n_ctx_tiles, split_in=split_in, split_out=split_out),
        grid=(n + 1,),
        in_specs=x_specs + [
            pl.BlockSpec((None, N_ADA, D_MODEL), lambda i: (cond(cur(i)), 0, 0)),
            pl.BlockSpec((None, N_ADA, D_MODEL), lambda i: (cond(prev(i)), 0, 0)),
            pl.BlockSpec(memory_space=pl.ANY),
            pl.BlockSpec(memory_space=pl.ANY),
            resident((1, D_MODEL)),
            resident((1, D_MODEL)),
        ],
        out_specs=out_specs,
        out_shape=out_shape,
        scratch_shapes=[
            pltpu.VMEM((tm, D_MODEL), F32),
            pltpu.VMEM((D_MODEL, 2 * D_FF), BF16),
            pltpu.VMEM((D_FF, D_MODEL), BF16),
            pltpu.VMEM((2, W1_ROWS, 2 * D_FF), F32),
            pltpu.VMEM((2, W2_ROWS, D_MODEL), F32),
            pltpu.SemaphoreType.DMA((2,)),
            pltpu.SemaphoreType.DMA((2,)),
        ],
        compiler_params=_cparams(("arbitrary",)),
        name="ffn",
    )(*x_args, ada_l, ada_l, ffn_w1, ffn_w2, lng.reshape(1, D_MODEL), lnb.reshape(1, D_MODEL))


INPROJ_NC = 8 * MXU_TILE


def _inproj_kernel(x_ref, ada_ref, w_ref, p_ref, sm_ref):
    h = (x_ref[...] * (1.0 + ada_ref[4:5, :]) + ada_ref[3:4, :]).astype(BF16)
    for c0 in range(0, COL_SM, INPROJ_NC):
        p_ref[:, c0:c0 + INPROJ_NC] = _dot(h, w_ref[:, c0:c0 + INPROJ_NC]).astype(BF16)
    sm_ref[...] = _dot(h, w_ref[:, COL_SM:D_PROJ_PAD])


def _inproj(x, ada_l, w_in_p, n_ctx_tok, lat_len, tm=512):
    t = x.shape[0]
    cond = lambda i: _cond_row(i * tm, n_ctx_tok, lat_len)
    return pl.pallas_call(
        _inproj_kernel,
        grid=(t // tm,),
        in_specs=[
            pl.BlockSpec((tm, D_MODEL), lambda i: (i, 0)),
            pl.BlockSpec((None, N_ADA, D_MODEL), lambda i: (cond(i), 0, 0)),
            pl.BlockSpec((D_MODEL, D_PROJ_PAD), lambda i: (0, 0), pipeline_mode=pl.Buffered(1)),
        ],
        out_specs=[pl.BlockSpec((tm, COL_SM), lambda i: (i, 0)),
                   pl.BlockSpec((tm, D_PROJ_PAD - COL_SM), lambda i: (i, 0))],
        out_shape=[jax.ShapeDtypeStruct((t, COL_SM), BF16),
                   jax.ShapeDtypeStruct((t, D_PROJ_PAD - COL_SM), F32)],
        compiler_params=_cparams(("arbitrary",)),
        name="inproj",
    )(x, ada_l, w_in_p)


MERGE_TM = 512
CTX_SEG = 256


def _merge_kernel(a_ref, m0_ref, m1_ref, m2_ref, odc_ref, odl_ref, ogc_ref, ogl_ref, x_ref, ada_ref, cw_ref,
                  wa_ref, wd_ref, wg_ref, wo_ref, lng_ref, lnb_ref, o_ref, *, n_ctx_tiles):
    i = pl.program_id(0)
    is_ctx = i < n_ctx_tiles
    o_d = jnp.where(is_ctx, odc_ref[...], odl_ref[...])
    o_g = jnp.where(is_ctx, ogc_ref[...], ogl_ref[...])
    seg = jnp.where(i < n_ctx_tiles, CTX_SEG, GRID_W)
    row = lax.broadcasted_iota(jnp.int32, (MERGE_TM, W_A), 0)
    pos = jnp.bitwise_and(row, seg - 1)
    a_x = a_ref[:, 0:W_A].astype(F32)
    a_b = a_ref[:, W_A:2 * W_A].astype(F32)
    a_c = a_ref[:, 2 * W_A:3 * W_A].astype(F32)
    z = a_c * a_x
    z_prev = jnp.where(pos == 0, 0.0, pltpu.roll(z, 1, 0))
    z_next = jnp.where(pos == seg - 1, 0.0, pltpu.roll(z, MERGE_TM - 1, 0))
    y_a = a_b * (cw_ref[0:1, :] * z_prev + cw_ref[1:2, :] * z + cw_ref[2:3, :] * z_next)
    br_a = _dot(y_a.astype(BF16), wa_ref[...])
    br_d = _dot(o_d, wd_ref[...])
    br_g = _dot(o_g, wg_ref[...])
    gate = lambda m_ref: _sigmoid(m_ref[...].astype(F32))
    merged = gate(m0_ref) * br_a + gate(m1_ref) * br_d + gate(m2_ref) * br_g
    y = _dot(merged.astype(BF16), wo_ref[...])
    y = ALPHA * x_ref[...] + ada_ref[5:6, :] * y
    o_ref[...] = _layer_norm(y, lng_ref[...], lnb_ref[...])


def _merge(p, od_ctx, od_lat, og_ctx, og_lat, x, ada_l, conv_a, wa, wd, wg, wo, lng, lnb, n_ctx_tok, lat_len):
    t = x.shape[0]
    tm = MERGE_TM
    n_ctx_tiles = n_ctx_tok // tm
    cond = lambda i: _cond_row(i * tm, n_ctx_tok, lat_len)
    full = lambda shape: pl.BlockSpec(shape, lambda i: (0,) * len(shape))
    ctx_blk = lambda i: (jnp.minimum(i, n_ctx_tiles - 1), 0)
    lat_blk = lambda i: (jnp.maximum(i - n_ctx_tiles, 0), 0)
    mg0 = COL_MG // D_MODEL
    return pl.pallas_call(
        functools.partial(_merge_kernel, n_ctx_tiles=n_ctx_tiles),
        grid=(t // tm,),
        in_specs=[
            pl.BlockSpec((tm, 3 * W_A), lambda i: (i, 0)),
            pl.BlockSpec((tm, D_MODEL), lambda i: (i, mg0)),
            pl.BlockSpec((tm, D_MODEL), lambda i: (i, mg0 + 1)),
            pl.BlockSpec((tm, D_MODEL), lambda i: (i, mg0 + 2)),
            pl.BlockSpec((tm, 512), ctx_blk), pl.BlockSpec((tm, 512), lat_blk),
            pl.BlockSpec((tm, 512), ctx_blk), pl.BlockSpec((tm, 512), lat_blk),
            pl.BlockSpec((tm, D_MODEL), lambda i: (i, 0)),
            pl.BlockSpec((None, N_ADA, D_MODEL), lambda i: (cond(i), 0, 0)),
            full((3, W_A)),
            full((W_A, D_MODEL)), full((512, D_MODEL)), full((512, D_MODEL)), full((D_MODEL, D_MODEL)),
            full((1, D_MODEL)), full((1, D_MODEL)),
        ],
        out_specs=pl.BlockSpec((tm, D_MODEL), lambda i: (i, 0)),
        out_shape=jax.ShapeDtypeStruct((t, D_MODEL), F32),
        compiler_params=_cparams(("arbitrary",)),
        name="merge",
    )(p, p, p, p, od_ctx, od_lat, og_ctx, og_lat, x, ada_l, conv_a, wa, wd, wg, wo,
      lng.reshape(1, D_MODEL), lnb.reshape(1, D_MODEL))


C = CHUNK_D


def _bdot(a, b):
    return lax.dot_general(a, b, (((2,), (1,)), ((0,), (0,))), preferred_element_type=F32)


def _bdot_tn(a, b):
    return lax.dot_general(a, b, (((1,), (1,)), ((0,), (0,))), preferred_element_type=F32)


def _tri_inverse(m, eye, row, col):
    def same_block(shift):
        return jnp.right_shift(row, shift) == jnp.right_shift(col, shift)

    m8 = jnp.where(same_block(3), m, 0.0)
    m16 = m8.astype(BF16)
    x = eye - m8
    sq = _bdot(m16, m16).astype(BF16)
    x = x + _bdot(x.astype(BF16), sq)
    sq = _bdot(sq, sq).astype(BF16)
    x = x + _bdot(x.astype(BF16), sq)
    for shift in (4, 5, 6):
        e = jnp.where(same_block(shift) & jnp.logical_not(same_block(shift - 1)), m, 0.0)
        x16 = x.astype(BF16)
        x = x - _bdot(_bdot(x16, e.astype(BF16)).astype(BF16), x16)
    return x


NHD = 2 * H_D
CB_MAX = 4
HALO = 16


def _delta_kernel(*refs, seq_len, seg, has_init, want_state):
    (q_ref, k_ref, v_ref, z_ref, sm_ref, cw_ref, arow_ref, dtb_ref, ng_ref, cum3_ref, tri2_ref, e2_ref) = refs[:12]
    pos = 12
    s0_ref = None
    if has_init:
        s0_ref = refs[pos]
        pos += 1
    o_ref = refs[pos]
    pos += 1
    sfin_ref = None
    if want_state:
        sfin_ref = refs[pos]
        pos += 1
    u_scr, wq_scr, at_scr, kd_scr, ls_scr, s_scr, of_scr, ob_scr = refs[pos:]

    n_chunks = seq_len // C
    cb = min(CB_MAX, n_chunks)
    nb = cb * NHD
    row = lax.broadcasted_iota(jnp.int32, (nb, C, C), 1)
    col = lax.broadcasted_iota(jnp.int32, (nb, C, C), 2)
    fwd = jnp.bitwise_and(lax.broadcasted_iota(jnp.int32, (nb, C, C), 0), NHD - 1) < H_D
    dist = jnp.where(fwd, row - col, col - row)
    strict = dist > 0
    incl = dist >= 0
    eye = jnp.where(row == col, 1.0, 0.0).astype(F32)
    row128 = lax.broadcasted_iota(jnp.int32, (C, 128), 0)

    def conv_block(ref, c0, wc0, r0, n):
        x = ref[pl.ds(r0, C), c0:c0 + 128].astype(F32)
        xp = pltpu.roll(x, 1, 0)
        xn = pltpu.roll(x, C - 1, 0)
        if seg == C:
            xp = jnp.where(row128 == 0, 0.0, xp)
            xn = jnp.where(row128 == C - 1, 0.0, xn)
        else:
            prev = ref[pl.ds(pl.multiple_of(jnp.maximum(r0 - HALO, 0), HALO), HALO), c0:c0 + 128].astype(F32)
            nxt = ref[pl.ds(pl.multiple_of(jnp.minimum(r0 + C, seq_len - HALO), HALO), HALO), c0:c0 + 128].astype(F32)
            pm = jnp.where(n > 0, 1.0, 0.0)
            nm = jnp.where(n < n_chunks - 1, 1.0, 0.0)
            xp = jnp.where(row128 == 0, prev[HALO - 1:HALO, :] * pm, xp)
            xn = jnp.where(row128 == C - 1, nxt[0:1, :] * nm, xn)
        y = cw_ref[0:1, wc0:wc0 + 128] * xp + cw_ref[1:2, wc0:wc0 + 128] * x + cw_ref[2:3, wc0:wc0 + 128] * xn
        return _silu(y)

    def l2n(x):
        return x * lax.rsqrt(jnp.sum(x * x, axis=-1, keepdims=True) + RMS_EPS)

    def chunk_body(it, carry):
        qs, ks, vs, kks, qks, betas, gcols, gsums, gtots = [], [], [], [], [], [], [], [], []
        for cc in range(cb):
            n = it * cb + cc
            r0 = pl.multiple_of(n * C, C)
            beta_full = _sigmoid(sm_ref[pl.ds(r0, C), 0:128])
            g_full = -jnp.exp(arow_ref[...]) * _softplus(sm_ref[pl.ds(r0, C), 128:256] + dtb_ref[...])
            cs = _dot(cum3_ref[...], _split3_rows(g_full))
            cols = jnp.concatenate([g_full, beta_full, cs], axis=0)
            chi, clo = _split2(cols)
            bcast = _dot(jnp.concatenate([chi, clo], axis=1), e2_ref[...])
            qh, kh, vh, kkh, qkh = [], [], [], [], []
            for h in range(H_D):
                q = l2n(conv_block(q_ref, h * 128, h * 128, r0, n)) * (DK_D ** -0.5)
                k = l2n(conv_block(k_ref, h * 128, 512 + h * 128, r0, n))
                v = conv_block(v_ref, h * 128, 1024 + h * 128, r0, n)
                k16 = k.astype(BF16)
                qh.append(q)
                kh.append(k)
                vh.append(v)
                kkh.append(_dot_nt(k16, k16))
                qkh.append(_dot_nt(q.astype(BF16), k16))
            for dst, src in ((qs, qh), (ks, kh), (vs, vh), (kks, kkh), (qks, qkh)):
                dst.extend(src + src)
            for b in range(NHD):
                lanes = slice(b * 128, (b + 1) * 128)
                d = b // H_D
                gcols.append(bcast[0:C, lanes])
                betas.append(bcast[C:2 * C, lanes])
                gsums.append(bcast[(2 + d) * C:(3 + d) * C, lanes])
                gtots.append(bcast[4 * C:5 * C, lanes])
        st = lambda xs: jnp.stack(xs, axis=0)
        q, k, v, kk, qk = st(qs), st(ks), st(vs), st(kks), st(qks)
        beta, gcol, gsum, gtot = st(betas), st(gcols), st(gsums), st(gtots)
        xg = jnp.where(strict, gcol[:, :, 0:C], 0.0)
        xh, xl = _split2(xg)
        tri2 = jnp.concatenate([tri2_ref[...]] * cb, axis=0)
        diff = _bdot(tri2, jnp.concatenate([xh, xl], axis=1))
        gamma = jnp.where(incl, jnp.exp(jnp.minimum(diff, 0.0)), 0.0)
        m = jnp.where(strict, beta[:, :, 0:C] * kk * gamma, 0.0)
        eg = jnp.exp(gsum)
        rhs = jnp.concatenate([v * beta, k * (beta * eg)], axis=2)
        sol = _bdot(_tri_inverse(m, eye, row, col).astype(BF16), rhs.astype(BF16))
        wq = jnp.concatenate([sol[:, :, 128:], q * eg], axis=1).astype(BF16)
        at = (qk * gamma).astype(BF16)
        kd = (k * jnp.exp(gtot - gsum)).astype(BF16)
        ls = jnp.exp(gtot[:, 0:8, :])
        for cc in range(cb):
            n = it * cb + cc
            for d in range(2):
                t = n if d == 0 else n_chunks - 1 - n
                dst = pl.ds(t * NHD + d * H_D, H_D)
                src = slice(cc * NHD + d * H_D, cc * NHD + (d + 1) * H_D)
                u_scr[dst] = sol[src, :, :128]
                wq_scr[dst] = wq[src]
                at_scr[dst] = at[src]
                kd_scr[dst] = kd[src]
                ls_scr[dst] = ls[src]
        return carry

    lax.fori_loop(0, n_chunks // cb, chunk_body, 0)

    if has_init:
        s_scr[...] = s0_ref[0]
    else:
        s_scr[...] = jnp.zeros_like(s_scr)

    def scan_body(i, carry):
        slot = pl.ds(i * NHD, NHD)
        s = s_scr[...]
        r = _bdot(wq_scr[slot], s.astype(BF16))
        v16 = (u_scr[slot] - r[:, 0:C]).astype(BF16)
        o = r[:, C:2 * C] + _bdot(at_scr[slot], v16)
        s_scr[...] = s * ls_scr[slot][:, 0:1, :] + _bdot_tn(kd_scr[slot], v16)
        for d in range(2):
            n = i if d == 0 else n_chunks - 1 - i
            r0 = pl.multiple_of(n * C, C)
            o_dst = of_scr if d == 0 else ob_scr
            for h in range(H_D):
                o_dst[pl.ds(r0, C), h * 128:(h + 1) * 128] = o[d * H_D + h]
        return carry

    lax.fori_loop(0, n_chunks, scan_body, 0)

    if want_state:
        sfin_ref[0] = s_scr[...]

    def out_body(n, carry):
        r0 = pl.multiple_of(n * C, C)
        for h in range(H_D):
            o = of_scr[pl.ds(r0, C), h * 128:(h + 1) * 128] + ob_scr[pl.ds(r0, C), h * 128:(h + 1) * 128]
            o = o * lax.rsqrt(jnp.mean(o * o, axis=-1, keepdims=True) + RMS_EPS) * ng_ref[...]
            zg = z_ref[pl.ds(r0, C), h * 128:(h + 1) * 128].astype(F32)
            o_ref[pl.ds(r0, C), h * 128:(h + 1) * 128] = (o * _silu(zg)).astype(BF16)
        return carry

    lax.fori_loop(0, n_chunks, out_body, 0)


def _delta(p, sm, consts, conv_qkv, arow, dtb, ng, s0, *, seq_len, n_seq, row_blk0, seg, want_state):
    has_init = s0 is not None
    n_chunks = seq_len // C
    nhd = 2 * H_D
    cq = COL_DQ // 512
    full = lambda shape: pl.BlockSpec(shape, lambda s: (0,) * len(shape))
    in_specs = [
        pl.BlockSpec((seq_len, 512), lambda s: (s + row_blk0, cq)),
        pl.BlockSpec((seq_len, 512), lambda s: (s + row_blk0, cq + 1)),
        pl.BlockSpec((seq_len, 512), lambda s: (s + row_blk0, cq + 2)),
        pl.BlockSpec((seq_len, 512), lambda s: (s + row_blk0, cq + 3)),
        pl.BlockSpec((seq_len, SM_W), lambda s: (s + row_blk0, 0)),
        full((3, 1536)), full((1, 128)), full((1, 128)), full((1, 128)),
        full((3 * C, 3 * C)), full((NHD, C, 2 * C)), full((2 * 128, NHD * 128)),
    ]
    args = [p, p, p, p, sm, conv_qkv, arow, dtb, ng, consts["cum3_d"], consts["tri2_d"], consts["e2_d"]]
    if has_init:
        in_specs.append(pl.BlockSpec((1, nhd, DK_D, DV_D), lambda s: (s, 0, 0, 0)))
        args.append(s0)
    out_specs = [pl.BlockSpec((seq_len, 512), lambda s: (s, 0))]
    out_shape = [jax.ShapeDtypeStruct((n_seq * seq_len, 512), BF16)]
    if want_state:
        out_specs.append(pl.BlockSpec((1, nhd, DK_D, DV_D), lambda s: (s, 0, 0, 0)))
        out_shape.append(jax.ShapeDtypeStruct((n_seq, nhd, DK_D, DV_D), F32))
    res = pl.pallas_call(
        functools.partial(_delta_kernel, seq_len=seq_len, seg=seg, has_init=has_init, want_state=want_state),
        grid=(n_seq,),
        in_specs=in_specs,
        out_specs=out_specs,
        out_shape=out_shape,
        scratch_shapes=[
            pltpu.VMEM((nhd * n_chunks, C, 128), F32),
            pltpu.VMEM((nhd * n_chunks, 2 * C, 128), BF16),
            pltpu.VMEM((nhd * n_chunks, C, C), BF16),
            pltpu.VMEM((nhd * n_chunks, C, 128), BF16),
            pltpu.VMEM((nhd * n_chunks, 8, 128), F32),
            pltpu.VMEM((nhd, DK_D, DV_D), F32),
            pltpu.VMEM((seq_len, 512), F32),
            pltpu.VMEM((seq_len, 512), F32),
        ],
        compiler_params=_cparams(("arbitrary",)),
        name="delta_lat" if has_init else "delta_ctx",
    )(*args)
    return res


CG = CHUNK_G
GB = 64
HP = H_G // 2


def _gla_kernel(*refs, seq_len, has_init, want_state):
    (q_ref, k_ref, v_ref, r_ref, sm_ref, w2_ref, b_ref, ng_ref, cum3_ref, sele_ref) = refs[:10]
    pos = 10
    s0_ref = None
    if has_init:
        s0_ref = refs[pos]
        pos += 1
    o_ref = refs[pos]
    pos += 1
    sfin_ref = None
    if want_state:
        sfin_ref = refs[pos]
        pos += 1
    bc_scr, tot_scr, oacc_scr, ointer_scr = refs[pos:]

    n_blocks = seq_len // GB
    rowi = lax.broadcasted_iota(jnp.int32, (CG, H_G * DK_G), 0)
    half = CG // 2
    rowh = lax.broadcasted_iota(jnp.int32, (half, H_G * DK_G), 0)
    zero_half = jnp.zeros((GB // CG, half, H_G * DK_G), F32)
    qscale = DK_G ** -0.5

    def pairs(x, w):
        return jnp.stack([x[:, p * w:(p + 1) * w] for p in range(HP)], axis=0)

    sm16 = sm_ref[...].astype(BF16)
    for d in range(2):
        logits = _dot(sm16, w2_ref[d]) + b_ref[d]
        la = -_softplus(-logits) * (LOG2_E / GLA_TAU)
        for blk in range(n_blocks):
            cs = _dot(cum3_ref[...], _split3_rows(la[blk * GB:(blk + 1) * GB]))
            bc_scr[d, blk * GB:(blk + 1) * GB, :] = cs[d * GB:(d + 1) * GB]
            tot_scr[d, blk * GB:(blk + 1) * GB, :] = cs[2 * GB:3 * GB]

    def block_body(b, carry):
        r0 = pl.multiple_of(b * GB, GB)
        q = q_ref[pl.ds(r0, GB), :].astype(F32) * qscale
        k = k_ref[pl.ds(r0, GB), :].astype(F32)
        v = v_ref[pl.ds(r0, GB), :].astype(F32)
        nc = GB // CG
        q3, k3, v3 = (t.reshape(nc, CG, t.shape[-1]) for t in (q, k, v))
        o_blk = None
        for d in range(2):
            bc3 = bc_scr[d, pl.ds(r0, GB), :].reshape(nc, CG, H_G * DK_G)
            pieces = []
            for j in range(CG):
                if d == 0:
                    rs = slice(half if j >= half else 0, CG)
                    keep = (rowh >= j - half) if j >= half else (rowi >= j)
                else:
                    rs = slice(0, half if j < half else CG)
                    keep = (rowh <= j) if j < half else (rowi <= j)
                e = jnp.exp2(jnp.minimum(bc3[:, rs] - bc3[:, j:j + 1], 0.0))
                a = jnp.where(keep, q3[:, rs] * e * k3[:, j:j + 1], 0.0)
                if rs.stop - rs.start < CG:
                    a = jnp.concatenate([zero_half, a] if rs.start else [a, zero_half], axis=1)
                pieces.append(a.astype(BF16))
            a_all = jnp.concatenate(pieces, axis=1)
            rr = _dot(a_all.reshape(nc * CG * CG, a_all.shape[-1]), sele_ref[...])
            rr = rr.reshape(nc, CG * CG, rr.shape[-1])
            top = [j for j in range(CG) if d == 1 or j < half]
            bot = [j for j in range(CG) if d == 0 or j >= half]
            acc_t = functools.reduce(jnp.add, [rr[:, j * CG:j * CG + half] * v3[:, j:j + 1] for j in top])
            acc_b = functools.reduce(jnp.add, [rr[:, j * CG + half:(j + 1) * CG] * v3[:, j:j + 1] for j in bot])
            acc = jnp.concatenate([acc_t, acc_b], axis=1)
            o_blk = acc if d == 0 else o_blk + acc
        oacc_scr[pl.ds(r0, GB), :] = o_blk.reshape(GB, o_blk.shape[-1])
        return carry

    lax.fori_loop(0, n_blocks, block_body, 0)

    if has_init:
        st0 = (s0_ref[0, 0:HP], s0_ref[0, HP:2 * HP])
    else:
        st0 = (jnp.zeros((HP, 2 * DV_G, 2 * DK_G), F32),) * 2
    prow = lax.broadcasted_iota(jnp.int32, (2 * DV_G, 2 * DK_G), 0) // DV_G
    pcol = lax.broadcasted_iota(jnp.int32, (2 * DV_G, 2 * DK_G), 1) // DK_G
    pmask = jnp.where(prow == pcol, 1.0, 0.0).astype(F32)

    lane_head = lax.broadcasted_iota(jnp.int32, (1, 2 * DK_G), 1) // DK_G
    head_mask = [jnp.where(lane_head == a, 1.0, 0.0).astype(F32) for a in range(2)]
    nc = GB // CG
    bnt = lambda a, b: lax.dot_general(a, b, (((2,), (2,)), ((0,), (0,))), preferred_element_type=F32)

    nk = GB - CG
    krow = [lax.broadcasted_iota(jnp.int32, (nk, H_G * DK_G), 0) + off for off in (0, CG)]

    def scan_body(i, carry):
        new, inter, r0s, qts, kts, vks = [], [], [], [], [], []
        for d in range(2):
            st = carry[d]
            blk = i if d == 0 else n_blocks - 1 - i
            r0 = pl.multiple_of(blk * GB, GB)
            bcum = bc_scr[d, pl.ds(r0, GB), :]
            tot = tot_scr[d, pl.ds(r0, GB), :]
            q = q_ref[pl.ds(r0, GB), :].astype(F32) * qscale
            k = k_ref[pl.ds(r0, GB), :].astype(F32)
            v16 = v_ref[pl.ds(r0, GB), :]
            qd = pairs(q * jnp.exp2(bcum), 2 * DK_G).astype(BF16)
            kd = pairs(k * jnp.exp2(tot - bcum), 2 * DK_G).astype(BF16)
            o = bnt(qd, st.astype(BF16))
            inter.append(jnp.concatenate([o[p] for p in range(HP)], axis=1))
            r0s.append(r0)
            upd = _bdot_tn(pairs(v16, 2 * DV_G), kd)
            new.append(st * jnp.exp2(pairs(tot, 2 * DK_G)[:, 0:1, :]) + upd * pmask)
            kwin = slice(0, nk) if d == 0 else slice(CG, GB)
            for c in (range(1, nc) if d == 0 else range(nc - 1)):
                rows = slice(c * CG, (c + 1) * CG)
                ref_row = c * CG - 1 if d == 0 else (c + 1) * CG
                valid = (krow[0] < c * CG) if d == 0 else (krow[1] >= (c + 1) * CG)
                bref = bcum[ref_row:ref_row + 1, :]
                qt = q[rows] * jnp.exp2(bcum[rows] - bref)
                kt = jnp.where(valid, k[kwin] * jnp.exp2(jnp.minimum(bref - bcum[kwin], 0.0)), 0.0)
                qts.append(pairs(qt, 2 * DK_G))
                kts.append(pairs(kt, 2 * DK_G).astype(BF16))
                vks.append(pairs(v16[kwin], 2 * DV_G))
        cat = lambda xs: jnp.concatenate(xs, axis=0)
        qt = cat(qts)
        qq = jnp.concatenate([qt * head_mask[0], qt * head_mask[1]], axis=1).astype(BF16)
        attn = bnt(qq, cat(kts)).astype(BF16)
        ov = _bdot(attn, cat(vks))
        for d in range(2):
            out_rows = []
            for c in range(nc):
                rows = slice(c * CG, (c + 1) * CG)
                ci = c - 1 if d == 0 else c
                if ci < 0 or ci >= nc - 1:
                    out_rows.append(inter[d][rows])
                    continue
                base = (d * (nc - 1) + ci) * HP
                cross = jnp.concatenate([ov[base + p, a * CG:(a + 1) * CG, a * DV_G:(a + 1) * DV_G]
                                         for p in range(HP) for a in range(2)], axis=1)
                out_rows.append(inter[d][rows] + cross)
            ointer_scr[d, pl.ds(r0s[d], GB), :] = jnp.concatenate(out_rows, axis=0)
        return tuple(new)

    st_fin = lax.fori_loop(0, n_blocks, scan_body, st0)

    if want_state:
        for d in range(2):
            for p in range(HP):
                t = st_fin[d][p].T
                for a in range(2):
                    sfin_ref[0, d * H_G + 2 * p + a] = t[a * DK_G:(a + 1) * DK_G, a * DV_G:(a + 1) * DV_G]

    def out_body(b, carry):
        r0 = pl.multiple_of(b * GB, GB)
        for h in range(H_G):
            lanes = slice(h * DV_G, (h + 1) * DV_G)
            o = oacc_scr[pl.ds(r0, GB), lanes] + ointer_scr[0, pl.ds(r0, GB), lanes] + ointer_scr[1, pl.ds(r0, GB), lanes]
            o = o * lax.rsqrt(jnp.mean(o * o, axis=-1, keepdims=True) + RMS_EPS) * ng_ref[...]
            rg = r_ref[pl.ds(r0, GB), h * DV_G:(h + 1) * DV_G].astype(F32)
            o_ref[pl.ds(r0, GB), h * DV_G:(h + 1) * DV_G] = (o * _silu(rg)).astype(BF16)
        return carry

    lax.fori_loop(0, n_blocks, out_body, 0)


def _gla(p, sm, consts, w2p, gb, ng, s0, *, seq_len, n_seq, row_blk0, want_state):
    has_init = s0 is not None
    full = lambda shape: pl.BlockSpec(shape, lambda s: (0,) * len(shape))
    hk, hv = H_G * DK_G, H_G * DV_G
    in_specs = [
        pl.BlockSpec((seq_len, hk), lambda s: (s + row_blk0, COL_GQ // hk)),
        pl.BlockSpec((seq_len, hk), lambda s: (s + row_blk0, COL_GQ // hk + 1)),
        pl.BlockSpec((seq_len, hv), lambda s: (s + row_blk0, (COL_GQ + 2 * hk) // hv)),
        pl.BlockSpec((seq_len, hv), lambda s: (s + row_blk0, (COL_GQ + 2 * hk) // hv + 1)),
        pl.BlockSpec((seq_len, 128), lambda s: (s + row_blk0, 0)),
        full((2, 128, hk)), full((2, 1, hk)), full((1, DV_G)),
        full((3 * GB, 3 * GB)), full((CG * CG, hv)),
    ]
    args = [p, p, p, p, sm, w2p, gb, ng, consts["cum3_d"], consts["sele"]]
    if has_init:
        in_specs.append(pl.BlockSpec((1, 2 * HP, 2 * DV_G, 2 * DK_G), lambda s: (s, 0, 0, 0)))
        args.append(s0)
    out_specs = [pl.BlockSpec((seq_len, hv), lambda s: (s, 0))]
    out_shape = [jax.ShapeDtypeStruct((n_seq * seq_len, hv), BF16)]
    if want_state:
        out_specs.append(pl.BlockSpec((1, 2 * H_G, DK_G, DV_G), lambda s: (s, 0, 0, 0)))
        out_shape.append(jax.ShapeDtypeStruct((n_seq, 2 * H_G, DK_G, DV_G), F32))
    return pl.pallas_call(
        functools.partial(_gla_kernel, seq_len=seq_len, has_init=has_init, want_state=want_state),
        grid=(n_seq,),
        in_specs=in_specs,
        out_specs=out_specs,
        out_shape=out_shape,
        scratch_shapes=[
            pltpu.VMEM((2, seq_len, hk), F32),
            pltpu.VMEM((2, seq_len, hk), F32),
            pltpu.VMEM((seq_len, hv), F32),
            pltpu.VMEM((2, seq_len, hv), F32),
        ],
        compiler_params=_cparams(("arbitrary",)),
        name="gla_lat" if has_init else "gla_ctx",
    )(*args)


def _constants():
    i = np.arange(C)
    lo = (i[:, None] >= i[None, :]).astype(np.float32)
    up = (i[:, None] <= i[None, :]).astype(np.float32)
    ones = np.ones((C, C), np.float32)
    cum_d = np.concatenate([lo, up, ones], axis=0)
    hk, hv = H_G * DK_G, H_G * DV_G
    sele = (np.arange(hk)[:, None] // DK_G == np.arange(hv)[None, :] // DV_G).astype(np.float32)
    lane_sel = (np.arange(128)[:, None] == np.arange(NHD * 128)[None, :] // 128).astype(np.float32)
    return {
        "e2_d": jnp.asarray(np.tile(lane_sel, (2, 1)), BF16),
        "cum3_d": jnp.asarray(np.tile(cum_d, (1, 3)), BF16),
        "tri2_d": jnp.asarray(np.stack([np.tile(lo, (1, 2))] * H_D + [np.tile(up, (1, 2))] * H_D), BF16),
        "sele": jnp.asarray(sele, BF16),
    }


def _permute_w_in(w_in_l):
    widths = (W_A, W_A, W_A, 512, 512, 512, 512, 8, 8, 256, 256, 512, 512, 32, 3 * D_MODEL)
    offs = np.concatenate([[0], np.cumsum(widths)])
    seg = lambda a, b: w_in_l[:, offs[a]:offs[b]]
    zeros = lambda n: jnp.zeros((D_MODEL, n), w_in_l.dtype)
    narrow = [seg(7, 8), zeros(SM_LR - 8), seg(13, 14), zeros(SM_A - SM_LR - 2 * GLA_RANK),
              seg(8, 9), zeros(SM_W - SM_A - 8)]
    return jnp.concatenate([seg(0, 3), seg(3, 7), seg(9, 13), seg(14, 15)] + narrow, axis=1).astype(BF16)


def _gla_pack_state(s):
    n = s.shape[0]
    st = jnp.swapaxes(s, -1, -2).reshape(n, 2, HP, 2, DV_G, DK_G)
    packed = jnp.einsum("ndpavk,ab->ndpavbk", st, jnp.eye(2, dtype=s.dtype))
    return packed.reshape(n, 2 * HP, 2 * DV_G, 2 * DK_G)


def _lane_row(vals8, lane0):
    return jnp.zeros((1, 128), F32).at[0, lane0:lane0 + 8].set(vals8.reshape(8).astype(F32))


def kernel(x_prompt, x_sample, state_delta, state_gla, c, c_ctx, w_ada, b_ada, ln_g, ln_b, ffn_w1, ffn_w2, w_in,
           conv_a, conv_qkv, delta_a_log, delta_dt_bias, delta_norm_g, gla_w2, gla_b, gla_norm_g,
           w_br_a, w_br_d, w_br_g, w_o):
    n_ctx, ctx_len, _ = x_prompt.shape
    n_lat, lat_len, _ = x_sample.shape
    n_ctx_tok = n_ctx * ctx_len
    n_lat_tok = n_lat * lat_len
    assert ctx_len == CTX_SEG and MERGE_TM % CTX_SEG == 0 and n_ctx_tok % MERGE_TM == 0
    assert lat_len % MERGE_TM == 0 and n_ctx_tok % lat_len == 0 and n_lat <= 8

    consts = _constants()
    x = (x_prompt.reshape(n_ctx_tok, D_MODEL), x_sample.reshape(n_lat_tok, D_MODEL))
    cond16 = jnp.zeros((16, D_MODEL), F32).at[:n_lat].set(c).at[8].set(c_ctx)
    ada = _ada_table(cond16, w_ada, b_ada)

    sds, sgs = [], []
    for l in range(DEPTH):
        ada_l = ada[l]
        kw = dict(n_ctx_tok=n_ctx_tok, lat_len=lat_len)
        x = _ffn(x, ada_l, ffn_w1, ffn_w2, ln_g[l, 0], ln_b[l, 0], l, 0, **kw)
        p, sm = _inproj(x, ada_l, _permute_w_in(w_in[l]), **kw)

        arow = _lane_row(delta_a_log[l], SM_A % 128)
        dtb = _lane_row(delta_dt_bias[l], SM_A % 128)
        ngd = delta_norm_g[l].reshape(1, DV_D)
        od_ctx, sd = _delta(p, sm, consts, conv_qkv[l], arow, dtb, ngd, None, seq_len=ctx_len, n_seq=n_ctx,
                            row_blk0=0, seg=ctx_len, want_state=True)
        s0d = state_delta[:, l].astype(F32).reshape(n_lat, 2 * H_D, DK_D, DV_D)
        (od_lat,) = _delta(p, sm, consts, conv_qkv[l], arow, dtb, ngd, s0d, seq_len=lat_len, n_seq=n_lat,
                           row_blk0=n_ctx_tok // lat_len, seg=GRID_W, want_state=False)
        sds.append(sd.reshape(n_ctx, 2, H_D, DK_D, DV_D))

        w2p = jnp.zeros((2, 128, H_G * DK_G), F32)
        for d in range(2):
            w2p = w2p.at[d, SM_LR + d * GLA_RANK:SM_LR + (d + 1) * GLA_RANK].set(gla_w2[l, d])
        w2p = w2p.astype(BF16)
        gb = gla_b[l].reshape(2, 1, H_G * DK_G)
        ngg = gla_norm_g[l].reshape(1, DV_G)
        og_ctx, sg = _gla(p, sm, consts, w2p, gb, ngg, None, seq_len=ctx_len, n_seq=n_ctx, row_blk0=0,
                          want_state=True)
        s0g = _gla_pack_state(state_gla[:, l].astype(F32))
        (og_lat,) = _gla(p, sm, consts, w2p, gb, ngg, s0g, seq_len=lat_len, n_seq=n_lat,
                         row_blk0=n_ctx_tok // lat_len, want_state=False)
        sgs.append(sg.reshape(n_ctx, 2, H_G, DK_G, DV_G))

        x = _merge(p, od_ctx, od_lat, og_ctx, og_lat, x, ada_l, conv_a[l], w_br_a[l].astype(BF16), w_br_d[l].astype(BF16),
                   w_br_g[l].astype(BF16), w_o[l].astype(BF16), ln_g[l, 1], ln_b[l, 1], **kw)
        x = _ffn(x, ada_l, ffn_w1, ffn_w2, ln_g[l, 2], ln_b[l, 2], l, 2, split_out=(l == DEPTH - 1), **kw)

    y_prompt = x[0].reshape(n_ctx, ctx_len, D_MODEL)
    y_sample = x[1].reshape(n_lat, lat_len, D_MODEL)
    new_state_delta = jnp.stack(sds, axis=1).astype(x_prompt.dtype)
    new_state_gla = jnp.stack(sgs, axis=1).astype(x_prompt.dtype)
    return (y_prompt, y_sample, new_state_delta, new_state_gla)
```

```python
import functools

import jax
import jax.numpy as jnp
import numpy as np
from jax import lax
from jax.experimental import pallas as pl
from jax.experimental.pallas import tpu as pltpu

F32 = jnp.float32
BF16 = jnp.bfloat16

D_MODEL = 1024
DEPTH = 2
GRID_W = 64
D_FF = 2816
W_A = 512
H_D, DK_D, DV_D, CHUNK_D = 4, 128, 128, 64
H_G, DK_G, DV_G, CHUNK_G = 4, 64, 128, 16
GLA_RANK = 16
GLA_TAU = 16.0
N_ADA = 9
ALPHA = float((2 * DEPTH) ** 0.25)
LN_EPS = 1e-5
RMS_EPS = 1e-6
LOG2_E = 1.4426950408889634

COL_A = 0
COL_DQ = 1536
COL_GQ = 3584
COL_MG = 5120
COL_SM = 8192
SM_W = 256
D_PROJ_PAD = COL_SM + SM_W
SM_BETA, SM_LR, SM_A = 0, 16, 128

VMEM_LIMIT = 56 * 1024 * 1024


def _cparams(sem):
    return pltpu.CompilerParams(dimension_semantics=sem, vmem_limit_bytes=VMEM_LIMIT)


def _dot(a, b):
    return jnp.dot(a, b, preferred_element_type=F32)


def _dot_nt(a, b):
    return lax.dot_general(a, b, (((1,), (1,)), ((), ())), preferred_element_type=F32)


def _sigmoid(x):
    return 1.0 / (1.0 + jnp.exp(-x))


def _silu(x):
    return x * _sigmoid(x)


def _softplus(x):
    return jnp.maximum(x, 0.0) + jnp.log1p(jnp.exp(-jnp.abs(x)))


def _split2(x):
    hi = x.astype(BF16)
    lo = (x - hi.astype(F32)).astype(BF16)
    return hi, lo


def _split3_rows(x):
    hi = x.astype(BF16)
    r = x - hi.astype(F32)
    mid = r.astype(BF16)
    lo = (r - mid.astype(F32)).astype(BF16)
    return jnp.concatenate([hi, mid, lo], axis=0)


def _layer_norm(y, g, b):
    mu = jnp.mean(y, axis=-1, keepdims=True)
    yc = y - mu
    var = jnp.mean(yc * yc, axis=-1, keepdims=True)
    return yc * lax.rsqrt(var + LN_EPS) * g + b


def _ada_kernel(cond_ref, w_ref, b_ref, o_ref):
    s = _silu(cond_ref[...]).astype(BF16)
    o_ref[0] = _dot(s, w_ref[0].astype(BF16)) + b_ref[0]


def _ada_table(cond16, w_ada, b_ada):
    n_l = w_ada.shape[0]
    tn = N_ADA * D_MODEL // 4
    out = pl.pallas_call(
        _ada_kernel,
        grid=(n_l, N_ADA * D_MODEL // tn),
        in_specs=[
            pl.BlockSpec((16, D_MODEL), lambda l, j: (0, 0)),
            pl.BlockSpec((1, D_MODEL, tn), lambda l, j: (l, 0, j)),
            pl.BlockSpec((1, 1, tn), lambda l, j: (l, 0, j)),
        ],
        out_specs=pl.BlockSpec((1, 16, tn), lambda l, j: (l, 0, j)),
        out_shape=jax.ShapeDtypeStruct((n_l, 16, N_ADA * D_MODEL), F32),
        compiler_params=_cparams(("arbitrary", "arbitrary")),
        name="ada",
    )(cond16, w_ada, b_ada.reshape(n_l, 1, N_ADA * D_MODEL))
    return out.reshape(n_l, 16, N_ADA, D_MODEL)


def _cond_row(tok0, n_ctx_tok, lat_len):
    return jnp.where(tok0 < n_ctx_tok, 8, (tok0 - n_ctx_tok) // lat_len)


MXU_TILE = 256
FF_SPLIT = (D_FF // MXU_TILE // 2) * MXU_TILE
FF_CHUNKS = ((0, FF_SPLIT), (FF_SPLIT, D_FF))


W1_ROWS = 128
W2_ROWS = 256


def _stage_weight(w_hbm, dst, stage, sem, rows):
    n_chunks = w_hbm.shape[0] // rows

    def copy(c):
        return pltpu.make_async_copy(w_hbm.at[pl.ds(c * rows, rows), :], stage.at[c % 2], sem.at[c % 2])

    copy(0).start()
    for c in range(n_chunks):
        if c + 1 < n_chunks:
            copy(c + 1).start()
        copy(c).wait()
        dst[c * rows:(c + 1) * rows, :] = stage[c % 2].astype(BF16)


def _ffn_kernel(*refs, l, j, n, n_ctx_tiles, split_in, split_out):
    n_x = 4 if split_in else 2
    x_refs = refs[:n_x]
    adac_ref, adap_ref, w1_hbm, w2_hbm, lng_ref, lnb_ref = refs[n_x:n_x + 6]
    n_o = 2 if split_out else 1
    o_refs = refs[n_x + 6:n_x + 6 + n_o]
    y_scr, w1_ref, w2_ref, st1, st2, sem1, sem2 = refs[n_x + 6 + n_o:]
    jj = j // 2
    i = pl.program_id(0)
    prev_is_ctx = jnp.maximum(i - 1, 0) < n_ctx_tiles

    @pl.when(i == 0)
    def _():
        y_scr[...] = jnp.zeros_like(y_scr)
        _stage_weight(w1_hbm.at[l, jj], w1_ref, st1, sem1, W1_ROWS)
        _stage_weight(w2_hbm.at[l, jj], w2_ref, st2, sem2, W2_ROWS)

    if split_in:
        x_cur = jnp.where(jnp.minimum(i, n - 1) < n_ctx_tiles, x_refs[0][...], x_refs[1][...])
        x_prev = jnp.where(prev_is_ctx, x_refs[2][...], x_refs[3][...])
    else:
        x_cur, x_prev = x_refs[0][...], x_refs[1][...]

    h = (x_cur * (1.0 + adac_ref[3 * j + 1:3 * j + 2, :]) + adac_ref[3 * j:3 * j + 1, :]).astype(BF16)
    y_new = None
    for c0, c1 in FF_CHUNKS:
        g = _dot(h, w1_ref[:, c0:c1])
        u = _dot(h, w1_ref[:, D_FF + c0:D_FF + c1])
        part = _dot((_silu(g) * u).astype(BF16), w2_ref[c0:c1, :])
        y_new = part if y_new is None else y_new + part

    y = ALPHA * x_prev + 0.5 * adap_ref[3 * j + 2:3 * j + 3, :] * y_scr[...]
    out = _layer_norm(y, lng_ref[...], lnb_ref[...])
    if split_out:
        @pl.when(prev_is_ctx)
        def _():
            o_refs[0][...] = out

        @pl.when(jnp.logical_not(prev_is_ctx))
        def _():
            o_refs[1][...] = out
    else:
        o_refs[0][...] = out
    y_scr[...] = y_new


def _ffn(x, ada_l, ffn_w1, ffn_w2, lng, lnb, l, j, n_ctx_tok, lat_len, tm=512, split_out=False):
    split_in = isinstance(x, tuple)
    n_ctx_tiles = n_ctx_tok // tm
    t = n_ctx_tok + x[1].shape[0] if split_in else x.shape[0]
    n = t // tm
    cur = lambda i: jnp.minimum(i, n - 1)
    prev = lambda i: jnp.maximum(i - 1, 0)
    ctx_blk = lambda f: (lambda i: (jnp.minimum(f(i), n_ctx_tiles - 1), 0))
    lat_blk = lambda f: (lambda i: (jnp.maximum(f(i) - n_ctx_tiles, 0), 0))
    cond = lambda i: _cond_row(i * tm, n_ctx_tok, lat_len)
    resident = lambda shape: pl.BlockSpec(shape, lambda i: (0,) * len(shape), pipeline_mode=pl.Buffered(1))
    tile = lambda index_map: pl.BlockSpec((tm, D_MODEL), index_map)
    if split_in:
        x_specs = [tile(ctx_blk(cur)), tile(lat_blk(cur)), tile(ctx_blk(prev)), tile(lat_blk(prev))]
        x_args = [x[0], x[1], x[0], x[1]]
    else:
        x_specs = [tile(lambda i: (cur(i), 0)), tile(lambda i: (prev(i), 0))]
        x_args = [x, x]
    if split_out:
        out_specs = [tile(ctx_blk(prev)), tile(lat_blk(prev))]
        out_shape = [jax.ShapeDtypeStruct((n_ctx_tok, D_MODEL), F32), jax.ShapeDtypeStruct((t - n_ctx_tok, D_MODEL), F32)]
    else:
        out_specs = tile(lambda i: (prev(i), 0))
        out_shape = jax.ShapeDtypeStruct((t, D_MODEL), F32)
    return pl.pallas_call(
        functools.partial(_ffn_kernel, l=l, j=j, n=n, n_ctx_tiles=n_ctx_tiles, split_in=split_in, split_out=split_out),
        grid=(n + 1,),
        in_specs=x_specs + [
            pl.BlockSpec((None, N_ADA, D_MODEL), lambda i: (cond(cur(i)), 0, 0)),
            pl.BlockSpec((None, N_ADA, D_MODEL), lambda i: (cond(prev(i)), 0, 0)),
            pl.BlockSpec(memory_space=pl.ANY),
            pl.BlockSpec(memory_space=pl.ANY),
            resident((1, D_MODEL)),
            resident((1, D_MODEL)),
        ],
        out_specs=out_specs,
        out_shape=out_shape,
        scratch_shapes=[
            pltpu.VMEM((tm, D_MODEL), F32),
            pltpu.VMEM((D_MODEL, 2 * D_FF), BF16),
            pltpu.VMEM((D_FF, D_MODEL), BF16),
            pltpu.VMEM((2, W1_ROWS, 2 * D_FF), F32),
            pltpu.VMEM((2, W2_ROWS, D_MODEL), F32),
            pltpu.SemaphoreType.DMA((2,)),
            pltpu.SemaphoreType.DMA((2,)),
        ],
        compiler_params=_cparams(("arbitrary",)),
        name="ffn",
    )(*x_args, ada_l, ada_l, ffn_w1, ffn_w2, lng.reshape(1, D_MODEL), lnb.reshape(1, D_MODEL))


INPROJ_NC = 8 * MXU_TILE


def _inproj_kernel(x_ref, ada_ref, w_ref, p_ref, sm_ref):
    h = (x_ref[...] * (1.0 + ada_ref[4:5, :]) + ada_ref[3:4, :]).astype(BF16)
    for c0 in range(0, COL_SM, INPROJ_NC):
        p_ref[:, c0:c0 + INPROJ_NC] = _dot(h, w_ref[:, c0:c0 + INPROJ_NC]).astype(BF16)
    sm_ref[...] = _dot(h, w_ref[:, COL_SM:D_PROJ_PAD])


def _inproj(x, ada_l, w_in_p, n_ctx_tok, lat_len, tm=512):
    t = x.shape[0]
    cond = lambda i: _cond_row(i * tm, n_ctx_tok, lat_len)
    return pl.pallas_call(
        _inproj_kernel,
        grid=(t // tm,),
        in_specs=[
            pl.BlockSpec((tm, D_MODEL), lambda i: (i, 0)),
            pl.BlockSpec((None, N_ADA, D_MODEL), lambda i: (cond(i), 0, 0)),
            pl.BlockSpec((D_MODEL, D_PROJ_PAD), lambda i: (0, 0), pipeline_mode=pl.Buffered(1)),
        ],
        out_specs=[pl.BlockSpec((tm, COL_SM), lambda i: (i, 0)),
                   pl.BlockSpec((tm, D_PROJ_PAD - COL_SM), lambda i: (i, 0))],
        out_shape=[jax.ShapeDtypeStruct((t, COL_SM), BF16),
                   jax.ShapeDtypeStruct((t, D_PROJ_PAD - COL_SM), F32)],
        compiler_params=_cparams(("arbitrary",)),
        name="inproj",
    )(x, ada_l, w_in_p)


MERGE_TM = 512
CTX_SEG = 256


def _merge_kernel(a_ref, m0_ref, m1_ref, m2_ref, odc_ref, odl_ref, ogc_ref, ogl_ref, x_ref, ada_ref, cw_ref,
                  wa_ref, wd_ref, wg_ref, wo_ref, lng_ref, lnb_ref, o_ref, *, n_ctx_tiles):
    i = pl.program_id(0)
    is_ctx = i < n_ctx_tiles
    o_d = jnp.where(is_ctx, odc_ref[...], odl_ref[...])
    o_g = jnp.where(is_ctx, ogc_ref[...], ogl_ref[...])
    seg = jnp.where(i < n_ctx_tiles, CTX_SEG, GRID_W)
    row = lax.broadcasted_iota(jnp.int32, (MERGE_TM, W_A), 0)
    pos = jnp.bitwise_and(row, seg - 1)
    a_x = a_ref[:, 0:W_A].astype(F32)
    a_b = a_ref[:, W_A:2 * W_A].astype(F32)
    a_c = a_ref[:, 2 * W_A:3 * W_A].astype(F32)
    z = a_c * a_x
    z_prev = jnp.where(pos == 0, 0.0, pltpu.roll(z, 1, 0))
    z_next = jnp.where(pos == seg - 1, 0.0, pltpu.roll(z, MERGE_TM - 1, 0))
    y_a = a_b * (cw_ref[0:1, :] * z_prev + cw_ref[1:2, :] * z + cw_ref[2:3, :] * z_next)
    br_a = _dot(y_a.astype(BF16), wa_ref[...])
    br_d = _dot(o_d, wd_ref[...])
    br_g = _dot(o_g, wg_ref[...])
    gate = lambda m_ref: _sigmoid(m_ref[...].astype(F32))
    merged = gate(m0_ref) * br_a + gate(m1_ref) * br_d + gate(m2_ref) * br_g
    y = _dot(merged.astype(BF16), wo_ref[...])
    y = ALPHA * x_ref[...] + ada_ref[5:6, :] * y
    o_ref[...] = _layer_norm(y, lng_ref[...], lnb_ref[...])


def _merge(p, od_ctx, od_lat, og_ctx, og_lat, x, ada_l, conv_a, wa, wd, wg, wo, lng, lnb, n_ctx_tok, lat_len):
    t = x.shape[0]
    tm = MERGE_TM
    n_ctx_tiles = n_ctx_tok // tm
    cond = lambda i: _cond_row(i * tm, n_ctx_tok, lat_len)
    full = lambda shape: pl.BlockSpec(shape, lambda i: (0,) * len(shape))
    ctx_blk = lambda i: (jnp.minimum(i, n_ctx_tiles - 1), 0)
    lat_blk = lambda i: (jnp.maximum(i - n_ctx_tiles, 0), 0)
    mg0 = COL_MG // D_MODEL
    return pl.pallas_call(
        functools.partial(_merge_kernel, n_ctx_tiles=n_ctx_tiles),
        grid=(t // tm,),
        in_specs=[
            pl.BlockSpec((tm, 3 * W_A), lambda i: (i, 0)),
            pl.BlockSpec((tm, D_MODEL), lambda i: (i, mg0)),
            pl.BlockSpec((tm, D_MODEL), lambda i: (i, mg0 + 1)),
            pl.BlockSpec((tm, D_MODEL), lambda i: (i, mg0 + 2)),
            pl.BlockSpec((tm, 512), ctx_blk), pl.BlockSpec((tm, 512), lat_blk),
            pl.BlockSpec((tm, 512), ctx_blk), pl.BlockSpec((tm, 512), lat_blk),
            pl.BlockSpec((tm, D_MODEL), lambda i: (i, 0)),
            pl.BlockSpec((None, N_ADA, D_MODEL), lambda i: (cond(i), 0, 0)),
            full((3, W_A)),
            full((W_A, D_MODEL)), full((512, D_MODEL)), full((512, D_MODEL)), full((D_MODEL, D_MODEL)),
            full((1, D_MODEL)), full((1, D_MODEL)),
        ],
        out_specs=pl.BlockSpec((tm, D_MODEL), lambda i: (i, 0)),
        out_shape=jax.ShapeDtypeStruct((t, D_MODEL), F32),
        compiler_params=_cparams(("arbitrary",)),
        name="merge",
    )(p, p, p, p, od_ctx, od_lat, og_ctx, og_lat, x, ada_l, conv_a, wa, wd, wg, wo,
      lng.reshape(1, D_MODEL), lnb.reshape(1, D_MODEL))


C = CHUNK_D


def _bdot(a, b):
    return lax.dot_general(a, b, (((2,), (1,)), ((0,), (0,))), preferred_element_type=F32)


def _bdot_tn(a, b):
    return lax.dot_general(a, b, (((1,), (1,)), ((0,), (0,))), preferred_element_type=F32)


def _tri_inverse(m, eye, row, col):
    def same_block(shift):
        return jnp.right_shift(row, shift) == jnp.right_shift(col, shift)

    m8 = jnp.where(same_block(3), m, 0.0)
    m16 = m8.astype(BF16)
    x = eye - m8
    sq = _bdot(m16, m16).astype(BF16)
    x = x + _bdot(x.astype(BF16), sq)
    sq = _bdot(sq, sq).astype(BF16)
    x = x + _bdot(x.astype(BF16), sq)
    for shift in (4, 5, 6):
        e = jnp.where(same_block(shift) & jnp.logical_not(same_block(shift - 1)), m, 0.0)
        x16 = x.astype(BF16)
        x = x - _bdot(_bdot(x16, e.astype(BF16)).astype(BF16), x16)
    return x


NHD = 2 * H_D
CB_MAX = 4
HALO = 16


def _delta_kernel(*refs, seq_len, seg, has_init, want_state, n_prev):
    (q_ref, k_ref, v_ref, z_ref, sm_ref, cw_ref, arow_ref, dtb_ref, ng_ref, cum3_ref, tri2_ref, e2_ref) = refs[:12]
    pos = 12
    s0_ref = None
    if has_init:
        s0_ref = refs[pos]
        pos += 1
    prev_ref = None
    if n_prev:
        prev_ref = refs[pos]
        pos += 1
    o_ref = refs[pos]
    pos += 1
    sfin_ref = None
    if want_state:
        sfin_ref = refs[pos]
        pos += 1
    u_scr, wq_scr, at_scr, kd_scr, ls_scr, s_scr, of_scr, ob_scr = refs[pos:]

    n_chunks = seq_len // C
    cb = min(CB_MAX, n_chunks)
    nb = cb * NHD
    row = lax.broadcasted_iota(jnp.int32, (nb, C, C), 1)
    col = lax.broadcasted_iota(jnp.int32, (nb, C, C), 2)
    fwd = jnp.bitwise_and(lax.broadcasted_iota(jnp.int32, (nb, C, C), 0), NHD - 1) < H_D
    dist = jnp.where(fwd, row - col, col - row)
    strict = dist > 0
    incl = dist >= 0
    eye = jnp.where(row == col, 1.0, 0.0).astype(F32)
    row128 = lax.broadcasted_iota(jnp.int32, (C, 128), 0)

    def conv_block(ref, c0, wc0, r0, n):
        x = ref[pl.ds(r0, C), c0:c0 + 128].astype(F32)
        xp = pltpu.roll(x, 1, 0)
        xn = pltpu.roll(x, C - 1, 0)
        if seg == C:
            xp = jnp.where(row128 == 0, 0.0, xp)
            xn = jnp.where(row128 == C - 1, 0.0, xn)
        else:
            prev = ref[pl.ds(pl.multiple_of(jnp.maximum(r0 - HALO, 0), HALO), HALO), c0:c0 + 128].astype(F32)
            nxt = ref[pl.ds(pl.multiple_of(jnp.minimum(r0 + C, seq_len - HALO), HALO), HALO), c0:c0 + 128].astype(F32)
            pm = jnp.where(n > 0, 1.0, 0.0)
            nm = jnp.where(n < n_chunks - 1, 1.0, 0.0)
            xp = jnp.where(row128 == 0, prev[HALO - 1:HALO, :] * pm, xp)
            xn = jnp.where(row128 == C - 1, nxt[0:1, :] * nm, xn)
        y = cw_ref[0:1, wc0:wc0 + 128] * xp + cw_ref[1:2, wc0:wc0 + 128] * x + cw_ref[2:3, wc0:wc0 + 128] * xn
        return _silu(y)

    def l2n(x):
        return x * lax.rsqrt(jnp.sum(x * x, axis=-1, keepdims=True) + RMS_EPS)

    def chunk_body(it, carry):
        qs, ks, vs, kks, qks, betas, gcols, gsums, gtots = [], [], [], [], [], [], [], [], []
        for cc in range(cb):
            n = it * cb + cc
            r0 = pl.multiple_of(n * C, C)
            beta_full = _sigmoid(sm_ref[pl.ds(r0, C), 0:128])
            g_full = -jnp.exp(arow_ref[...]) * _softplus(sm_ref[pl.ds(r0, C), 128:256] + dtb_ref[...])
            cs = _dot(cum3_ref[...], _split3_rows(g_full))
            cols = jnp.concatenate([g_full, beta_full, cs], axis=0)
            chi, clo = _split2(cols)
            bcast = _dot(jnp.concatenate([chi, clo], axis=1), e2_ref[...])
            qh, kh, vh, kkh, qkh = [], [], [], [], []
            for h in range(H_D):
                q = l2n(conv_block(q_ref, h * 128, h * 128, r0, n)) * (DK_D ** -0.5)
                k = l2n(conv_block(k_ref, h * 128, 512 + h * 128, r0, n))
                v = conv_block(v_ref, h * 128, 1024 + h * 128, r0, n)
                k16 = k.astype(BF16)
                qh.append(q)
                kh.append(k)
                vh.append(v)
                kkh.append(_dot_nt(k16, k16))
                qkh.append(_dot_nt(q.astype(BF16), k16))
            for dst, src in ((qs, qh), (ks, kh), (vs, vh), (kks, kkh), (qks, qkh)):
                dst.extend(src + src)
            for b in range(NHD):
                lanes = slice(b * 128, (b + 1) * 128)
                d = b // H_D
                gcols.append(bcast[0:C, lanes])
                betas.append(bcast[C:2 * C, lanes])
                gsums.append(bcast[(2 + d) * C:(3 + d) * C, lanes])
                gtots.append(bcast[4 * C:5 * C, lanes])
        st = lambda xs: jnp.stack(xs, axis=0)
        q, k, v, kk, qk = st(qs), st(ks), st(vs), st(kks), st(qks)
        beta, gcol, gsum, gtot = st(betas), st(gcols), st(gsums), st(gtots)
        xg = jnp.where(strict, gcol[:, :, 0:C], 0.0)
        xh, xl = _split2(xg)
        tri2 = jnp.concatenate([tri2_ref[...]] * cb, axis=0)
        diff = _bdot(tri2, jnp.concatenate([xh, xl], axis=1))
        gamma = jnp.where(incl, jnp.exp(jnp.minimum(diff, 0.0)), 0.0)
        m = jnp.where(strict, beta[:, :, 0:C] * kk * gamma, 0.0)
        eg = jnp.exp(gsum)
        rhs = jnp.concatenate([v * beta, k * (beta * eg)], axis=2)
        sol = _bdot(_tri_inverse(m, eye, row, col).astype(BF16), rhs.astype(BF16))
        wq = jnp.concatenate([sol[:, :, 128:], q * eg], axis=1).astype(BF16)
        at = (qk * gamma).astype(BF16)
        kd = (k * jnp.exp(gtot - gsum)).astype(BF16)
        ls = jnp.exp(gtot[:, 0:8, :])
        for cc in range(cb):
            n = it * cb + cc
            for d in range(2):
                t = n if d == 0 else n_chunks - 1 - n
                dst = pl.ds(t * NHD + d * H_D, H_D)
                src = slice(cc * NHD + d * H_D, cc * NHD + (d + 1) * H_D)
                u_scr[dst] = sol[src, :, :128]
                wq_scr[dst] = wq[src]
                at_scr[dst] = at[src]
                kd_scr[dst] = kd[src]
                ls_scr[dst] = ls[src]
        return carry

    lax.fori_loop(0, n_chunks // cb, chunk_body, 0)

    if has_init:
        s_scr[...] = s0_ref[0]
    else:
        s_scr[...] = jnp.zeros_like(s_scr)

    def scan_body(i, carry):
        slot = pl.ds(i * NHD, NHD)
        s = s_scr[...]
        r = _bdot(wq_scr[slot], s.astype(BF16))
        v16 = (u_scr[slot] - r[:, 0:C]).astype(BF16)
        o = r[:, C:2 * C] + _bdot(at_scr[slot], v16)
        s_scr[...] = s * ls_scr[slot][:, 0:1, :] + _bdot_tn(kd_scr[slot], v16)
        for d in range(2):
            n = i if d == 0 else n_chunks - 1 - i
            r0 = pl.multiple_of(n * C, C)
            o_dst = of_scr if d == 0 else ob_scr
            for h in range(H_D):
                o_dst[pl.ds(r0, C), h * 128:(h + 1) * 128] = o[d * H_D + h]
        return carry

    lax.fori_loop(0, n_chunks, scan_body, 0)

    if want_state:
        if n_prev:
            sfin_ref[0, 0:n_prev] = prev_ref[0]
        sfin_ref[0, n_prev] = s_scr[...]

    def out_body(n, carry):
        r0 = pl.multiple_of(n * C, C)
        for h in range(H_D):
            o = of_scr[pl.ds(r0, C), h * 128:(h + 1) * 128] + ob_scr[pl.ds(r0, C), h * 128:(h + 1) * 128]
            o = o * lax.rsqrt(jnp.mean(o * o, axis=-1, keepdims=True) + RMS_EPS) * ng_ref[...]
            zg = z_ref[pl.ds(r0, C), h * 128:(h + 1) * 128].astype(F32)
            o_ref[pl.ds(r0, C), h * 128:(h + 1) * 128] = (o * _silu(zg)).astype(BF16)
        return carry

    lax.fori_loop(0, n_chunks, out_body, 0)


def _delta(p, sm, consts, conv_qkv, arow, dtb, ng, s0, *, seq_len, n_seq, row_blk0, seg, want_state,
           prev_states=None):
    has_init = s0 is not None
    n_prev = 0 if prev_states is None else prev_states.shape[1]
    n_chunks = seq_len // C
    nhd = 2 * H_D
    cq = COL_DQ // 512
    full = lambda shape: pl.BlockSpec(shape, lambda s: (0,) * len(shape))
    in_specs = [
        pl.BlockSpec((seq_len, 512), lambda s: (s + row_blk0, cq)),
        pl.BlockSpec((seq_len, 512), lambda s: (s + row_blk0, cq + 1)),
        pl.BlockSpec((seq_len, 512), lambda s: (s + row_blk0, cq + 2)),
        pl.BlockSpec((seq_len, 512), lambda s: (s + row_blk0, cq + 3)),
        pl.BlockSpec((seq_len, SM_W), lambda s: (s + row_blk0, 0)),
        full((3, 1536)), full((1, 128)), full((1, 128)), full((1, 128)),
        full((3 * C, 3 * C)), full((NHD, C, 2 * C)), full((2 * 128, NHD * 128)),
    ]
    args = [p, p, p, p, sm, conv_qkv, arow, dtb, ng, consts["cum3_d"], consts["tri2_d"], consts["e2_d"]]
    if has_init:
        in_specs.append(pl.BlockSpec((1, nhd, DK_D, DV_D), lambda s: (s, 0, 0, 0)))
        args.append(s0)
    if n_prev:
        in_specs.append(pl.BlockSpec((1, n_prev, nhd, DK_D, DV_D), lambda s: (s, 0, 0, 0, 0)))
        args.append(prev_states)
    out_specs = [pl.BlockSpec((seq_len, 512), lambda s: (s, 0))]
    out_shape = [jax.ShapeDtypeStruct((n_seq * seq_len, 512), BF16)]
    if want_state:
        out_specs.append(pl.BlockSpec((1, n_prev + 1, nhd, DK_D, DV_D), lambda s: (s, 0, 0, 0, 0)))
        out_shape.append(jax.ShapeDtypeStruct((n_seq, n_prev + 1, nhd, DK_D, DV_D), F32))
    res = pl.pallas_call(
        functools.partial(_delta_kernel, seq_len=seq_len, seg=seg, has_init=has_init, want_state=want_state,
                          n_prev=n_prev),
        grid=(n_seq,),
        in_specs=in_specs,
        out_specs=out_specs,
        out_shape=out_shape,
        scratch_shapes=[
            pltpu.VMEM((nhd * n_chunks, C, 128), F32),
            pltpu.VMEM((nhd * n_chunks, 2 * C, 128), BF16),
            pltpu.VMEM((nhd * n_chunks, C, C), BF16),
            pltpu.VMEM((nhd * n_chunks, C, 128), BF16),
            pltpu.VMEM((nhd * n_chunks, 8, 128), F32),
            pltpu.VMEM((nhd, DK_D, DV_D), F32),
            pltpu.VMEM((seq_len, 512), F32),
            pltpu.VMEM((seq_len, 512), F32),
        ],
        compiler_params=_cparams(("arbitrary",)),
        name="delta_lat" if has_init else "delta_ctx",
    )(*args)
    return res


CG = CHUNK_G
GB = 64
HP = H_G // 2


def _gla_kernel(*refs, seq_len, has_init, want_state, n_prev):
    (q_ref, k_ref, v_ref, r_ref, sm_ref, w2_ref, b_ref, ng_ref, cum3_ref, sele_ref) = refs[:10]
    pos = 10
    s0_ref = None
    if has_init:
        s0_ref = refs[pos]
        pos += 1
    prev_ref = None
    if n_prev:
        prev_ref = refs[pos]
        pos += 1
    o_ref = refs[pos]
    pos += 1
    sfin_ref = None
    if want_state:
        sfin_ref = refs[pos]
        pos += 1
    bc_scr, tot_scr, oacc_scr, ointer_scr = refs[pos:]

    n_blocks = seq_len // GB
    rowi = lax.broadcasted_iota(jnp.int32, (CG, H_G * DK_G), 0)
    half = CG // 2
    rowh = lax.broadcasted_iota(jnp.int32, (half, H_G * DK_G), 0)
    zero_half = jnp.zeros((GB // CG, half, H_G * DK_G), F32)
    qscale = DK_G ** -0.5

    def pairs(x, w):
        return jnp.stack([x[:, p * w:(p + 1) * w] for p in range(HP)], axis=0)

    sm16 = sm_ref[...].astype(BF16)
    for d in range(2):
        logits = _dot(sm16, w2_ref[d]) + b_ref[d]
        la = -_softplus(-logits) * (LOG2_E / GLA_TAU)
        for blk in range(n_blocks):
            cs = _dot(cum3_ref[...], _split3_rows(la[blk * GB:(blk + 1) * GB]))
            bc_scr[d, blk * GB:(blk + 1) * GB, :] = cs[d * GB:(d + 1) * GB]
            tot_scr[d, blk * GB:(blk + 1) * GB, :] = cs[2 * GB:3 * GB]

    def block_body(b, carry):
        r0 = pl.multiple_of(b * GB, GB)
        q = q_ref[pl.ds(r0, GB), :].astype(F32) * qscale
        k = k_ref[pl.ds(r0, GB), :].astype(F32)
        v = v_ref[pl.ds(r0, GB), :].astype(F32)
        nc = GB // CG
        q3, k3, v3 = (t.reshape(nc, CG, t.shape[-1]) for t in (q, k, v))
        o_blk = None
        for d in range(2):
            bc3 = bc_scr[d, pl.ds(r0, GB), :].reshape(nc, CG, H_G * DK_G)
            pieces = []
            for j in range(CG):
                if d == 0:
                    rs = slice(half if j >= half else 0, CG)
                    keep = (rowh >= j - half) if j >= half else (rowi >= j)
                else:
                    rs = slice(0, half if j < half else CG)
                    keep = (rowh <= j) if j < half else (rowi <= j)
                e = jnp.exp2(jnp.minimum(bc3[:, rs] - bc3[:, j:j + 1], 0.0))
                a = jnp.where(keep, q3[:, rs] * e * k3[:, j:j + 1], 0.0)
                if rs.stop - rs.start < CG:
                    a = jnp.concatenate([zero_half, a] if rs.start else [a, zero_half], axis=1)
                pieces.append(a.astype(BF16))
            a_all = jnp.concatenate(pieces, axis=1)
            rr = _dot(a_all.reshape(nc * CG * CG, a_all.shape[-1]), sele_ref[...])
            rr = rr.reshape(nc, CG * CG, rr.shape[-1])
            top = [j for j in range(CG) if d == 1 or j < half]
            bot = [j for j in range(CG) if d == 0 or j >= half]
            acc_t = functools.reduce(jnp.add, [rr[:, j * CG:j * CG + half] * v3[:, j:j + 1] for j in top])
            acc_b = functools.reduce(jnp.add, [rr[:, j * CG + half:(j + 1) * CG] * v3[:, j:j + 1] for j in bot])
            acc = jnp.concatenate([acc_t, acc_b], axis=1)
            o_blk = acc if d == 0 else o_blk + acc
        oacc_scr[pl.ds(r0, GB), :] = o_blk.reshape(GB, o_blk.shape[-1])
        return carry

    lax.fori_loop(0, n_blocks, block_body, 0)

    if has_init:
        st0 = (s0_ref[0, 0:HP], s0_ref[0, HP:2 * HP])
    else:
        st0 = (jnp.zeros((HP, 2 * DV_G, 2 * DK_G), F32),) * 2
    prow = lax.broadcasted_iota(jnp.int32, (2 * DV_G, 2 * DK_G), 0) // DV_G
    pcol = lax.broadcasted_iota(jnp.int32, (2 * DV_G, 2 * DK_G), 1) // DK_G
    pmask = jnp.where(prow == pcol, 1.0, 0.0).astype(F32)

    lane_head = lax.broadcasted_iota(jnp.int32, (1, 2 * DK_G), 1) // DK_G
    head_mask = [jnp.where(lane_head == a, 1.0, 0.0).astype(F32) for a in range(2)]
    nc = GB // CG
    bnt = lambda a, b: lax.dot_general(a, b, (((2,), (2,)), ((0,), (0,))), preferred_element_type=F32)

    nk = GB - CG
    krow = [lax.broadcasted_iota(jnp.int32, (nk, H_G * DK_G), 0) + off for off in (0, CG)]

    def scan_body(i, carry):
        new, inter, r0s, qts, kts, vks = [], [], [], [], [], []
        for d in range(2):
            st = carry[d]
            blk = i if d == 0 else n_blocks - 1 - i
            r0 = pl.multiple_of(blk * GB, GB)
            bcum = bc_scr[d, pl.ds(r0, GB), :]
            tot = tot_scr[d, pl.ds(r0, GB), :]
            q = q_ref[pl.ds(r0, GB), :].astype(F32) * qscale
            k = k_ref[pl.ds(r0, GB), :].astype(F32)
            v16 = v_ref[pl.ds(r0, GB), :]
            qd = pairs(q * jnp.exp2(bcum), 2 * DK_G).astype(BF16)
            kd = pairs(k * jnp.exp2(tot - bcum), 2 * DK_G).astype(BF16)
            o = bnt(qd, st.astype(BF16))
            inter.append(jnp.concatenate([o[p] for p in range(HP)], axis=1))
            r0s.append(r0)
            upd = _bdot_tn(pairs(v16, 2 * DV_G), kd)
            new.append(st * jnp.exp2(pairs(tot, 2 * DK_G)[:, 0:1, :]) + upd * pmask)
            kwin = slice(0, nk) if d == 0 else slice(CG, GB)
            for c in (range(1, nc) if d == 0 else range(nc - 1)):
                rows = slice(c * CG, (c + 1) * CG)
                ref_row = c * CG - 1 if d == 0 else (c + 1) * CG
                valid = (krow[0] < c * CG) if d == 0 else (krow[1] >= (c + 1) * CG)
                bref = bcum[ref_row:ref_row + 1, :]
                qt = q[rows] * jnp.exp2(bcum[rows] - bref)
                kt = jnp.where(valid, k[kwin] * jnp.exp2(jnp.minimum(bref - bcum[kwin], 0.0)), 0.0)
                qts.append(pairs(qt, 2 * DK_G))
                kts.append(pairs(kt, 2 * DK_G).astype(BF16))
                vks.append(pairs(v16[kwin], 2 * DV_G))
        cat = lambda xs: jnp.concatenate(xs, axis=0)
        qt = cat(qts)
        qq = jnp.concatenate([qt * head_mask[0], qt * head_mask[1]], axis=1).astype(BF16)
        attn = bnt(qq, cat(kts)).astype(BF16)
        ov = _bdot(attn, cat(vks))
        for d in range(2):
            out_rows = []
            for c in range(nc):
                rows = slice(c * CG, (c + 1) * CG)
                ci = c - 1 if d == 0 else c
                if ci < 0 or ci >= nc - 1:
                    out_rows.append(inter[d][rows])
                    continue
                base = (d * (nc - 1) + ci) * HP
                cross = jnp.concatenate([ov[base + p, a * CG:(a + 1) * CG, a * DV_G:(a + 1) * DV_G]
                                         for p in range(HP) for a in range(2)], axis=1)
                out_rows.append(inter[d][rows] + cross)
            ointer_scr[d, pl.ds(r0s[d], GB), :] = jnp.concatenate(out_rows, axis=0)
        return tuple(new)

    st_fin = lax.fori_loop(0, n_blocks, scan_body, st0)

    if want_state:
        if n_prev:
            sfin_ref[0, 0:n_prev] = prev_ref[0]
        for d in range(2):
            for p in range(HP):
                t = st_fin[d][p].T
                for a in range(2):
                    sfin_ref[0, n_prev, d * H_G + 2 * p + a] = t[a * DK_G:(a + 1) * DK_G, a * DV_G:(a + 1) * DV_G]

    def out_body(b, carry):
        r0 = pl.multiple_of(b * GB, GB)
        for h in range(H_G):
            lanes = slice(h * DV_G, (h + 1) * DV_G)
            o = oacc_scr[pl.ds(r0, GB), lanes] + ointer_scr[0, pl.ds(r0, GB), lanes] + ointer_scr[1, pl.ds(r0, GB), lanes]
            o = o * lax.rsqrt(jnp.mean(o * o, axis=-1, keepdims=True) + RMS_EPS) * ng_ref[...]
            rg = r_ref[pl.ds(r0, GB), h * DV_G:(h + 1) * DV_G].astype(F32)
            o_ref[pl.ds(r0, GB), h * DV_G:(h + 1) * DV_G] = (o * _silu(rg)).astype(BF16)
        return carry

    lax.fori_loop(0, n_blocks, out_body, 0)


def _gla(p, sm, consts, w2p, gb, ng, s0, *, seq_len, n_seq, row_blk0, want_state, prev_states=None):
    has_init = s0 is not None
    n_prev = 0 if prev_states is None else prev_states.shape[1]
    full = lambda shape: pl.BlockSpec(shape, lambda s: (0,) * len(shape))
    hk, hv = H_G * DK_G, H_G * DV_G
    in_specs = [
        pl.BlockSpec((seq_len, hk), lambda s: (s + row_blk0, COL_GQ // hk)),
        pl.BlockSpec((seq_len, hk), lambda s: (s + row_blk0, COL_GQ // hk + 1)),
        pl.BlockSpec((seq_len, hv), lambda s: (s + row_blk0, (COL_GQ + 2 * hk) // hv)),
        pl.BlockSpec((seq_len, hv), lambda s: (s + row_blk0, (COL_GQ + 2 * hk) // hv + 1)),
        pl.BlockSpec((seq_len, 128), lambda s: (s + row_blk0, 0)),
        full((2, 128, hk)), full((2, 1, hk)), full((1, DV_G)),
        full((3 * GB, 3 * GB)), full((CG * CG, hv)),
    ]
    args = [p, p, p, p, sm, w2p, gb, ng, consts["cum3_d"], consts["sele"]]
    if has_init:
        in_specs.append(pl.BlockSpec((1, 2 * HP, 2 * DV_G, 2 * DK_G), lambda s: (s, 0, 0, 0)))
        args.append(s0)
    if n_prev:
        in_specs.append(pl.BlockSpec((1, n_prev, 2 * H_G, DK_G, DV_G), lambda s: (s, 0, 0, 0, 0)))
        args.append(prev_states)
    out_specs = [pl.BlockSpec((seq_len, hv), lambda s: (s, 0))]
    out_shape = [jax.ShapeDtypeStruct((n_seq * seq_len, hv), BF16)]
    if want_state:
        out_specs.append(pl.BlockSpec((1, n_prev + 1, 2 * H_G, DK_G, DV_G), lambda s: (s, 0, 0, 0, 0)))
        out_shape.append(jax.ShapeDtypeStruct((n_seq, n_prev + 1, 2 * H_G, DK_G, DV_G), F32))
    return pl.pallas_call(
        functools.partial(_gla_kernel, seq_len=seq_len, has_init=has_init, want_state=want_state, n_prev=n_prev),
        grid=(n_seq,),
        in_specs=in_specs,
        out_specs=out_specs,
        out_shape=out_shape,
        scratch_shapes=[
            pltpu.VMEM((2, seq_len, hk), F32),
            pltpu.VMEM((2, seq_len, hk), F32),
            pltpu.VMEM((seq_len, hv), F32),
            pltpu.VMEM((2, seq_len, hv), F32),
        ],
        compiler_params=_cparams(("arbitrary",)),
        name="gla_lat" if has_init else "gla_ctx",
    )(*args)


def _constants():
    i = np.arange(C)
    lo = (i[:, None] >= i[None, :]).astype(np.float32)
    up = (i[:, None] <= i[None, :]).astype(np.float32)
    ones = np.ones((C, C), np.float32)
    cum_d = np.concatenate([lo, up, ones], axis=0)
    hk, hv = H_G * DK_G, H_G * DV_G
    sele = (np.arange(hk)[:, None] // DK_G == np.arange(hv)[None, :] // DV_G).astype(np.float32)
    lane_sel = (np.arange(128)[:, None] == np.arange(NHD * 128)[None, :] // 128).astype(np.float32)
    return {
        "e2_d": jnp.asarray(np.tile(lane_sel, (2, 1)), BF16),
        "cum3_d": jnp.asarray(np.tile(cum_d, (1, 3)), BF16),
        "tri2_d": jnp.asarray(np.stack([np.tile(lo, (1, 2))] * H_D + [np.tile(up, (1, 2))] * H_D), BF16),
        "sele": jnp.asarray(sele, BF16),
    }


def _permute_w_in(w_in_l):
    widths = (W_A, W_A, W_A, 512, 512, 512, 512, 8, 8, 256, 256, 512, 512, 32, 3 * D_MODEL)
    offs = np.concatenate([[0], np.cumsum(widths)])
    seg = lambda a, b: w_in_l[:, offs[a]:offs[b]]
    zeros = lambda n: jnp.zeros((D_MODEL, n), w_in_l.dtype)
    narrow = [seg(7, 8), zeros(SM_LR - 8), seg(13, 14), zeros(SM_A - SM_LR - 2 * GLA_RANK),
              seg(8, 9), zeros(SM_W - SM_A - 8)]
    return jnp.concatenate([seg(0, 3), seg(3, 7), seg(9, 13), seg(14, 15)] + narrow, axis=1).astype(BF16)


def _gla_pack_state(s):
    n = s.shape[0]
    st = jnp.swapaxes(s, -1, -2).reshape(n, 2, HP, 2, DV_G, DK_G)
    packed = jnp.einsum("ndpavk,ab->ndpavbk", st, jnp.eye(2, dtype=s.dtype))
    return packed.reshape(n, 2 * HP, 2 * DV_G, 2 * DK_G)


def _lane_row(vals8, lane0):
    return jnp.zeros((1, 128), F32).at[0, lane0:lane0 + 8].set(vals8.reshape(8).astype(F32))


def kernel(x_prompt, x_sample, state_delta, state_gla, c, c_ctx, w_ada, b_ada, ln_g, ln_b, ffn_w1, ffn_w2, w_in,
           conv_a, conv_qkv, delta_a_log, delta_dt_bias, delta_norm_g, gla_w2, gla_b, gla_norm_g,
           w_br_a, w_br_d, w_br_g, w_o):
    n_ctx, ctx_len, _ = x_prompt.shape
    n_lat, lat_len, _ = x_sample.shape
    n_ctx_tok = n_ctx * ctx_len
    n_lat_tok = n_lat * lat_len
    assert ctx_len == CTX_SEG and MERGE_TM % CTX_SEG == 0 and n_ctx_tok % MERGE_TM == 0
    assert lat_len % MERGE_TM == 0 and n_ctx_tok % lat_len == 0 and n_lat <= 8

    consts = _constants()
    x = (x_prompt.reshape(n_ctx_tok, D_MODEL), x_sample.reshape(n_lat_tok, D_MODEL))
    cond16 = jnp.zeros((16, D_MODEL), F32).at[:n_lat].set(c).at[8].set(c_ctx)
    ada = _ada_table(cond16, w_ada, b_ada)

    sd = sg = None
    for l in range(DEPTH):
        ada_l = ada[l]
        kw = dict(n_ctx_tok=n_ctx_tok, lat_len=lat_len)
        x = _ffn(x, ada_l, ffn_w1, ffn_w2, ln_g[l, 0], ln_b[l, 0], l, 0, **kw)
        p, sm = _inproj(x, ada_l, _permute_w_in(w_in[l]), **kw)

        arow = _lane_row(delta_a_log[l], SM_A % 128)
        dtb = _lane_row(delta_dt_bias[l], SM_A % 128)
        ngd = delta_norm_g[l].reshape(1, DV_D)
        od_ctx, sd = _delta(p, sm, consts, conv_qkv[l], arow, dtb, ngd, None, seq_len=ctx_len, n_seq=n_ctx,
                            row_blk0=0, seg=ctx_len, want_state=True, prev_states=sd)
        s0d = state_delta[:, l].astype(F32).reshape(n_lat, 2 * H_D, DK_D, DV_D)
        (od_lat,) = _delta(p, sm, consts, conv_qkv[l], arow, dtb, ngd, s0d, seq_len=lat_len, n_seq=n_lat,
                           row_blk0=n_ctx_tok // lat_len, seg=GRID_W, want_state=False)

        w2p = jnp.zeros((2, 128, H_G * DK_G), F32)
        for d in range(2):
            w2p = w2p.at[d, SM_LR + d * GLA_RANK:SM_LR + (d + 1) * GLA_RANK].set(gla_w2[l, d])
        w2p = w2p.astype(BF16)
        gb = gla_b[l].reshape(2, 1, H_G * DK_G)
        ngg = gla_norm_g[l].reshape(1, DV_G)
        og_ctx, sg = _gla(p, sm, consts, w2p, gb, ngg, None, seq_len=ctx_len, n_seq=n_ctx, row_blk0=0,
                          want_state=True, prev_states=sg)
        s0g = _gla_pack_state(state_gla[:, l].astype(F32))
        (og_lat,) = _gla(p, sm, consts, w2p, gb, ngg, s0g, seq_len=lat_len, n_seq=n_lat,
                         row_blk0=n_ctx_tok // lat_len, want_state=False)

        x = _merge(p, od_ctx, od_lat, og_ctx, og_lat, x, ada_l, conv_a[l], w_br_a[l].astype(BF16), w_br_d[l].astype(BF16),
                   w_br_g[l].astype(BF16), w_o[l].astype(BF16), ln_g[l, 1], ln_b[l, 1], **kw)
        x = _ffn(x, ada_l, ffn_w1, ffn_w2, ln_g[l, 2], ln_b[l, 2], l, 2, split_out=(l == DEPTH - 1), **kw)

    y_prompt = x[0].reshape(n_ctx, ctx_len, D_MODEL)
    y_sample = x[1].reshape(n_lat, lat_len, D_MODEL)
    new_state_delta = sd.reshape(n_ctx, DEPTH, 2, H_D, DK_D, DV_D).astype(x_prompt.dtype)
    new_state_gla = sg.reshape(n_ctx, DEPTH, 2, H_G, DK_G, DV_G).astype(x_prompt.dtype)
    return (y_prompt, y_sample, new_state_delta, new_state_gla)
```

```python
import functools

import jax
import jax.numpy as jnp
import numpy as np
from jax import lax
from jax.experimental import pallas as pl
from jax.experimental.pallas import tpu as pltpu

F32 = jnp.float32
BF16 = jnp.bfloat16

D_MODEL = 1024
DEPTH = 2
GRID_W = 64
D_FF = 2816
W_A = 512
H_D, DK_D, DV_D, CHUNK_D = 4, 128, 128, 64
H_G, DK_G, DV_G, CHUNK_G = 4, 64, 128, 16
GLA_RANK = 16
GLA_TAU = 16.0
N_ADA = 9
ALPHA = float((2 * DEPTH) ** 0.25)
LN_EPS = 1e-5
RMS_EPS = 1e-6
LOG2_E = 1.4426950408889634

COL_A = 0
COL_DQ = 1536
COL_GQ = 3584
COL_MG = 5120
COL_SM = 8192
SM_W = 256
D_PROJ_PAD = COL_SM + SM_W
SM_BETA, SM_LR, SM_A = 0, 16, 128

VMEM_LIMIT = 56 * 1024 * 1024


def _cparams(sem):
    return pltpu.CompilerParams(dimension_semantics=sem, vmem_limit_bytes=VMEM_LIMIT)


def _dot(a, b):
    return jnp.dot(a, b, preferred_element_type=F32)


def _dot_nt(a, b):
    return lax.dot_general(a, b, (((1,), (1,)), ((), ())), preferred_element_type=F32)


def _sigmoid(x):
    return 1.0 / (1.0 + jnp.exp(-x))


def _silu(x):
    return x * _sigmoid(x)


def _softplus(x):
    return jnp.maximum(x, 0.0) + jnp.log1p(jnp.exp(-jnp.abs(x)))


def _split2(x):
    hi = x.astype(BF16)
    lo = (x - hi.astype(F32)).astype(BF16)
    return hi, lo


def _split3_rows(x):
    hi = x.astype(BF16)
    r = x - hi.astype(F32)
    mid = r.astype(BF16)
    lo = (r - mid.astype(F32)).astype(BF16)
    return jnp.concatenate([hi, mid, lo], axis=0)


def _layer_norm(y, g, b):
    mu = jnp.mean(y, axis=-1, keepdims=True)
    yc = y - mu
    var = jnp.mean(yc * yc, axis=-1, keepdims=True)
    return yc * lax.rsqrt(var + LN_EPS) * g + b


def _ada_kernel(cond_ref, w_ref, b_ref, o_ref):
    s = _silu(cond_ref[...]).astype(BF16)
    o_ref[0] = _dot(s, w_ref[0].astype(BF16)) + b_ref[0]


def _ada_table(cond16, w_ada, b_ada):
    n_l = w_ada.shape[0]
    tn = N_ADA * D_MODEL // 4
    out = pl.pallas_call(
        _ada_kernel,
        grid=(n_l, N_ADA * D_MODEL // tn),
        in_specs=[
            pl.BlockSpec((16, D_MODEL), lambda l, j: (0, 0)),
            pl.BlockSpec((1, D_MODEL, tn), lambda l, j: (l, 0, j)),
            pl.BlockSpec((1, 1, tn), lambda l, j: (l, 0, j)),
        ],
        out_specs=pl.BlockSpec((1, 16, tn), lambda l, j: (l, 0, j)),
        out_shape=jax.ShapeDtypeStruct((n_l, 16, N_ADA * D_MODEL), F32),
        compiler_params=_cparams(("arbitrary", "arbitrary")),
        name="ada",
    )(cond16, w_ada, b_ada.reshape(n_l, 1, N_ADA * D_MODEL))
    return out.reshape(n_l, 16, N_ADA, D_MODEL)


def _cond_row(tok0, n_ctx_tok, lat_len):
    return jnp.where(tok0 < n_ctx_tok, 8, (tok0 - n_ctx_tok) // lat_len)


MXU_TILE = 256
FF_SPLIT = (D_FF // MXU_TILE // 2) * MXU_TILE
FF_CHUNKS = ((0, FF_SPLIT), (FF_SPLIT, D_FF))


W1_ROWS = 128
W2_ROWS = 256


def _stage_weight(w_hbm, dst, stage, sem, rows):
    n_chunks = w_hbm.shape[0] // rows

    def copy(c):
        return pltpu.make_async_copy(w_hbm.at[pl.ds(c * rows, rows), :], stage.at[c % 2], sem.at[c % 2])

    copy(0).start()
    for c in range(n_chunks):
        if c + 1 < n_chunks:
            copy(c + 1).start()
        copy(c).wait()
        dst[c * rows:(c + 1) * rows, :] = stage[c % 2].astype(BF16)


def _ffn_kernel(*refs, l, j, n, n_ctx_tiles, split_in, split_out):
    n_x = 4 if split_in else 2
    x_refs = refs[:n_x]
    adac_ref, adap_ref, w1_hbm, w2_hbm, lng_ref, lnb_ref = refs[n_x:n_x + 6]
    n_o = 2 if split_out else 1
    o_refs = refs[n_x + 6:n_x + 6 + n_o]
    y_scr, w1_ref, w2_ref, st1, st2, sem1, sem2 = refs[n_x + 6 + n_o:]
    jj = j // 2
    i = pl.program_id(0)
    prev_is_ctx = jnp.maximum(i - 1, 0) < n_ctx_tiles

    @pl.when(i == 0)
    def _():
        y_scr[...] = jnp.zeros_like(y_scr)
        _stage_weight(w1_hbm.at[l, jj], w1_ref, st1, sem1, W1_ROWS)
        _stage_weight(w2_hbm.at[l, jj], w2_ref, st2, sem2, W2_ROWS)

    if split_in:
        x_cur = jnp.where(jnp.minimum(i, n - 1) < n_ctx_tiles, x_refs[0][...], x_refs[1][...])
        x_prev = jnp.where(prev_is_ctx, x_refs[2][...], x_refs[3][...])
    else:
        x_cur, x_prev = x_refs[0][...], x_refs[1][...]

    h = (x_cur * (1.0 + adac_ref[3 * j + 1:3 * j + 2, :]) + adac_ref[3 * j:3 * j + 1, :]).astype(BF16)
    y_new = None
    for c0, c1 in FF_CHUNKS:
        g = _dot(h, w1_ref[:, c0:c1])
        u = _dot(h, w1_ref[:, D_FF + c0:D_FF + c1])
        part = _dot((_silu(g) * u).astype(BF16), w2_ref[c0:c1, :])
        y_new = part if y_new is None else y_new + part

    y = ALPHA * x_prev + 0.5 * adap_ref[3 * j + 2:3 * j + 3, :] * y_scr[...]
    out = _layer_norm(y, lng_ref[...], lnb_ref[...])
    if split_out:
        @pl.when(prev_is_ctx)
        def _():
            o_refs[0][...] = out

        @pl.when(jnp.logical_not(prev_is_ctx))
        def _():
            o_refs[1][...] = out
    else:
        o_refs[0][...] = out
    y_scr[...] = y_new


def _ffn(x, ada_l, ffn_w1, ffn_w2, lng, lnb, l, j, n_ctx_tok, lat_len, tm=512, split_out=False):
    split_in = isinstance(x, tuple)
    n_ctx_tiles = n_ctx_tok // tm
    t = n_ctx_tok + x[1].shape[0] if split_in else x.shape[0]
    n = t // tm
    cur = lambda i: jnp.minimum(i, n - 1)
    prev = lambda i: jnp.maximum(i - 1, 0)
    ctx_blk = lambda f: (lambda i: (jnp.minimum(f(i), n_ctx_tiles - 1), 0))
    lat_blk = lambda f: (lambda i: (jnp.maximum(f(i) - n_ctx_tiles, 0), 0))
    cond = lambda i: _cond_row(i * tm, n_ctx_tok, lat_len)
    resident = lambda shape: pl.BlockSpec(shape, lambda i: (0,) * len(shape), pipeline_mode=pl.Buffered(1))
    tile = lambda index_map: pl.BlockSpec((tm, D_MODEL), index_map)
    if split_in:
        x_specs = [tile(ctx_blk(cur)), tile(lat_blk(cur)), tile(ctx_blk(prev)), tile(lat_blk(prev))]
        x_args = [x[0], x[1], x[0], x[1]]
    else:
        x_specs = [tile(lambda i: (cur(i), 0)), tile(lambda i: (prev(i), 0))]
        x_args = [x, x]
    if split_out:
        out_specs = [tile(ctx_blk(prev)), tile(lat_blk(prev))]
        out_shape = [jax.ShapeDtypeStruct((n_ctx_tok, D_MODEL), F32), jax.ShapeDtypeStruct((t - n_ctx_tok, D_MODEL), F32)]
    else:
        out_specs = tile(lambda i: (prev(i), 0))
        out_shape = jax.ShapeDtypeStruct((t, D_MODEL), F32)
    return pl.pallas_call(
        functools.partial(_ffn_kernel, l=l, j=j, n=n, n_ctx_tiles=n_ctx_tiles, split_in=split_in, split_out=split_out),
        grid=(n + 1,),
        in_specs=x_specs + [
            pl.BlockSpec((None, N_ADA, D_MODEL), lambda i: (cond(cur(i)), 0, 0)),
            pl.BlockSpec((None, N_ADA, D_MODEL), lambda i: (cond(prev(i)), 0, 0)),
            pl.BlockSpec(memory_space=pl.ANY),
            pl.BlockSpec(memory_space=pl.ANY),
            resident((1, D_MODEL)),
            resident((1, D_MODEL)),
        ],
        out_specs=out_specs,
        out_shape=out_shape,
        scratch_shapes=[
            pltpu.VMEM((tm, D_MODEL), F32),
            pltpu.VMEM((D_MODEL, 2 * D_FF), BF16),
            pltpu.VMEM((D_FF, D_MODEL), BF16),
            pltpu.VMEM((2, W1_ROWS, 2 * D_FF), F32),
            pltpu.VMEM((2, W2_ROWS, D_MODEL), F32),
            pltpu.SemaphoreType.DMA((2,)),
            pltpu.SemaphoreType.DMA((2,)),
        ],
        compiler_params=_cparams(("arbitrary",)),
        name="ffn",
    )(*x_args, ada_l, ada_l, ffn_w1, ffn_w2, lng.reshape(1, D_MODEL), lnb.reshape(1, D_MODEL))


INPROJ_NC = 8 * MXU_TILE


def _inproj_kernel(x_ref, ada_ref, w_ref, p_ref, sm_ref):
    h = (x_ref[...] * (1.0 + ada_ref[4:5, :]) + ada_ref[3:4, :]).astype(BF16)
    for c0 in range(0, COL_SM, INPROJ_NC):
        p_ref[:, c0:c0 + INPROJ_NC] = _dot(h, w_ref[:, c0:c0 + INPROJ_NC]).astype(BF16)
    sm_ref[...] = _dot(h, w_ref[:, COL_SM:D_PROJ_PAD])


def _inproj(x, ada_l, w_in_p, n_ctx_tok, lat_len, tm=512):
    t = x.shape[0]
    cond = lambda i: _cond_row(i * tm, n_ctx_tok, lat_len)
    return pl.pallas_call(
        _inproj_kernel,
        grid=(t // tm,),
        in_specs=[
            pl.BlockSpec((tm, D_MODEL), lambda i: (i, 0)),
            pl.BlockSpec((None, N_ADA, D_MODEL), lambda i: (cond(i), 0, 0)),
            pl.BlockSpec((D_MODEL, D_PROJ_PAD), lambda i: (0, 0), pipeline_mode=pl.Buffered(1)),
        ],
        out_specs=[pl.BlockSpec((tm, COL_SM), lambda i: (i, 0)),
                   pl.BlockSpec((tm, D_PROJ_PAD - COL_SM), lambda i: (i, 0))],
        out_shape=[jax.ShapeDtypeStruct((t, COL_SM), BF16),
                   jax.ShapeDtypeStruct((t, D_PROJ_PAD - COL_SM), F32)],
        compiler_params=_cparams(("arbitrary",)),
        name="inproj",
    )(x, ada_l, w_in_p)


MERGE_TM = 512
CTX_SEG = 256


def _merge_kernel(a_ref, m0_ref, m1_ref, m2_ref, odc_ref, odl_ref, ogc_ref, ogl_ref, x_ref, ada_ref, cw_ref,
                  wa_ref, wd_ref, wg_ref, wo_ref, lng_ref, lnb_ref, o_ref, *, n_ctx_tiles):
    i = pl.program_id(0)
    is_ctx = i < n_ctx_tiles
    o_d = jnp.where(is_ctx, odc_ref[...], odl_ref[...])
    o_g = jnp.where(is_ctx, ogc_ref[...], ogl_ref[...])
    seg = jnp.where(i < n_ctx_tiles, CTX_SEG, GRID_W)
    row = lax.broadcasted_iota(jnp.int32, (MERGE_TM, W_A), 0)
    pos = jnp.bitwise_and(row, seg - 1)
    a_x = a_ref[:, 0:W_A].astype(F32)
    a_b = a_ref[:, W_A:2 * W_A].astype(F32)
    a_c = a_ref[:, 2 * W_A:3 * W_A].astype(F32)
    z = a_c * a_x
    z_prev = jnp.where(pos == 0, 0.0, pltpu.roll(z, 1, 0))
    z_next = jnp.where(pos == seg - 1, 0.0, pltpu.roll(z, MERGE_TM - 1, 0))
    y_a = a_b * (cw_ref[0:1, :] * z_prev + cw_ref[1:2, :] * z + cw_ref[2:3, :] * z_next)
    br_a = _dot(y_a.astype(BF16), wa_ref[...])
    br_d = _dot(o_d, wd_ref[...])
    br_g = _dot(o_g, wg_ref[...])
    gate = lambda m_ref: _sigmoid(m_ref[...].astype(F32))
    merged = gate(m0_ref) * br_a + gate(m1_ref) * br_d + gate(m2_ref) * br_g
    y = _dot(merged.astype(BF16), wo_ref[...])
    y = ALPHA * x_ref[...] + ada_ref[5:6, :] * y
    o_ref[...] = _layer_norm(y, lng_ref[...], lnb_ref[...])


def _merge(p, od_ctx, od_lat, og_ctx, og_lat, x, ada_l, conv_a, wa, wd, wg, wo, lng, lnb, n_ctx_tok, lat_len):
    t = x.shape[0]
    tm = MERGE_TM
    n_ctx_tiles = n_ctx_tok // tm
    cond = lambda i: _cond_row(i * tm, n_ctx_tok, lat_len)
    full = lambda shape: pl.BlockSpec(shape, lambda i: (0,) * len(shape))
    ctx_blk = lambda i: (jnp.minimum(i, n_ctx_tiles - 1), 0)
    lat_blk = lambda i: (jnp.maximum(i - n_ctx_tiles, 0), 0)
    mg0 = COL_MG // D_MODEL
    return pl.pallas_call(
        functools.partial(_merge_kernel, n_ctx_tiles=n_ctx_tiles),
        grid=(t // tm,),
        in_specs=[
            pl.BlockSpec((tm, 3 * W_A), lambda i: (i, 0)),
            pl.BlockSpec((tm, D_MODEL), lambda i: (i, mg0)),
            pl.BlockSpec((tm, D_MODEL), lambda i: (i, mg0 + 1)),
            pl.BlockSpec((tm, D_MODEL), lambda i: (i, mg0 + 2)),
            pl.BlockSpec((tm, 512), ctx_blk), pl.BlockSpec((tm, 512), lat_blk),
            pl.BlockSpec((tm, 512), ctx_blk), pl.BlockSpec((tm, 512), lat_blk),
            pl.BlockSpec((tm, D_MODEL), lambda i: (i, 0)),
            pl.BlockSpec((None, N_ADA, D_MODEL), lambda i: (cond(i), 0, 0)),
            full((3, W_A)),
            full((W_A, D_MODEL)), full((512, D_MODEL)), full((512, D_MODEL)), full((D_MODEL, D_MODEL)),
            full((1, D_MODEL)), full((1, D_MODEL)),
        ],
        out_specs=pl.BlockSpec((tm, D_MODEL), lambda i: (i, 0)),
        out_shape=jax.ShapeDtypeStruct((t, D_MODEL), F32),
        compiler_params=_cparams(("arbitrary",)),
        name="merge",
    )(p, p, p, p, od_ctx, od_lat, og_ctx, og_lat, x, ada_l, conv_a, wa, wd, wg, wo,
      lng.reshape(1, D_MODEL), lnb.reshape(1, D_MODEL))


C = CHUNK_D


def _bdot(a, b):
    return lax.dot_general(a, b, (((2,), (1,)), ((0,), (0,))), preferred_element_type=F32)


def _bdot_tn(a, b):
    return lax.dot_general(a, b, (((1,), (1,)), ((0,), (0,))), preferred_element_type=F32)


def _tri_inverse(m, eye, row, col):
    def same_block(shift):
        return jnp.right_shift(row, shift) == jnp.right_shift(col, shift)

    m8 = jnp.where(same_block(3), m, 0.0)
    m16 = m8.astype(BF16)
    x = eye - m8
    sq = _bdot(m16, m16).astype(BF16)
    x = x + _bdot(x.astype(BF16), sq)
    sq = _bdot(sq, sq).astype(BF16)
    x = x + _bdot(x.astype(BF16), sq)
    for shift in (4, 5, 6):
        e = jnp.where(same_block(shift) & jnp.logical_not(same_block(shift - 1)), m, 0.0)
        x16 = x.astype(BF16)
        x = x - _bdot(_bdot(x16, e.astype(BF16)).astype(BF16), x16)
    return x


NHD = 2 * H_D
CB_MAX = 4
HALO = 16


def _delta_kernel(*refs, seq_len, seg, has_init, want_state, n_prev):
    (q_ref, k_ref, v_ref, z_ref, sm_ref, cw_ref, arow_ref, dtb_ref, ng_ref, cum3_ref, tri2_ref, e2_ref) = refs[:12]
    pos = 12
    s0_ref = None
    if has_init:
        s0_ref = refs[pos]
        pos += 1
    prev_ref = None
    if n_prev:
        prev_ref = refs[pos]
        pos += 1
    o_ref = refs[pos]
    pos += 1
    sfin_ref = None
    if want_state:
        sfin_ref = refs[pos]
        pos += 1
    u_scr, wq_scr, at_scr, kd_scr, ls_scr, s_scr, of_scr, ob_scr = refs[pos:]

    n_chunks = seq_len // C
    cb = min(CB_MAX, n_chunks)
    nb = cb * NHD
    row = lax.broadcasted_iota(jnp.int32, (nb, C, C), 1)
    col = lax.broadcasted_iota(jnp.int32, (nb, C, C), 2)
    fwd = jnp.bitwise_and(lax.broadcasted_iota(jnp.int32, (nb, C, C), 0), NHD - 1) < H_D
    dist = jnp.where(fwd, row - col, col - row)
    strict = dist > 0
    incl = dist >= 0
    eye = jnp.where(row == col, 1.0, 0.0).astype(F32)
    row128 = lax.broadcasted_iota(jnp.int32, (C, 128), 0)

    def conv_block(ref, c0, wc0, r0, n):
        x = ref[pl.ds(r0, C), c0:c0 + 128].astype(F32)
        xp = pltpu.roll(x, 1, 0)
        xn = pltpu.roll(x, C - 1, 0)
        if seg == C:
            xp = jnp.where(row128 == 0, 0.0, xp)
            xn = jnp.where(row128 == C - 1, 0.0, xn)
        else:
            prev = ref[pl.ds(pl.multiple_of(jnp.maximum(r0 - HALO, 0), HALO), HALO), c0:c0 + 128].astype(F32)
            nxt = ref[pl.ds(pl.multiple_of(jnp.minimum(r0 + C, seq_len - HALO), HALO), HALO), c0:c0 + 128].astype(F32)
            pm = jnp.where(n > 0, 1.0, 0.0)
            nm = jnp.where(n < n_chunks - 1, 1.0, 0.0)
            xp = jnp.where(row128 == 0, prev[HALO - 1:HALO, :] * pm, xp)
            xn = jnp.where(row128 == C - 1, nxt[0:1, :] * nm, xn)
        y = cw_ref[0:1, wc0:wc0 + 128] * xp + cw_ref[1:2, wc0:wc0 + 128] * x + cw_ref[2:3, wc0:wc0 + 128] * xn
        return _silu(y)

    def l2n(x):
        return x * lax.rsqrt(jnp.sum(x * x, axis=-1, keepdims=True) + RMS_EPS)

    def chunk_body(it, carry):
        qs, ks, vs, kks, qks, betas, gcols, gsums, gtots = [], [], [], [], [], [], [], [], []
        for cc in range(cb):
            n = it * cb + cc
            r0 = pl.multiple_of(n * C, C)
            beta_full = _sigmoid(sm_ref[pl.ds(r0, C), 0:128])
            g_full = -jnp.exp(arow_ref[...]) * _softplus(sm_ref[pl.ds(r0, C), 128:256] + dtb_ref[...])
            cs = _dot(cum3_ref[...], _split3_rows(g_full))
            cols = jnp.concatenate([g_full, beta_full, cs], axis=0)
            chi, clo = _split2(cols)
            bcast = _dot(jnp.concatenate([chi, clo], axis=1), e2_ref[...])
            qh, kh, vh, kkh, qkh = [], [], [], [], []
            for h in range(H_D):
                q = l2n(conv_block(q_ref, h * 128, h * 128, r0, n)) * (DK_D ** -0.5)
                k = l2n(conv_block(k_ref, h * 128, 512 + h * 128, r0, n))
                v = conv_block(v_ref, h * 128, 1024 + h * 128, r0, n)
                k16 = k.astype(BF16)
                qh.append(q)
                kh.append(k)
                vh.append(v)
                kkh.append(_dot_nt(k16, k16))
                qkh.append(_dot_nt(q.astype(BF16), k16))
            for dst, src in ((qs, qh), (ks, kh), (vs, vh), (kks, kkh), (qks, qkh)):
                dst.extend(src + src)
            for b in range(NHD):
                lanes = slice(b * 128, (b + 1) * 128)
                d = b // H_D
                gcols.append(bcast[0:C, lanes])
                betas.append(bcast[C:2 * C, lanes])
                gsums.append(bcast[(2 + d) * C:(3 + d) * C, lanes])
                gtots.append(bcast[4 * C:5 * C, lanes])
        st = lambda xs: jnp.stack(xs, axis=0)
        q, k, v, kk, qk = st(qs), st(ks), st(vs), st(kks), st(qks)
        beta, gcol, gsum, gtot = st(betas), st(gcols), st(gsums), st(gtots)
        xg = jnp.where(strict, gcol[:, :, 0:C], 0.0)
        xh, xl = _split2(xg)
        tri2 = jnp.concatenate([tri2_ref[...]] * cb, axis=0)
        diff = _bdot(tri2, jnp.concatenate([xh, xl], axis=1))
        gamma = jnp.where(incl, jnp.exp(jnp.minimum(diff, 0.0)), 0.0)
        m = jnp.where(strict, beta[:, :, 0:C] * kk * gamma, 0.0)
        eg = jnp.exp(gsum)
        rhs = jnp.concatenate([v * beta, k * (beta * eg)], axis=2)
        sol = _bdot(_tri_inverse(m, eye, row, col).astype(BF16), rhs.astype(BF16))
        wq = jnp.concatenate([sol[:, :, 128:], q * eg], axis=1).astype(BF16)
        at = (qk * gamma).astype(BF16)
        kd = (k * jnp.exp(gtot - gsum)).astype(BF16)
        ls = jnp.exp(gtot[:, 0:8, :])
        for cc in range(cb):
            n = it * cb + cc
            for d in range(2):
                t = n if d == 0 else n_chunks - 1 - n
                dst = pl.ds(t * NHD + d * H_D, H_D)
                src = slice(cc * NHD + d * H_D, cc * NHD + (d + 1) * H_D)
                u_scr[dst] = sol[src, :, :128]
                wq_scr[dst] = wq[src]
                at_scr[dst] = at[src]
                kd_scr[dst] = kd[src]
                ls_scr[dst] = ls[src]
        return carry

    lax.fori_loop(0, n_chunks // cb, chunk_body, 0)

    if has_init:
        s_scr[...] = s0_ref[0]
    else:
        s_scr[...] = jnp.zeros_like(s_scr)

    def scan_body(i, carry):
        slot = pl.ds(i * NHD, NHD)
        s = s_scr[...]
        r = _bdot(wq_scr[slot], s.astype(BF16))
        v16 = (u_scr[slot] - r[:, 0:C]).astype(BF16)
        o = r[:, C:2 * C] + _bdot(at_scr[slot], v16)
        s_scr[...] = s * ls_scr[slot][:, 0:1, :] + _bdot_tn(kd_scr[slot], v16)
        for d in range(2):
            n = i if d == 0 else n_chunks - 1 - i
            r0 = pl.multiple_of(n * C, C)
            o_dst = of_scr if d == 0 else ob_scr
            for h in range(H_D):
                o_dst[pl.ds(r0, C), h * 128:(h + 1) * 128] = o[d * H_D + h]
        return carry

    lax.fori_loop(0, n_chunks, scan_body, 0)

    if want_state:
        if n_prev:
            sfin_ref[0, 0:n_prev] = prev_ref[0]
        sfin_ref[0, n_prev] = s_scr[...]

    def out_body(n, carry):
        r0 = pl.multiple_of(n * C, C)
        for h in range(H_D):
            o = of_scr[pl.ds(r0, C), h * 128:(h + 1) * 128] + ob_scr[pl.ds(r0, C), h * 128:(h + 1) * 128]
            o = o * lax.rsqrt(jnp.mean(o * o, axis=-1, keepdims=True) + RMS_EPS) * ng_ref[...]
            zg = z_ref[pl.ds(r0, C), h * 128:(h + 1) * 128].astype(F32)
            o_ref[pl.ds(r0, C), h * 128:(h + 1) * 128] = (o * _silu(zg)).astype(BF16)
        return carry

    lax.fori_loop(0, n_chunks, out_body, 0)


def _delta(p, sm, consts, conv_qkv, arow, dtb, ng, s0, *, seq_len, n_seq, row_blk0, seg, want_state,
           prev_states=None):
    has_init = s0 is not None
    n_prev = 0 if prev_states is None else prev_states.shape[1]
    n_chunks = seq_len // C
    nhd = 2 * H_D
    cq = COL_DQ // 512
    full = lambda shape: pl.BlockSpec(shape, lambda s: (0,) * len(shape))
    in_specs = [
        pl.BlockSpec((seq_len, 512), lambda s: (s + row_blk0, cq)),
        pl.BlockSpec((seq_len, 512), lambda s: (s + row_blk0, cq + 1)),
        pl.BlockSpec((seq_len, 512), lambda s: (s + row_blk0, cq + 2)),
        pl.BlockSpec((seq_len, 512), lambda s: (s + row_blk0, cq + 3)),
        pl.BlockSpec((seq_len, SM_W), lambda s: (s + row_blk0, 0)),
        full((3, 1536)), full((1, 128)), full((1, 128)), full((1, 128)),
        full((3 * C, 3 * C)), full((NHD, C, 2 * C)), full((2 * 128, NHD * 128)),
    ]
    args = [p, p, p, p, sm, conv_qkv, arow, dtb, ng, consts["cum3_d"], consts["tri2_d"], consts["e2_d"]]
    if has_init:
        in_specs.append(pl.BlockSpec((1, nhd, DK_D, DV_D), lambda s: (s, 0, 0, 0)))
        args.append(s0)
    if n_prev:
        in_specs.append(pl.BlockSpec((1, n_prev, nhd, DK_D, DV_D), lambda s: (s, 0, 0, 0, 0)))
        args.append(prev_states)
    out_specs = [pl.BlockSpec((seq_len, 512), lambda s: (s, 0))]
    out_shape = [jax.ShapeDtypeStruct((n_seq * seq_len, 512), BF16)]
    if want_state:
        out_specs.append(pl.BlockSpec((1, n_prev + 1, nhd, DK_D, DV_D), lambda s: (s, 0, 0, 0, 0)))
        out_shape.append(jax.ShapeDtypeStruct((n_seq, n_prev + 1, nhd, DK_D, DV_D), F32))
    res = pl.pallas_call(
        functools.partial(_delta_kernel, seq_len=seq_len, seg=seg, has_init=has_init, want_state=want_state,
                          n_prev=n_prev),
        grid=(n_seq,),
        in_specs=in_specs,
        out_specs=out_specs,
        out_shape=out_shape,
        scratch_shapes=[
            pltpu.VMEM((nhd * n_chunks, C, 128), F32),
            pltpu.VMEM((nhd * n_chunks, 2 * C, 128), BF16),
            pltpu.VMEM((nhd * n_chunks, C, C), BF16),
            pltpu.VMEM((nhd * n_chunks, C, 128), BF16),
            pltpu.VMEM((nhd * n_chunks, 8, 128), F32),
            pltpu.VMEM((nhd, DK_D, DV_D), F32),
            pltpu.VMEM((seq_len, 512), F32),
            pltpu.VMEM((seq_len, 512), F32),
        ],
        compiler_params=_cparams(("arbitrary",)),
        name="delta_lat" if has_init else "delta_ctx",
    )(*args)
    return res


CG = CHUNK_G
GB = 64
HP = H_G // 2


def _gla_kernel(*refs, seq_len, has_init, want_state, n_prev):
    (q_ref, k_ref, v_ref, r_ref, sm_ref, w2_ref, b_ref, ng_ref, cum3_ref, sele_ref) = refs[:10]
    pos = 10
    s0_ref = None
    if has_init:
        s0_ref = refs[pos]
        pos += 1
    prev_ref = None
    if n_prev:
        prev_ref = refs[pos]
        pos += 1
    o_ref = refs[pos]
    pos += 1
    sfin_ref = None
    if want_state:
        sfin_ref = refs[pos]
        pos += 1
    bc_scr, tot_scr, oacc_scr, ointer_scr = refs[pos:]

    n_blocks = seq_len // GB
    rowi = lax.broadcasted_iota(jnp.int32, (CG, H_G * DK_G), 0)
    half = CG // 2
    rowh = lax.broadcasted_iota(jnp.int32, (half, H_G * DK_G), 0)
    zero_half = jnp.zeros((GB // CG, half, H_G * DK_G), F32)
    qscale = DK_G ** -0.5

    def pairs(x, w):
        return jnp.stack([x[:, p * w:(p + 1) * w] for p in range(HP)], axis=0)

    sm16 = sm_ref[...].astype(BF16)
    for d in range(2):
        logits = _dot(sm16, w2_ref[d]) + b_ref[d]
        la = -_softplus(-logits) * (LOG2_E / GLA_TAU)
        for blk in range(n_blocks):
            cs = _dot(cum3_ref[...], _split3_rows(la[blk * GB:(blk + 1) * GB]))
            bc_scr[d, blk * GB:(blk + 1) * GB, :] = cs[d * GB:(d + 1) * GB]
            tot_scr[d, blk * GB:(blk + 1) * GB, :] = cs[2 * GB:3 * GB]

    def block_body(b, carry):
        r0 = pl.multiple_of(b * GB, GB)
        q = q_ref[pl.ds(r0, GB), :].astype(F32) * qscale
        k = k_ref[pl.ds(r0, GB), :].astype(F32)
        v = v_ref[pl.ds(r0, GB), :].astype(F32)
        nc = GB // CG
        q3, k3, v3 = (t.reshape(nc, CG, t.shape[-1]) for t in (q, k, v))
        o_blk = None
        for d in range(2):
            bc3 = bc_scr[d, pl.ds(r0, GB), :].reshape(nc, CG, H_G * DK_G)
            pieces = []
            for j in range(CG):
                if d == 0:
                    rs = slice(half if j >= half else 0, CG)
                    keep = (rowh >= j - half) if j >= half else (rowi >= j)
                else:
                    rs = slice(0, half if j < half else CG)
                    keep = (rowh <= j) if j < half else (rowi <= j)
                e = jnp.exp2(jnp.minimum(bc3[:, rs] - bc3[:, j:j + 1], 0.0))
                a = jnp.where(keep, q3[:, rs] * e * k3[:, j:j + 1], 0.0)
                if rs.stop - rs.start < CG:
                    a = jnp.concatenate([zero_half, a] if rs.start else [a, zero_half], axis=1)
                pieces.append(a.astype(BF16))
            a_all = jnp.concatenate(pieces, axis=1)
            rr = _dot(a_all.reshape(nc * CG * CG, a_all.shape[-1]), sele_ref[...])
            rr = rr.reshape(nc, CG * CG, rr.shape[-1])
            top = [j for j in range(CG) if d == 1 or j < half]
            bot = [j for j in range(CG) if d == 0 or j >= half]
            acc_t = functools.reduce(jnp.add, [rr[:, j * CG:j * CG + half] * v3[:, j:j + 1] for j in top])
            acc_b = functools.reduce(jnp.add, [rr[:, j * CG + half:(j + 1) * CG] * v3[:, j:j + 1] for j in bot])
            acc = jnp.concatenate([acc_t, acc_b], axis=1)
            o_blk = acc if d == 0 else o_blk + acc
        oacc_scr[pl.ds(r0, GB), :] = o_blk.reshape(GB, o_blk.shape[-1])
        return carry

    lax.fori_loop(0, n_blocks, block_body, 0)

    if has_init:
        st0 = (s0_ref[0, 0:HP], s0_ref[0, HP:2 * HP])
    else:
        st0 = (jnp.zeros((HP, 2 * DV_G, 2 * DK_G), F32),) * 2
    prow = lax.broadcasted_iota(jnp.int32, (2 * DV_G, 2 * DK_G), 0) // DV_G
    pcol = lax.broadcasted_iota(jnp.int32, (2 * DV_G, 2 * DK_G), 1) // DK_G
    pmask = jnp.where(prow == pcol, 1.0, 0.0).astype(F32)

    lane_head = lax.broadcasted_iota(jnp.int32, (1, 2 * DK_G), 1) // DK_G
    head_mask = [jnp.where(lane_head == a, 1.0, 0.0).astype(F32) for a in range(2)]
    nc = GB // CG
    bnt = lambda a, b: lax.dot_general(a, b, (((2,), (2,)), ((0,), (0,))), preferred_element_type=F32)

    nk = GB - CG
    krow = [lax.broadcasted_iota(jnp.int32, (nk, H_G * DK_G), 0) + off for off in (0, CG)]

    def scan_body(i, carry):
        new, inter, r0s, qts, kts, vks = [], [], [], [], [], []
        for d in range(2):
            st = carry[d]
            blk = i if d == 0 else n_blocks - 1 - i
            r0 = pl.multiple_of(blk * GB, GB)
            bcum = bc_scr[d, pl.ds(r0, GB), :]
            tot = tot_scr[d, pl.ds(r0, GB), :]
            q = q_ref[pl.ds(r0, GB), :].astype(F32) * qscale
            k = k_ref[pl.ds(r0, GB), :].astype(F32)
            v16 = v_ref[pl.ds(r0, GB), :]
            qd = pairs(q * jnp.exp2(bcum), 2 * DK_G).astype(BF16)
            kd = pairs(k * jnp.exp2(tot - bcum), 2 * DK_G).astype(BF16)
            o = bnt(qd, st.astype(BF16))
            inter.append(jnp.concatenate([o[p] for p in range(HP)], axis=1))
            r0s.append(r0)
            upd = _bdot_tn(pairs(v16, 2 * DV_G), kd)
            new.append(st * jnp.exp2(pairs(tot, 2 * DK_G)[:, 0:1, :]) + upd * pmask)
            kwin = slice(0, nk) if d == 0 else slice(CG, GB)
            for c in (range(1, nc) if d == 0 else range(nc - 1)):
                rows = slice(c * CG, (c + 1) * CG)
                ref_row = c * CG - 1 if d == 0 else (c + 1) * CG
                valid = (krow[0] < c * CG) if d == 0 else (krow[1] >= (c + 1) * CG)
                bref = bcum[ref_row:ref_row + 1, :]
                qt = q[rows] * jnp.exp2(bcum[rows] - bref)
                kt = jnp.where(valid, k[kwin] * jnp.exp2(jnp.minimum(bref - bcum[kwin], 0.0)), 0.0)
                qts.append(pairs(qt, 2 * DK_G))
                kts.append(pairs(kt, 2 * DK_G).astype(BF16))
                vks.append(pairs(v16[kwin], 2 * DV_G))
        cat = lambda xs: jnp.concatenate(xs, axis=0)
        qt = cat(qts)
        qq = jnp.concatenate([qt * head_mask[0], qt * head_mask[1]], axis=1).astype(BF16)
        attn = bnt(qq, cat(kts)).astype(BF16)
        ov = _bdot(attn, cat(vks))
        for d in range(2):
            out_rows = []
            for c in range(nc):
                rows = slice(c * CG, (c + 1) * CG)
                ci = c - 1 if d == 0 else c
                if ci < 0 or ci >= nc - 1:
                    out_rows.append(inter[d][rows])
                    continue
                base = (d * (nc - 1) + ci) * HP
                cross = jnp.concatenate([ov[base + p, a * CG:(a + 1) * CG, a * DV_G:(a + 1) * DV_G]
                                         for p in range(HP) for a in range(2)], axis=1)
                out_rows.append(inter[d][rows] + cross)
            ointer_scr[d, pl.ds(r0s[d], GB), :] = jnp.concatenate(out_rows, axis=0)
        return tuple(new)

    st_fin = lax.fori_loop(0, n_blocks, scan_body, st0)

    if want_state:
        if n_prev:
            sfin_ref[0, 0:n_prev] = prev_ref[0]
        for d in range(2):
            for p in range(HP):
                t = st_fin[d][p].T
                for a in range(2):
                    sfin_ref[0, n_prev, d * H_G + 2 * p + a] = t[a * DK_G:(a + 1) * DK_G, a * DV_G:(a + 1) * DV_G]

    def out_body(b, carry):
        r0 = pl.multiple_of(b * GB, GB)
        for h in range(H_G):
            lanes = slice(h * DV_G, (h + 1) * DV_G)
            o = oacc_scr[pl.ds(r0, GB), lanes] + ointer_scr[0, pl.ds(r0, GB), lanes] + ointer_scr[1, pl.ds(r0, GB), lanes]
            o = o * lax.rsqrt(jnp.mean(o * o, axis=-1, keepdims=True) + RMS_EPS) * ng_ref[...]
            rg = r_ref[pl.ds(r0, GB), h * DV_G:(h + 1) * DV_G].astype(F32)
            o_ref[pl.ds(r0, GB), h * DV_G:(h + 1) * DV_G] = (o * _silu(rg)).astype(BF16)
        return carry

    lax.fori_loop(0, n_blocks, out_body, 0)


def _gla(p, sm, consts, w2p, gb, ng, s0, *, seq_len, n_seq, row_blk0, want_state, prev_states=None):
    has_init = s0 is not None
    n_prev = 0 if prev_states is None else prev_states.shape[1]
    full = lambda shape: pl.BlockSpec(shape, lambda s: (0,) * len(shape))
    hk, hv = H_G * DK_G, H_G * DV_G
    in_specs = [
        pl.BlockSpec((seq_len, hk), lambda s: (s + row_blk0, COL_GQ // hk)),
        pl.BlockSpec((seq_len, hk), lambda s: (s + row_blk0, COL_GQ // hk + 1)),
        pl.BlockSpec((seq_len, hv), lambda s: (s + row_blk0, (COL_GQ + 2 * hk) // hv)),
        pl.BlockSpec((seq_len, hv), lambda s: (s + row_blk0, (COL_GQ + 2 * hk) // hv + 1)),
        pl.BlockSpec((seq_len, 128), lambda s: (s + row_blk0, 0)),
        full((2, 128, hk)), full((2, 1, hk)), full((1, DV_G)),
        full((3 * GB, 3 * GB)), full((CG * CG, hv)),
    ]
    args = [p, p, p, p, sm, w2p, gb, ng, consts["cum3_d"], consts["sele"]]
    if has_init:
        in_specs.append(pl.BlockSpec((1, 2 * HP, 2 * DV_G, 2 * DK_G), lambda s: (s, 0, 0, 0)))
        args.append(s0)
    if n_prev:
        in_specs.append(pl.BlockSpec((1, n_prev, 2 * H_G, DK_G, DV_G), lambda s: (s, 0, 0, 0, 0)))
        args.append(prev_states)
    out_specs = [pl.BlockSpec((seq_len, hv), lambda s: (s, 0))]
    out_shape = [jax.ShapeDtypeStruct((n_seq * seq_len, hv), BF16)]
    if want_state:
        out_specs.append(pl.BlockSpec((1, n_prev + 1, 2 * H_G, DK_G, DV_G), lambda s: (s, 0, 0, 0, 0)))
        out_shape.append(jax.ShapeDtypeStruct((n_seq, n_prev + 1, 2 * H_G, DK_G, DV_G), F32))
    return pl.pallas_call(
        functools.partial(_gla_kernel, seq_len=seq_len, has_init=has_init, want_state=want_state, n_prev=n_prev),
        grid=(n_seq,),
        in_specs=in_specs,
        out_specs=out_specs,
        out_shape=out_shape,
        scratch_shapes=[
            pltpu.VMEM((2, seq_len, hk), F32),
            pltpu.VMEM((2, seq_len, hk), F32),
            pltpu.VMEM((seq_len, hv), F32),
            pltpu.VMEM((2, seq_len, hv), F32),
        ],
        compiler_params=_cparams(("arbitrary",)),
        name="gla_lat" if has_init else "gla_ctx",
    )(*args)


def _constants():
    i = np.arange(C)
    lo = (i[:, None] >= i[None, :]).astype(np.float32)
    up = (i[:, None] <= i[None, :]).astype(np.float32)
    ones = np.ones((C, C), np.float32)
    cum_d = np.concatenate([lo, up, ones], axis=0)
    hk, hv = H_G * DK_G, H_G * DV_G
    sele = (np.arange(hk)[:, None] // DK_G == np.arange(hv)[None, :] // DV_G).astype(np.float32)
    lane_sel = (np.arange(128)[:, None] == np.arange(NHD * 128)[None, :] // 128).astype(np.float32)
    return {
        "e2_d": jnp.asarray(np.tile(lane_sel, (2, 1)), BF16),
        "cum3_d": jnp.asarray(np.tile(cum_d, (1, 3)), BF16),
        "tri2_d": jnp.asarray(np.stack([np.tile(lo, (1, 2))] * H_D + [np.tile(up, (1, 2))] * H_D), BF16),
        "sele": jnp.asarray(sele, BF16),
    }


W_IN_WIDTHS = (W_A, W_A, W_A, 512, 512, 512, 512, 8, 8, 256, 256, 512, 512, 32, 3 * D_MODEL)
W_IN_OFFS = tuple(int(o) for o in np.concatenate([[0], np.cumsum(W_IN_WIDTHS)]))
W_IN_MOVES = ((W_IN_OFFS[0], W_IN_OFFS[7], COL_A), (W_IN_OFFS[9], W_IN_OFFS[13], COL_GQ), (W_IN_OFFS[14], W_IN_OFFS[15], COL_MG))
PERM_ROWS = 128


def _permute_kernel(w_ref, narrow_ref, o_ref):
    x = w_ref[...]
    for c0, c1, dst in W_IN_MOVES:
        o_ref[:, dst:dst + c1 - c0] = x[:, c0:c1].astype(BF16)
    o_ref[:, COL_SM:D_PROJ_PAD] = narrow_ref[...]


def _permute_w_in(w_in, l):
    w_l = w_in[l]
    seg = lambda a: w_l[:, W_IN_OFFS[a]:W_IN_OFFS[a + 1]]
    zeros = lambda n: jnp.zeros((D_MODEL, n), w_l.dtype)
    narrow = jnp.concatenate([seg(7), zeros(SM_LR - 8), seg(13), zeros(SM_A - SM_LR - 2 * GLA_RANK),
                              seg(8), zeros(SM_W - SM_A - 8)], axis=1).astype(BF16)
    d_in = w_in.shape[-1]
    return pl.pallas_call(
        _permute_kernel,
        grid=(D_MODEL // PERM_ROWS,),
        in_specs=[pl.BlockSpec((None, PERM_ROWS, d_in), lambda r: (l, r, 0)),
                  pl.BlockSpec((PERM_ROWS, SM_W), lambda r: (r, 0))],
        out_specs=pl.BlockSpec((PERM_ROWS, D_PROJ_PAD), lambda r: (r, 0)),
        out_shape=jax.ShapeDtypeStruct((D_MODEL, D_PROJ_PAD), BF16),
        compiler_params=_cparams(("arbitrary",)),
        name="w_in_layout",
    )(w_in, narrow)


def _gla_pack_state(s):
    n = s.shape[0]
    st = jnp.swapaxes(s, -1, -2).reshape(n, 2, HP, 2, DV_G, DK_G)
    packed = jnp.einsum("ndpavk,ab->ndpavbk", st, jnp.eye(2, dtype=s.dtype))
    return packed.reshape(n, 2 * HP, 2 * DV_G, 2 * DK_G)


def _lane_row(vals8, lane0):
    return jnp.zeros((1, 128), F32).at[0, lane0:lane0 + 8].set(vals8.reshape(8).astype(F32))


def kernel(x_prompt, x_sample, state_delta, state_gla, c, c_ctx, w_ada, b_ada, ln_g, ln_b, ffn_w1, ffn_w2, w_in,
           conv_a, conv_qkv, delta_a_log, delta_dt_bias, delta_norm_g, gla_w2, gla_b, gla_norm_g,
           w_br_a, w_br_d, w_br_g, w_o):
    n_ctx, ctx_len, _ = x_prompt.shape
    n_lat, lat_len, _ = x_sample.shape
    n_ctx_tok = n_ctx * ctx_len
    n_lat_tok = n_lat * lat_len
    assert ctx_len == CTX_SEG and MERGE_TM % CTX_SEG == 0 and n_ctx_tok % MERGE_TM == 0
    assert lat_len % MERGE_TM == 0 and n_ctx_tok % lat_len == 0 and n_lat <= 8

    consts = _constants()
    x = (x_prompt.reshape(n_ctx_tok, D_MODEL), x_sample.reshape(n_lat_tok, D_MODEL))
    cond16 = jnp.zeros((16, D_MODEL), F32).at[:n_lat].set(c).at[8].set(c_ctx)
    ada = _ada_table(cond16, w_ada, b_ada)

    sd = sg = None
    for l in range(DEPTH):
        ada_l = ada[l]
        kw = dict(n_ctx_tok=n_ctx_tok, lat_len=lat_len)
        x = _ffn(x, ada_l, ffn_w1, ffn_w2, ln_g[l, 0], ln_b[l, 0], l, 0, **kw)
        p, sm = _inproj(x, ada_l, _permute_w_in(w_in, l), **kw)

        arow = _lane_row(delta_a_log[l], SM_A % 128)
        dtb = _lane_row(delta_dt_bias[l], SM_A % 128)
        ngd = delta_norm_g[l].reshape(1, DV_D)
        od_ctx, sd = _delta(p, sm, consts, conv_qkv[l], arow, dtb, ngd, None, seq_len=ctx_len, n_seq=n_ctx,
                            row_blk0=0, seg=ctx_len, want_state=True, prev_states=sd)
        s0d = state_delta[:, l].astype(F32).reshape(n_lat, 2 * H_D, DK_D, DV_D)
        (od_lat,) = _delta(p, sm, consts, conv_qkv[l], arow, dtb, ngd, s0d, seq_len=lat_len, n_seq=n_lat,
                           row_blk0=n_ctx_tok // lat_len, seg=GRID_W, want_state=False)

        w2p = jnp.zeros((2, 128, H_G * DK_G), F32)
        for d in range(2):
            w2p = w2p.at[d, SM_LR + d * GLA_RANK:SM_LR + (d + 1) * GLA_RANK].set(gla_w2[l, d])
        w2p = w2p.astype(BF16)
        gb = gla_b[l].reshape(2, 1, H_G * DK_G)
        ngg = gla_norm_g[l].reshape(1, DV_G)
        og_ctx, sg = _gla(p, sm, consts, w2p, gb, ngg, None, seq_len=ctx_len, n_seq=n_ctx, row_blk0=0,
                          want_state=True, prev_states=sg)
        s0g = _gla_pack_state(state_gla[:, l].astype(F32))
        (og_lat,) = _gla(p, sm, consts, w2p, gb, ngg, s0g, seq_len=lat_len, n_seq=n_lat,
                         row_blk0=n_ctx_tok // lat_len, want_state=False)

        x = _merge(p, od_ctx, od_lat, og_ctx, og_lat, x, ada_l, conv_a[l], w_br_a[l].astype(BF16), w_br_d[l].astype(BF16),
                   w_br_g[l].astype(BF16), w_o[l].astype(BF16), ln_g[l, 1], ln_b[l, 1], **kw)
        x = _ffn(x, ada_l, ffn_w1, ffn_w2, ln_g[l, 2], ln_b[l, 2], l, 2, split_out=(l == DEPTH - 1), **kw)

    y_prompt = x[0].reshape(n_ctx, ctx_len, D_MODEL)
    y_sample = x[1].reshape(n_lat, lat_len, D_MODEL)
    new_state_delta = sd.reshape(n_ctx, DEPTH, 2, H_D, DK_D, DV_D).astype(x_prompt.dtype)
    new_state_gla = sg.reshape(n_ctx, DEPTH, 2, H_G, DK_G, DV_G).astype(x_prompt.dtype)
    return (y_prompt, y_sample, new_state_delta, new_state_gla)
```

```python
import functools

import jax
import jax.numpy as jnp
import numpy as np
from jax import lax
from jax.experimental import pallas as pl
from jax.experimental.pallas import tpu as pltpu

F32 = jnp.float32
BF16 = jnp.bfloat16

D_MODEL = 1024
DEPTH = 2
GRID_W = 64
D_FF = 2816
W_A = 512
H_D, DK_D, DV_D, CHUNK_D = 4, 128, 128, 64
H_G, DK_G, DV_G, CHUNK_G = 4, 64, 128, 16
GLA_RANK = 16
GLA_TAU = 16.0
N_ADA = 9
ALPHA = float((2 * DEPTH) ** 0.25)
LN_EPS = 1e-5
RMS_EPS = 1e-6
LOG2_E = 1.4426950408889634

COL_A = 0
COL_DQ = 1536
COL_GQ = 3584
COL_MG = 5120
COL_SM = 8192
SM_W = 256
D_PROJ_PAD = COL_SM + SM_W
SM_BETA, SM_LR, SM_A = 0, 16, 128

VMEM_LIMIT = 56 * 1024 * 1024


def _cparams(sem):
    return pltpu.CompilerParams(dimension_semantics=sem, vmem_limit_bytes=VMEM_LIMIT)


def _dot(a, b):
    return jnp.dot(a, b, preferred_element_type=F32)


def _dot_nt(a, b):
    return lax.dot_general(a, b, (((1,), (1,)), ((), ())), preferred_element_type=F32)


def _sigmoid(x):
    return 1.0 / (1.0 + jnp.exp(-x))


def _silu(x):
    return x * _sigmoid(x)


def _softplus(x):
    return jnp.maximum(x, 0.0) + jnp.log1p(jnp.exp(-jnp.abs(x)))


def _split2(x):
    hi = x.astype(BF16)
    lo = (x - hi.astype(F32)).astype(BF16)
    return hi, lo


def _split3_rows(x):
    hi = x.astype(BF16)
    r = x - hi.astype(F32)
    mid = r.astype(BF16)
    lo = (r - mid.astype(F32)).astype(BF16)
    return jnp.concatenate([hi, mid, lo], axis=0)


def _layer_norm(y, g, b):
    mu = jnp.mean(y, axis=-1, keepdims=True)
    yc = y - mu
    var = jnp.mean(yc * yc, axis=-1, keepdims=True)
    return yc * lax.rsqrt(var + LN_EPS) * g + b


def _ada_kernel(cond_ref, w_ref, b_ref, o_ref):
    s = _silu(cond_ref[...]).astype(BF16)
    o_ref[0] = _dot(s, w_ref[0].astype(BF16)) + b_ref[0]


def _ada_table(cond16, w_ada, b_ada):
    n_l = w_ada.shape[0]
    tn = N_ADA * D_MODEL // 4
    out = pl.pallas_call(
        _ada_kernel,
        grid=(n_l, N_ADA * D_MODEL // tn),
        in_specs=[
            pl.BlockSpec((16, D_MODEL), lambda l, j: (0, 0)),
            pl.BlockSpec((1, D_MODEL, tn), lambda l, j: (l, 0, j)),
            pl.BlockSpec((1, 1, tn), lambda l, j: (l, 0, j)),
        ],
        out_specs=pl.BlockSpec((1, 16, tn), lambda l, j: (l, 0, j)),
        out_shape=jax.ShapeDtypeStruct((n_l, 16, N_ADA * D_MODEL), F32),
        compiler_params=_cparams(("arbitrary", "arbitrary")),
        name="ada",
    )(cond16, w_ada, b_ada.reshape(n_l, 1, N_ADA * D_MODEL))
    return out.reshape(n_l, 16, N_ADA, D_MODEL)


def _cond_row(tok0, n_ctx_tok, lat_len):
    return jnp.where(tok0 < n_ctx_tok, 8, (tok0 - n_ctx_tok) // lat_len)


MXU_TILE = 256
FF_SPLIT = (D_FF // MXU_TILE // 2) * MXU_TILE
FF_CHUNKS = ((0, FF_SPLIT), (FF_SPLIT, D_FF))


W1_ROWS = 128
W2_ROWS = 256


def _stage_weight(w_hbm, dst, stage, sem, rows):
    n_chunks = w_hbm.shape[0] // rows

    def copy(c):
        return pltpu.make_async_copy(w_hbm.at[pl.ds(c * rows, rows), :], stage.at[c % 2], sem.at[c % 2])

    copy(0).start()
    for c in range(n_chunks):
        if c + 1 < n_chunks:
            copy(c + 1).start()
        copy(c).wait()
        dst[c * rows:(c + 1) * rows, :] = stage[c % 2].astype(BF16)


def _ffn_kernel(*refs, l, j, n, n_ctx_tiles, split_in, split_out):
    n_x = 4 if split_in else 2
    x_refs = refs[:n_x]
    adac_ref, adap_ref, w1_hbm, w2_hbm, lng_ref, lnb_ref = refs[n_x:n_x + 6]
    n_o = 2 if split_out else 1
    o_refs = refs[n_x + 6:n_x + 6 + n_o]
    y_scr, w1_ref, w2_ref, st1, st2, sem1, sem2 = refs[n_x + 6 + n_o:]
    jj = j // 2
    i = pl.program_id(0)
    prev_is_ctx = jnp.maximum(i - 1, 0) < n_ctx_tiles

    @pl.when(i == 0)
    def _():
        y_scr[...] = jnp.zeros_like(y_scr)
        _stage_weight(w1_hbm.at[l, jj], w1_ref, st1, sem1, W1_ROWS)
        _stage_weight(w2_hbm.at[l, jj], w2_ref, st2, sem2, W2_ROWS)

    if split_in:
        x_cur = jnp.where(jnp.minimum(i, n - 1) < n_ctx_tiles, x_refs[0][...], x_refs[1][...])
        x_prev = jnp.where(prev_is_ctx, x_refs[2][...], x_refs[3][...])
    else:
        x_cur, x_prev = x_refs[0][...], x_refs[1][...]

    h = (x_cur * (1.0 + adac_ref[3 * j + 1:3 * j + 2, :]) + adac_ref[3 * j:3 * j + 1, :]).astype(BF16)
    y_new = None
    for c0, c1 in FF_CHUNKS:
        g = _dot(h, w1_ref[:, c0:c1])
        u = _dot(h, w1_ref[:, D_FF + c0:D_FF + c1])
        part = _dot((_silu(g) * u).astype(BF16), w2_ref[c0:c1, :])
        y_new = part if y_new is None else y_new + part

    y = ALPHA * x_prev + 0.5 * adap_ref[3 * j + 2:3 * j + 3, :] * y_scr[...]
    out = _layer_norm(y, lng_ref[...], lnb_ref[...])
    if split_out:
        @pl.when(prev_is_ctx)
        def _():
            o_refs[0][...] = out

        @pl.when(jnp.logical_not(prev_is_ctx))
        def _():
            o_refs[1][...] = out
    else:
        o_refs[0][...] = out
    y_scr[...] = y_new


def _ffn(x, ada_l, ffn_w1, ffn_w2, lng, lnb, l, j, n_ctx_tok, lat_len, tm=512, split_out=False):
    split_in = isinstance(x, tuple)
    n_ctx_tiles = n_ctx_tok // tm
    t = n_ctx_tok + x[1].shape[0] if split_in else x.shape[0]
    n = t // tm
    cur = lambda i: jnp.minimum(i, n - 1)
    prev = lambda i: jnp.maximum(i - 1, 0)
    ctx_blk = lambda f: (lambda i: (jnp.minimum(f(i), n_ctx_tiles - 1), 0))
    lat_blk = lambda f: (lambda i: (jnp.maximum(f(i) - n_ctx_tiles, 0), 0))
    cond = lambda i: _cond_row(i * tm, n_ctx_tok, lat_len)
    resident = lambda shape: pl.BlockSpec(shape, lambda i: (0,) * len(shape), pipeline_mode=pl.Buffered(1))
    tile = lambda index_map: pl.BlockSpec((tm, D_MODEL), index_map)
    if split_in:
        x_specs = [tile(ctx_blk(cur)), tile(lat_blk(cur)), tile(ctx_blk(prev)), tile(lat_blk(prev))]
        x_args = [x[0], x[1], x[0], x[1]]
    else:
        x_specs = [tile(lambda i: (cur(i), 0)), tile(lambda i: (prev(i), 0))]
        x_args = [x, x]
    if split_out:
        out_specs = [tile(ctx_blk(prev)), tile(lat_blk(prev))]
        out_shape = [jax.ShapeDtypeStruct((n_ctx_tok, D_MODEL), F32), jax.ShapeDtypeStruct((t - n_ctx_tok, D_MODEL), F32)]
    else:
        out_specs = tile(lambda i: (prev(i), 0))
        out_shape = jax.ShapeDtypeStruct((t, D_MODEL), F32)
    return pl.pallas_call(
        functools.partial(_ffn_kernel, l=l, j=j, n=n, n_ctx_tiles=n_ctx_tiles, split_in=split_in, split_out=split_out),
        grid=(n + 1,),
        in_specs=x_specs + [
            pl.BlockSpec((None, N_ADA, D_MODEL), lambda i: (cond(cur(i)), 0, 0)),
            pl.BlockSpec((None, N_ADA, D_MODEL), lambda i: (cond(prev(i)), 0, 0)),
            pl.BlockSpec(memory_space=pl.ANY),
            pl.BlockSpec(memory_space=pl.ANY),
            resident((1, D_MODEL)),
            resident((1, D_MODEL)),
        ],
        out_specs=out_specs,
        out_shape=out_shape,
        scratch_shapes=[
            pltpu.VMEM((tm, D_MODEL), F32),
            pltpu.VMEM((D_MODEL, 2 * D_FF), BF16),
            pltpu.VMEM((D_FF, D_MODEL), BF16),
            pltpu.VMEM((2, W1_ROWS, 2 * D_FF), F32),
            pltpu.VMEM((2, W2_ROWS, D_MODEL), F32),
            pltpu.SemaphoreType.DMA((2,)),
            pltpu.SemaphoreType.DMA((2,)),
        ],
        compiler_params=_cparams(("arbitrary",)),
        name="ffn",
    )(*x_args, ada_l, ada_l, ffn_w1, ffn_w2, lng.reshape(1, D_MODEL), lnb.reshape(1, D_MODEL))


INPROJ_NC = 8 * MXU_TILE


def _inproj_kernel(x_ref, ada_ref, w_ref, p_ref, sm_ref):
    h = (x_ref[...] * (1.0 + ada_ref[4:5, :]) + ada_ref[3:4, :]).astype(BF16)
    for c0 in range(0, COL_SM, INPROJ_NC):
        p_ref[:, c0:c0 + INPROJ_NC] = _dot(h, w_ref[:, c0:c0 + INPROJ_NC]).astype(BF16)
    sm_ref[...] = _dot(h, w_ref[:, COL_SM:D_PROJ_PAD])


def _inproj(x, ada_l, w_in_p, n_ctx_tok, lat_len, tm=512):
    t = x.shape[0]
    cond = lambda i: _cond_row(i * tm, n_ctx_tok, lat_len)
    return pl.pallas_call(
        _inproj_kernel,
        grid=(t // tm,),
        in_specs=[
            pl.BlockSpec((tm, D_MODEL), lambda i: (i, 0)),
            pl.BlockSpec((None, N_ADA, D_MODEL), lambda i: (cond(i), 0, 0)),
            pl.BlockSpec((D_MODEL, D_PROJ_PAD), lambda i: (0, 0), pipeline_mode=pl.Buffered(1)),
        ],
        out_specs=[pl.BlockSpec((tm, COL_SM), lambda i: (i, 0)),
                   pl.BlockSpec((tm, D_PROJ_PAD - COL_SM), lambda i: (i, 0))],
        out_shape=[jax.ShapeDtypeStruct((t, COL_SM), BF16),
                   jax.ShapeDtypeStruct((t, D_PROJ_PAD - COL_SM), F32)],
        compiler_params=_cparams(("arbitrary",)),
        name="inproj",
    )(x, ada_l, w_in_p)


MERGE_TM = 512
CTX_SEG = 256


def _merge_kernel(a_ref, m0_ref, m1_ref, m2_ref, odc_ref, odl_ref, ogc_ref, ogl_ref, x_ref, ada_ref, cw_ref,
                  wa_ref, wd_ref, wg_ref, wo_ref, lng_ref, lnb_ref, o_ref, *, n_ctx_tiles):
    i = pl.program_id(0)
    is_ctx = i < n_ctx_tiles
    o_d = jnp.where(is_ctx, odc_ref[...], odl_ref[...])
    o_g = jnp.where(is_ctx, ogc_ref[...], ogl_ref[...])
    seg = jnp.where(i < n_ctx_tiles, CTX_SEG, GRID_W)
    row = lax.broadcasted_iota(jnp.int32, (MERGE_TM, W_A), 0)
    pos = jnp.bitwise_and(row, seg - 1)
    a_x = a_ref[:, 0:W_A].astype(F32)
    a_b = a_ref[:, W_A:2 * W_A].astype(F32)
    a_c = a_ref[:, 2 * W_A:3 * W_A].astype(F32)
    z = a_c * a_x
    z_prev = jnp.where(pos == 0, 0.0, pltpu.roll(z, 1, 0))
    z_next = jnp.where(pos == seg - 1, 0.0, pltpu.roll(z, MERGE_TM - 1, 0))
    y_a = a_b * (cw_ref[0:1, :] * z_prev + cw_ref[1:2, :] * z + cw_ref[2:3, :] * z_next)
    br_a = _dot(y_a.astype(BF16), wa_ref[...])
    br_d = _dot(o_d, wd_ref[...])
    br_g = _dot(o_g, wg_ref[...])
    gate = lambda m_ref: _sigmoid(m_ref[...].astype(F32))
    merged = gate(m0_ref) * br_a + gate(m1_ref) * br_d + gate(m2_ref) * br_g
    y = _dot(merged.astype(BF16), wo_ref[...])
    y = ALPHA * x_ref[...] + ada_ref[5:6, :] * y
    o_ref[...] = _layer_norm(y, lng_ref[...], lnb_ref[...])


def _merge(p, od_ctx, od_lat, og_ctx, og_lat, x, ada_l, conv_a, wa, wd, wg, wo, lng, lnb, n_ctx_tok, lat_len):
    t = x.shape[0]
    tm = MERGE_TM
    n_ctx_tiles = n_ctx_tok // tm
    cond = lambda i: _cond_row(i * tm, n_ctx_tok, lat_len)
    full = lambda shape: pl.BlockSpec(shape, lambda i: (0,) * len(shape))
    ctx_blk = lambda i: (jnp.minimum(i, n_ctx_tiles - 1), 0)
    lat_blk = lambda i: (jnp.maximum(i - n_ctx_tiles, 0), 0)
    mg0 = COL_MG // D_MODEL
    return pl.pallas_call(
        functools.partial(_merge_kernel, n_ctx_tiles=n_ctx_tiles),
        grid=(t // tm,),
        in_specs=[
            pl.BlockSpec((tm, 3 * W_A), lambda i: (i, 0)),
            pl.BlockSpec((tm, D_MODEL), lambda i: (i, mg0)),
            pl.BlockSpec((tm, D_MODEL), lambda i: (i, mg0 + 1)),
            pl.BlockSpec((tm, D_MODEL), lambda i: (i, mg0 + 2)),
            pl.BlockSpec((tm, 512), ctx_blk), pl.BlockSpec((tm, 512), lat_blk),
            pl.BlockSpec((tm, 512), ctx_blk), pl.BlockSpec((tm, 512), lat_blk),
            pl.BlockSpec((tm, D_MODEL), lambda i: (i, 0)),
            pl.BlockSpec((None, N_ADA, D_MODEL), lambda i: (cond(i), 0, 0)),
            full((3, W_A)),
            full((W_A, D_MODEL)), full((512, D_MODEL)), full((512, D_MODEL)), full((D_MODEL, D_MODEL)),
            full((1, D_MODEL)), full((1, D_MODEL)),
        ],
        out_specs=pl.BlockSpec((tm, D_MODEL), lambda i: (i, 0)),
        out_shape=jax.ShapeDtypeStruct((t, D_MODEL), F32),
        compiler_params=_cparams(("arbitrary",)),
        name="merge",
    )(p, p, p, p, od_ctx, od_lat, og_ctx, og_lat, x, ada_l, conv_a, wa, wd, wg, wo,
      lng.reshape(1, D_MODEL), lnb.reshape(1, D_MODEL))


C = CHUNK_D


def _bdot(a, b):
    return lax.dot_general(a, b, (((2,), (1,)), ((0,), (0,))), preferred_element_type=F32)


def _bdot_tn(a, b):
    return lax.dot_general(a, b, (((1,), (1,)), ((0,), (0,))), preferred_element_type=F32)


def _tri_inverse(m, eye, row, col):
    def same_block(shift):
        return jnp.right_shift(row, shift) == jnp.right_shift(col, shift)

    m8 = jnp.where(same_block(3), m, 0.0)
    m16 = m8.astype(BF16)
    x = eye - m8
    sq = _bdot(m16, m16).astype(BF16)
    x = x + _bdot(x.astype(BF16), sq)
    sq = _bdot(sq, sq).astype(BF16)
    x = x + _bdot(x.astype(BF16), sq)
    for shift in (4, 5, 6):
        e = jnp.where(same_block(shift) & jnp.logical_not(same_block(shift - 1)), m, 0.0)
        x16 = x.astype(BF16)
        x = x - _bdot(_bdot(x16, e.astype(BF16)).astype(BF16), x16)
    return x


NHD = 2 * H_D
CB_MAX = 4
HALO = 16


def _delta_kernel(*refs, seq_len, seg, has_init, want_state, n_prev):
    (q_ref, k_ref, v_ref, z_ref, sm_ref, cw_ref, arow_ref, dtb_ref, ng_ref, cum3_ref, tri2_ref, e2_ref) = refs[:12]
    pos = 12
    s0_ref = None
    if has_init:
        s0_ref = refs[pos]
        pos += 1
    prev_ref = None
    if n_prev:
        prev_ref = refs[pos]
        pos += 1
    o_ref = refs[pos]
    pos += 1
    sfin_ref = None
    if want_state:
        sfin_ref = refs[pos]
        pos += 1
    u_scr, wq_scr, at_scr, kd_scr, ls_scr, s_scr, of_scr, ob_scr = refs[pos:]

    n_chunks = seq_len // C
    cb = min(CB_MAX, n_chunks)
    nb = cb * NHD
    row = lax.broadcasted_iota(jnp.int32, (nb, C, C), 1)
    col = lax.broadcasted_iota(jnp.int32, (nb, C, C), 2)
    fwd = jnp.bitwise_and(lax.broadcasted_iota(jnp.int32, (nb, C, C), 0), NHD - 1) < H_D
    dist = jnp.where(fwd, row - col, col - row)
    strict = dist > 0
    incl = dist >= 0
    eye = jnp.where(row == col, 1.0, 0.0).astype(F32)
    row128 = lax.broadcasted_iota(jnp.int32, (C, 128), 0)

    def conv_block(ref, c0, wc0, r0, n):
        x = ref[pl.ds(r0, C), c0:c0 + 128].astype(F32)
        xp = pltpu.roll(x, 1, 0)
        xn = pltpu.roll(x, C - 1, 0)
        if seg == C:
            xp = jnp.where(row128 == 0, 0.0, xp)
            xn = jnp.where(row128 == C - 1, 0.0, xn)
        else:
            prev = ref[pl.ds(pl.multiple_of(jnp.maximum(r0 - HALO, 0), HALO), HALO), c0:c0 + 128].astype(F32)
            nxt = ref[pl.ds(pl.multiple_of(jnp.minimum(r0 + C, seq_len - HALO), HALO), HALO), c0:c0 + 128].astype(F32)
            pm = jnp.where(n > 0, 1.0, 0.0)
            nm = jnp.where(n < n_chunks - 1, 1.0, 0.0)
            xp = jnp.where(row128 == 0, prev[HALO - 1:HALO, :] * pm, xp)
            xn = jnp.where(row128 == C - 1, nxt[0:1, :] * nm, xn)
        y = cw_ref[0:1, wc0:wc0 + 128] * xp + cw_ref[1:2, wc0:wc0 + 128] * x + cw_ref[2:3, wc0:wc0 + 128] * xn
        return _silu(y)

    def l2n(x):
        return x * lax.rsqrt(jnp.sum(x * x, axis=-1, keepdims=True) + RMS_EPS)

    def chunk_body(it, carry):
        qs, ks, vs, kks, qks, betas, gcols, gsums, gtots = [], [], [], [], [], [], [], [], []
        for cc in range(cb):
            n = it * cb + cc
            r0 = pl.multiple_of(n * C, C)
            beta_full = _sigmoid(sm_ref[pl.ds(r0, C), 0:128])
            g_full = -jnp.exp(arow_ref[...]) * _softplus(sm_ref[pl.ds(r0, C), 128:256] + dtb_ref[...])
            cs = _dot(cum3_ref[...], _split3_rows(g_full))
            cols = jnp.concatenate([g_full, beta_full, cs], axis=0)
            chi, clo = _split2(cols)
            bcast = _dot(jnp.concatenate([chi, clo], axis=1), e2_ref[...])
            qh, kh, vh, kkh, qkh = [], [], [], [], []
            for h in range(H_D):
                q = l2n(conv_block(q_ref, h * 128, h * 128, r0, n)) * (DK_D ** -0.5)
                k = l2n(conv_block(k_ref, h * 128, 512 + h * 128, r0, n))
                v = conv_block(v_ref, h * 128, 1024 + h * 128, r0, n)
                k16 = k.astype(BF16)
                qh.append(q)
                kh.append(k)
                vh.append(v)
                kkh.append(_dot_nt(k16, k16))
                qkh.append(_dot_nt(q.astype(BF16), k16))
            for dst, src in ((qs, qh), (ks, kh), (vs, vh), (kks, kkh), (qks, qkh)):
                dst.extend(src + src)
            for b in range(NHD):
                lanes = slice(b * 128, (b + 1) * 128)
                d = b // H_D
                gcols.append(bcast[0:C, lanes])
                betas.append(bcast[C:2 * C, lanes])
                gsums.append(bcast[(2 + d) * C:(3 + d) * C, lanes])
                gtots.append(bcast[4 * C:5 * C, lanes])
        st = lambda xs: jnp.stack(xs, axis=0)
        q, k, v, kk, qk = st(qs), st(ks), st(vs), st(kks), st(qks)
        beta, gcol, gsum, gtot = st(betas), st(gcols), st(gsums), st(gtots)
        xg = jnp.where(strict, gcol[:, :, 0:C], 0.0)
        xh, xl = _split2(xg)
        tri2 = jnp.concatenate([tri2_ref[...]] * cb, axis=0)
        diff = _bdot(tri2, jnp.concatenate([xh, xl], axis=1))
        gamma = jnp.where(incl, jnp.exp(jnp.minimum(diff, 0.0)), 0.0)
        m = jnp.where(strict, beta[:, :, 0:C] * kk * gamma, 0.0)
        eg = jnp.exp(gsum)
        rhs = jnp.concatenate([v * beta, k * (beta * eg)], axis=2)
        sol = _bdot(_tri_inverse(m, eye, row, col).astype(BF16), rhs.astype(BF16))
        wq = jnp.concatenate([sol[:, :, 128:], q * eg], axis=1).astype(BF16)
        at = (qk * gamma).astype(BF16)
        kd = (k * jnp.exp(gtot - gsum)).astype(BF16)
        ls = jnp.exp(gtot[:, 0:8, :])
        for cc in range(cb):
            n = it * cb + cc
            for d in range(2):
                t = n if d == 0 else n_chunks - 1 - n
                dst = pl.ds(t * NHD + d * H_D, H_D)
                src = slice(cc * NHD + d * H_D, cc * NHD + (d + 1) * H_D)
                u_scr[dst] = sol[src, :, :128]
                wq_scr[dst] = wq[src]
                at_scr[dst] = at[src]
                kd_scr[dst] = kd[src]
                ls_scr[dst] = ls[src]
        return carry

    lax.fori_loop(0, n_chunks // cb, chunk_body, 0)

    if has_init:
        s_scr[...] = s0_ref[0]
    else:
        s_scr[...] = jnp.zeros_like(s_scr)

    def scan_body(i, carry):
        slot = pl.ds(i * NHD, NHD)
        s = s_scr[...]
        r = _bdot(wq_scr[slot], s.astype(BF16))
        v16 = (u_scr[slot] - r[:, 0:C]).astype(BF16)
        o = r[:, C:2 * C] + _bdot(at_scr[slot], v16)
        s_scr[...] = s * ls_scr[slot][:, 0:1, :] + _bdot_tn(kd_scr[slot], v16)
        for d in range(2):
            n = i if d == 0 else n_chunks - 1 - i
            r0 = pl.multiple_of(n * C, C)
            o_dst = of_scr if d == 0 else ob_scr
            for h in range(H_D):
                o_dst[pl.ds(r0, C), h * 128:(h + 1) * 128] = o[d * H_D + h]
        return carry

    lax.fori_loop(0, n_chunks, scan_body, 0)

    if want_state:
        if n_prev:
            sfin_ref[0, 0:n_prev] = prev_ref[0]
        sfin_ref[0, n_prev] = s_scr[...]

    def out_body(n, carry):
        r0 = pl.multiple_of(n * C, C)
        for h in range(H_D):
            o = of_scr[pl.ds(r0, C), h * 128:(h + 1) * 128] + ob_scr[pl.ds(r0, C), h * 128:(h + 1) * 128]
            o = o * lax.rsqrt(jnp.mean(o * o, axis=-1, keepdims=True) + RMS_EPS) * ng_ref[...]
            zg = z_ref[pl.ds(r0, C), h * 128:(h + 1) * 128].astype(F32)
            o_ref[pl.ds(r0, C), h * 128:(h + 1) * 128] = (o * _silu(zg)).astype(BF16)
        return carry

    lax.fori_loop(0, n_chunks, out_body, 0)


def _delta(p, sm, consts, conv_qkv, arow, dtb, ng, s0, *, seq_len, n_seq, row_blk0, seg, want_state,
           prev_states=None):
    has_init = s0 is not None
    n_prev = 0 if prev_states is None else prev_states.shape[1]
    n_chunks = seq_len // C
    nhd = 2 * H_D
    cq = COL_DQ // 512
    full = lambda shape: pl.BlockSpec(shape, lambda s: (0,) * len(shape))
    in_specs = [
        pl.BlockSpec((seq_len, 512), lambda s: (s + row_blk0, cq)),
        pl.BlockSpec((seq_len, 512), lambda s: (s + row_blk0, cq + 1)),
        pl.BlockSpec((seq_len, 512), lambda s: (s + row_blk0, cq + 2)),
        pl.BlockSpec((seq_len, 512), lambda s: (s + row_blk0, cq + 3)),
        pl.BlockSpec((seq_len, SM_W), lambda s: (s + row_blk0, 0)),
        full((3, 1536)), full((1, 128)), full((1, 128)), full((1, 128)),
        full((3 * C, 3 * C)), full((NHD, C, 2 * C)), full((2 * 128, NHD * 128)),
    ]
    args = [p, p, p, p, sm, conv_qkv, arow, dtb, ng, consts["cum3_d"], consts["tri2_d"], consts["e2_d"]]
    if has_init:
        in_specs.append(pl.BlockSpec((1, nhd, DK_D, DV_D), lambda s: (s, 0, 0, 0)))
        args.append(s0)
    if n_prev:
        in_specs.append(pl.BlockSpec((1, n_prev, nhd, DK_D, DV_D), lambda s: (s, 0, 0, 0, 0)))
        args.append(prev_states)
    out_specs = [pl.BlockSpec((seq_len, 512), lambda s: (s, 0))]
    out_shape = [jax.ShapeDtypeStruct((n_seq * seq_len, 512), BF16)]
    if want_state:
        out_specs.append(pl.BlockSpec((1, n_prev + 1, nhd, DK_D, DV_D), lambda s: (s, 0, 0, 0, 0)))
        out_shape.append(jax.ShapeDtypeStruct((n_seq, n_prev + 1, nhd, DK_D, DV_D), F32))
    res = pl.pallas_call(
        functools.partial(_delta_kernel, seq_len=seq_len, seg=seg, has_init=has_init, want_state=want_state,
                          n_prev=n_prev),
        grid=(n_seq,),
        in_specs=in_specs,
        out_specs=out_specs,
        out_shape=out_shape,
        scratch_shapes=[
            pltpu.VMEM((nhd * n_chunks, C, 128), F32),
            pltpu.VMEM((nhd * n_chunks, 2 * C, 128), BF16),
            pltpu.VMEM((nhd * n_chunks, C, C), BF16),
            pltpu.VMEM((nhd * n_chunks, C, 128), BF16),
            pltpu.VMEM((nhd * n_chunks, 8, 128), F32),
            pltpu.VMEM((nhd, DK_D, DV_D), F32),
            pltpu.VMEM((seq_len, 512), F32),
            pltpu.VMEM((seq_len, 512), F32),
        ],
        compiler_params=_cparams(("arbitrary",)),
        name="delta_lat" if has_init else "delta_ctx",
    )(*args)
    return res


CG = CHUNK_G
GB = 64
HP = H_G // 2


def _gla_kernel(*refs, seq_len, has_init, want_state, n_prev):
    (q_ref, k_ref, v_ref, r_ref, sm_ref, w2_ref, b_ref, ng_ref, cum3_ref, sele_ref) = refs[:10]
    pos = 10
    s0_ref = None
    if has_init:
        s0_ref = refs[pos]
        pos += 1
    prev_ref = None
    if n_prev:
        prev_ref = refs[pos]
        pos += 1
    o_ref = refs[pos]
    pos += 1
    sfin_ref = None
    if want_state:
        sfin_ref = refs[pos]
        pos += 1
    bc_scr, tot_scr, oacc_scr, ointer_scr = refs[pos:]

    n_blocks = seq_len // GB
    rowi = lax.broadcasted_iota(jnp.int32, (CG, H_G * DK_G), 0)
    half = CG // 2
    rowh = lax.broadcasted_iota(jnp.int32, (half, H_G * DK_G), 0)
    zero_half = jnp.zeros((GB // CG, half, H_G * DK_G), F32)
    qscale = DK_G ** -0.5

    def pairs(x, w):
        return jnp.stack([x[:, p * w:(p + 1) * w] for p in range(HP)], axis=0)

    sm16 = sm_ref[...].astype(BF16)
    for d in range(2):
        logits = _dot(sm16, w2_ref[d]) + b_ref[d]
        la = -_softplus(-logits) * (LOG2_E / GLA_TAU)
        for blk in range(n_blocks):
            cs = _dot(cum3_ref[...], _split3_rows(la[blk * GB:(blk + 1) * GB]))
            bc_scr[d, blk * GB:(blk + 1) * GB, :] = cs[d * GB:(d + 1) * GB]
            tot_scr[d, blk * GB:(blk + 1) * GB, :] = cs[2 * GB:3 * GB]

    def block_body(b, carry):
        r0 = pl.multiple_of(b * GB, GB)
        q = q_ref[pl.ds(r0, GB), :].astype(F32) * qscale
        k = k_ref[pl.ds(r0, GB), :].astype(F32)
        v = v_ref[pl.ds(r0, GB), :].astype(F32)
        nc = GB // CG
        q3, k3, v3 = (t.reshape(nc, CG, t.shape[-1]) for t in (q, k, v))
        o_blk = None
        for d in range(2):
            bc3 = bc_scr[d, pl.ds(r0, GB), :].reshape(nc, CG, H_G * DK_G)
            pieces = []
            for j in range(CG):
                if d == 0:
                    rs = slice(half if j >= half else 0, CG)
                    keep = (rowh >= j - half) if j >= half else (rowi >= j)
                else:
                    rs = slice(0, half if j < half else CG)
                    keep = (rowh <= j) if j < half else (rowi <= j)
                e = jnp.exp2(jnp.minimum(bc3[:, rs] - bc3[:, j:j + 1], 0.0))
                a = jnp.where(keep, q3[:, rs] * e * k3[:, j:j + 1], 0.0)
                if rs.stop - rs.start < CG:
                    a = jnp.concatenate([zero_half, a] if rs.start else [a, zero_half], axis=1)
                pieces.append(a.astype(BF16))
            a_all = jnp.concatenate(pieces, axis=1)
            rr = _dot(a_all.reshape(nc * CG * CG, a_all.shape[-1]), sele_ref[...])
            rr = rr.reshape(nc, CG * CG, rr.shape[-1])
            top = [j for j in range(CG) if d == 1 or j < half]
            bot = [j for j in range(CG) if d == 0 or j >= half]
            acc_t = functools.reduce(jnp.add, [rr[:, j * CG:j * CG + half] * v3[:, j:j + 1] for j in top])
            acc_b = functools.reduce(jnp.add, [rr[:, j * CG + half:(j + 1) * CG] * v3[:, j:j + 1] for j in bot])
            acc = jnp.concatenate([acc_t, acc_b], axis=1)
            o_blk = acc if d == 0 else o_blk + acc
        oacc_scr[pl.ds(r0, GB), :] = o_blk.reshape(GB, o_blk.shape[-1])
        return carry

    lax.fori_loop(0, n_blocks, block_body, 0)

    if has_init:
        st0 = (s0_ref[0, 0:HP], s0_ref[0, HP:2 * HP])
    else:
        st0 = (jnp.zeros((HP, 2 * DV_G, 2 * DK_G), F32),) * 2
    prow = lax.broadcasted_iota(jnp.int32, (2 * DV_G, 2 * DK_G), 0) // DV_G
    pcol = lax.broadcasted_iota(jnp.int32, (2 * DV_G, 2 * DK_G), 1) // DK_G
    pmask = jnp.where(prow == pcol, 1.0, 0.0).astype(F32)

    lane_head = lax.broadcasted_iota(jnp.int32, (1, 2 * DK_G), 1) // DK_G
    head_mask = [jnp.where(lane_head == a, 1.0, 0.0).astype(F32) for a in range(2)]
    nc = GB // CG
    bnt = lambda a, b: lax.dot_general(a, b, (((2,), (2,)), ((0,), (0,))), preferred_element_type=F32)

    nk = GB - CG
    krow = [lax.broadcasted_iota(jnp.int32, (nk, H_G * DK_G), 0) + off for off in (0, CG)]

    def scan_body(i, carry):
        new, inter, r0s, qts, kts, vks = [], [], [], [], [], []
        for d in range(2):
            st = carry[d]
            blk = i if d == 0 else n_blocks - 1 - i
            r0 = pl.multiple_of(blk * GB, GB)
            bcum = bc_scr[d, pl.ds(r0, GB), :]
            tot = tot_scr[d, pl.ds(r0, GB), :]
            q = q_ref[pl.ds(r0, GB), :].astype(F32) * qscale
            k = k_ref[pl.ds(r0, GB), :].astype(F32)
            v16 = v_ref[pl.ds(r0, GB), :]
            qd = pairs(q * jnp.exp2(bcum), 2 * DK_G).astype(BF16)
            kd = pairs(k * jnp.exp2(tot - bcum), 2 * DK_G).astype(BF16)
            o = bnt(qd, st.astype(BF16))
            inter.append(jnp.concatenate([o[p] for p in range(HP)], axis=1))
            r0s.append(r0)
            upd = _bdot_tn(pairs(v16, 2 * DV_G), kd)
            new.append(st * jnp.exp2(pairs(tot, 2 * DK_G)[:, 0:1, :]) + upd * pmask)
            kwin = slice(0, nk) if d == 0 else slice(CG, GB)
            for c in (range(1, nc) if d == 0 else range(nc - 1)):
                rows = slice(c * CG, (c + 1) * CG)
                ref_row = c * CG - 1 if d == 0 else (c + 1) * CG
                valid = (krow[0] < c * CG) if d == 0 else (krow[1] >= (c + 1) * CG)
                bref = bcum[ref_row:ref_row + 1, :]
                qt = q[rows] * jnp.exp2(bcum[rows] - bref)
                kt = jnp.where(valid, k[kwin] * jnp.exp2(jnp.minimum(bref - bcum[kwin], 0.0)), 0.0)
                qts.append(pairs(qt, 2 * DK_G))
                kts.append(pairs(kt, 2 * DK_G).astype(BF16))
                vks.append(pairs(v16[kwin], 2 * DV_G))
        cat = lambda xs: jnp.concatenate(xs, axis=0)
        qt = cat(qts)
        qq = jnp.concatenate([qt * head_mask[0], qt * head_mask[1]], axis=1).astype(BF16)
        attn = bnt(qq, cat(kts)).astype(BF16)
        ov = _bdot(attn, cat(vks))
        for d in range(2):
            out_rows = []
            for c in range(nc):
                rows = slice(c * CG, (c + 1) * CG)
                ci = c - 1 if d == 0 else c
                if ci < 0 or ci >= nc - 1:
                    out_rows.append(inter[d][rows])
                    continue
                base = (d * (nc - 1) + ci) * HP
                cross = jnp.concatenate([ov[base + p, a * CG:(a + 1) * CG, a * DV_G:(a + 1) * DV_G]
                                         for p in range(HP) for a in range(2)], axis=1)
                out_rows.append(inter[d][rows] + cross)
            ointer_scr[d, pl.ds(r0s[d], GB), :] = jnp.concatenate(out_rows, axis=0)
        return tuple(new)

    st_fin = lax.fori_loop(0, n_blocks, scan_body, st0)

    if want_state:
        if n_prev:
            sfin_ref[0, 0:n_prev] = prev_ref[0]
        for d in range(2):
            for p in range(HP):
                t = st_fin[d][p].T
                for a in range(2):
                    sfin_ref[0, n_prev, d * H_G + 2 * p + a] = t[a * DK_G:(a + 1) * DK_G, a * DV_G:(a + 1) * DV_G]

    def out_body(b, carry):
        r0 = pl.multiple_of(b * GB, GB)
        for h in range(H_G):
            lanes = slice(h * DV_G, (h + 1) * DV_G)
            o = oacc_scr[pl.ds(r0, GB), lanes] + ointer_scr[0, pl.ds(r0, GB), lanes] + ointer_scr[1, pl.ds(r0, GB), lanes]
            o = o * lax.rsqrt(jnp.mean(o * o, axis=-1, keepdims=True) + RMS_EPS) * ng_ref[...]
            rg = r_ref[pl.ds(r0, GB), h * DV_G:(h + 1) * DV_G].astype(F32)
            o_ref[pl.ds(r0, GB), h * DV_G:(h + 1) * DV_G] = (o * _silu(rg)).astype(BF16)
        return carry

    lax.fori_loop(0, n_blocks, out_body, 0)


def _gla(p, sm, consts, w2p, gb, ng, s0, *, seq_len, n_seq, row_blk0, want_state, prev_states=None):
    has_init = s0 is not None
    n_prev = 0 if prev_states is None else prev_states.shape[1]
    full = lambda shape: pl.BlockSpec(shape, lambda s: (0,) * len(shape))
    hk, hv = H_G * DK_G, H_G * DV_G
    in_specs = [
        pl.BlockSpec((seq_len, hk), lambda s: (s + row_blk0, COL_GQ // hk)),
        pl.BlockSpec((seq_len, hk), lambda s: (s + row_blk0, COL_GQ // hk + 1)),
        pl.BlockSpec((seq_len, hv), lambda s: (s + row_blk0, (COL_GQ + 2 * hk) // hv)),
        pl.BlockSpec((seq_len, hv), lambda s: (s + row_blk0, (COL_GQ + 2 * hk) // hv + 1)),
        pl.BlockSpec((seq_len, 128), lambda s: (s + row_blk0, 0)),
        full((2, 128, hk)), full((2, 1, hk)), full((1, DV_G)),
        full((3 * GB, 3 * GB)), full((CG * CG, hv)),
    ]
    args = [p, p, p, p, sm, w2p, gb, ng, consts["cum3_d"], consts["sele"]]
    if has_init:
        in_specs.append(pl.BlockSpec((1, 2 * HP, 2 * DV_G, 2 * DK_G), lambda s: (s, 0, 0, 0)))
        args.append(s0)
    if n_prev:
        in_specs.append(pl.BlockSpec((1, n_prev, 2 * H_G, DK_G, DV_G), lambda s: (s, 0, 0, 0, 0)))
        args.append(prev_states)
    out_specs = [pl.BlockSpec((seq_len, hv), lambda s: (s, 0))]
    out_shape = [jax.ShapeDtypeStruct((n_seq * seq_len, hv), BF16)]
    if want_state:
        out_specs.append(pl.BlockSpec((1, n_prev + 1, 2 * H_G, DK_G, DV_G), lambda s: (s, 0, 0, 0, 0)))
        out_shape.append(jax.ShapeDtypeStruct((n_seq, n_prev + 1, 2 * H_G, DK_G, DV_G), F32))
    return pl.pallas_call(
        functools.partial(_gla_kernel, seq_len=seq_len, has_init=has_init, want_state=want_state, n_prev=n_prev),
        grid=(n_seq,),
        in_specs=in_specs,
        out_specs=out_specs,
        out_shape=out_shape,
        scratch_shapes=[
            pltpu.VMEM((2, seq_len, hk), F32),
            pltpu.VMEM((2, seq_len, hk), F32),
            pltpu.VMEM((seq_len, hv), F32),
            pltpu.VMEM((2, seq_len, hv), F32),
        ],
        compiler_params=_cparams(("arbitrary",)),
        name="gla_lat" if has_init else "gla_ctx",
    )(*args)


def _constants():
    i = np.arange(C)
    lo = (i[:, None] >= i[None, :]).astype(np.float32)
    up = (i[:, None] <= i[None, :]).astype(np.float32)
    ones = np.ones((C, C), np.float32)
    cum_d = np.concatenate([lo, up, ones], axis=0)
    hk, hv = H_G * DK_G, H_G * DV_G
    sele = (np.arange(hk)[:, None] // DK_G == np.arange(hv)[None, :] // DV_G).astype(np.float32)
    lane_sel = (np.arange(128)[:, None] == np.arange(NHD * 128)[None, :] // 128).astype(np.float32)
    return {
        "e2_d": jnp.asarray(np.tile(lane_sel, (2, 1)), BF16),
        "cum3_d": jnp.asarray(np.tile(cum_d, (1, 3)), BF16),
        "tri2_d": jnp.asarray(np.stack([np.tile(lo, (1, 2))] * H_D + [np.tile(up, (1, 2))] * H_D), BF16),
        "sele": jnp.asarray(sele, BF16),
    }


W_IN_WIDTHS = (W_A, W_A, W_A, 512, 512, 512, 512, 8, 8, 256, 256, 512, 512, 32, 3 * D_MODEL)
W_IN_OFFS = tuple(int(o) for o in np.concatenate([[0], np.cumsum(W_IN_WIDTHS)]))
W_IN_MOVES = ((W_IN_OFFS[0], W_IN_OFFS[7], COL_A), (W_IN_OFFS[9], W_IN_OFFS[13], COL_GQ), (W_IN_OFFS[14], W_IN_OFFS[15], COL_MG))
PERM_ROWS = 128


def _permute_kernel(w_ref, o_ref):
    x = w_ref[...]
    for c0, c1, dst in W_IN_MOVES:
        o_ref[:, dst:dst + c1 - c0] = x[:, c0:c1].astype(BF16)
    gates = x[:, W_IN_OFFS[7]:W_IN_OFFS[7] + 128]
    lr0 = W_IN_OFFS[13] - SM_LR
    lrs = x[:, lr0:lr0 + 128]
    lane = lax.broadcasted_iota(jnp.int32, gates.shape, 1)
    group0 = jnp.where(lane < 8, gates, jnp.where((lane >= SM_LR) & (lane < SM_LR + 2 * GLA_RANK), lrs, 0.0))
    group1 = jnp.where(lane < 8, pltpu.roll(gates, 128 - 8, 1), 0.0)
    o_ref[:, COL_SM:COL_SM + 128] = group0.astype(BF16)
    o_ref[:, COL_SM + 128:D_PROJ_PAD] = group1.astype(BF16)


def _permute_w_in(w_in, l):
    assert W_IN_OFFS[7] % 128 == 0 and (W_IN_OFFS[13] - SM_LR) % 128 == 0 and SM_A == 128 and SM_W == 256
    d_in = w_in.shape[-1]
    return pl.pallas_call(
        _permute_kernel,
        grid=(D_MODEL // PERM_ROWS,),
        in_specs=[pl.BlockSpec((None, PERM_ROWS, d_in), lambda r: (l, r, 0))],
        out_specs=pl.BlockSpec((PERM_ROWS, D_PROJ_PAD), lambda r: (r, 0)),
        out_shape=jax.ShapeDtypeStruct((D_MODEL, D_PROJ_PAD), BF16),
        compiler_params=_cparams(("arbitrary",)),
        name="w_in_layout",
    )(w_in)


def _gla_pack_state(s):
    n = s.shape[0]
    st = jnp.swapaxes(s, -1, -2).reshape(n, 2, HP, 2, DV_G, DK_G)
    packed = jnp.einsum("ndpavk,ab->ndpavbk", st, jnp.eye(2, dtype=s.dtype))
    return packed.reshape(n, 2 * HP, 2 * DV_G, 2 * DK_G)


def _lane_row(vals8, lane0):
    return jnp.zeros((1, 128), F32).at[0, lane0:lane0 + 8].set(vals8.reshape(8).astype(F32))


def kernel(x_prompt, x_sample, state_delta, state_gla, c, c_ctx, w_ada, b_ada, ln_g, ln_b, ffn_w1, ffn_w2, w_in,
           conv_a, conv_qkv, delta_a_log, delta_dt_bias, delta_norm_g, gla_w2, gla_b, gla_norm_g,
           w_br_a, w_br_d, w_br_g, w_o):
    n_ctx, ctx_len, _ = x_prompt.shape
    n_lat, lat_len, _ = x_sample.shape
    n_ctx_tok = n_ctx * ctx_len
    n_lat_tok = n_lat * lat_len
    assert ctx_len == CTX_SEG and MERGE_TM % CTX_SEG == 0 and n_ctx_tok % MERGE_TM == 0
    assert lat_len % MERGE_TM == 0 and n_ctx_tok % lat_len == 0 and n_lat <= 8

    consts = _constants()
    x = (x_prompt.reshape(n_ctx_tok, D_MODEL), x_sample.reshape(n_lat_tok, D_MODEL))
    cond16 = jnp.zeros((16, D_MODEL), F32).at[:n_lat].set(c).at[8].set(c_ctx)
    ada = _ada_table(cond16, w_ada, b_ada)

    sd = sg = None
    for l in range(DEPTH):
        ada_l = ada[l]
        kw = dict(n_ctx_tok=n_ctx_tok, lat_len=lat_len)
        x = _ffn(x, ada_l, ffn_w1, ffn_w2, ln_g[l, 0], ln_b[l, 0], l, 0, **kw)
        p, sm = _inproj(x, ada_l, _permute_w_in(w_in, l), **kw)

        arow = _lane_row(delta_a_log[l], SM_A % 128)
        dtb = _lane_row(delta_dt_bias[l], SM_A % 128)
        ngd = delta_norm_g[l].reshape(1, DV_D)
        od_ctx, sd = _delta(p, sm, consts, conv_qkv[l], arow, dtb, ngd, None, seq_len=ctx_len, n_seq=n_ctx,
                            row_blk0=0, seg=ctx_len, want_state=True, prev_states=sd)
        s0d = state_delta[:, l].astype(F32).reshape(n_lat, 2 * H_D, DK_D, DV_D)
        (od_lat,) = _delta(p, sm, consts, conv_qkv[l], arow, dtb, ngd, s0d, seq_len=lat_len, n_seq=n_lat,
                           row_blk0=n_ctx_tok // lat_len, seg=GRID_W, want_state=False)

        w2p = jnp.zeros((2, 128, H_G * DK_G), F32)
        for d in range(2):
            w2p = w2p.at[d, SM_LR + d * GLA_RANK:SM_LR + (d + 1) * GLA_RANK].set(gla_w2[l, d])
        w2p = w2p.astype(BF16)
        gb = gla_b[l].reshape(2, 1, H_G * DK_G)
        ngg = gla_norm_g[l].reshape(1, DV_G)
        og_ctx, sg = _gla(p, sm, consts, w2p, gb, ngg, None, seq_len=ctx_len, n_seq=n_ctx, row_blk0=0,
                          want_state=True, prev_states=sg)
        s0g = _gla_pack_state(state_gla[:, l].astype(F32))
        (og_lat,) = _gla(p, sm, consts, w2p, gb, ngg, s0g, seq_len=lat_len, n_seq=n_lat,
                         row_blk0=n_ctx_tok // lat_len, want_state=False)

        x = _merge(p, od_ctx, od_lat, og_ctx, og_lat, x, ada_l, conv_a[l], w_br_a[l].astype(BF16), w_br_d[l].astype(BF16),
                   w_br_g[l].astype(BF16), w_o[l].astype(BF16), ln_g[l, 1], ln_b[l, 1], **kw)
        x = _ffn(x, ada_l, ffn_w1, ffn_w2, ln_g[l, 2], ln_b[l, 2], l, 2, split_out=(l == DEPTH - 1), **kw)

    y_prompt = x[0].reshape(n_ctx, ctx_len, D_MODEL)
    y_sample = x[1].reshape(n_lat, lat_len, D_MODEL)
    new_state_delta = sd.reshape(n_ctx, DEPTH, 2, H_D, DK_D, DV_D).astype(x_prompt.dtype)
    new_state_gla = sg.reshape(n_ctx, DEPTH, 2, H_G, DK_G, DV_G).astype(x_prompt.dtype)
    return (y_prompt, y_sample, new_state_delta, new_state_gla)
```
